```python
import math
import jax
import jax.numpy as jnp
from jax import lax
import numpy as np

D_MODEL = 2048
BATCH = 2
SEQ = 8192
DEPTH = 4
DEC_BATCH = 16
DEC_SEQ = 16
PAST_LEN = 2048

CHUNK = 64
Q_BLOCK = 128
N_MIXERS = 3
MEM_W = D_MODEL // 4
TOK_W = D_MODEL - MEM_W
H_MEM = 4
HD = MEM_W // H_MEM
N_MEM = 256
H_A = TOK_W // HD
HKV_A = 4
GQA = H_A // HKV_A
H_IDX = 8
D_IDX = 64
K_SEL_MAX = 256
H_B = TOK_W // HD
HC = TOK_W // (2 * HD)
N_BIAS = H_A
NUM_BUCKETS = 32
MAX_DISTANCE = 128
N_EXPERTS = 32
TOP_K = 4
D_FF = D_MODEL
SWIGLU_LIMIT = 7.0
SWIGLU_ALPHA = 1.702
MOE_BLOCK = 128
LN_EPS = 1e-5
DEEPNORM_ALPHA = (2 * DEPTH) ** 0.25
DEEPNORM_BETA = (8 * DEPTH) ** -0.25
N_A = len(range(0, DEPTH, N_MIXERS))
N_B = len(range(1, DEPTH, N_MIXERS))
N_C = len(range(2, DEPTH, N_MIXERS))
A_SIZES = (H_A * HD, HKV_A * HD, HKV_A * HD, H_IDX * D_IDX, D_IDX, H_IDX)
B_SIZES = (H_B * HD,) * 3
C_SIZES = (2 * HC * HD,) * 3
COLS_A = sum(A_SIZES) + MEM_W
COLS_B = sum(B_SIZES) + MEM_W
COLS_C = sum(C_SIZES) + MEM_W

kernel_name = 'chunk_causal_hybrid_dsa_stickbreak_diff_moe_step'


def split_cols(h, sizes):
    return jnp.split(h, np.cumsum(sizes)[:-1].tolist(), axis=-1)


def layer_norm(x, g, b):
    xf = x.astype(jnp.float32)
    mu = jnp.mean(xf, -1, keepdims=True)
    var = jnp.mean(jnp.square(xf - mu), -1, keepdims=True)
    return ((xf - mu) * lax.rsqrt(var + LN_EPS) * g + b).astype(x.dtype)


def rms_norm(x, g):
    xf = x.astype(jnp.float32)
    return (xf * lax.rsqrt(jnp.mean(jnp.square(xf), -1, keepdims=True) + LN_EPS) * g).astype(x.dtype)


def chunk_allowed(q_pos, k_pos):
    return (k_pos[None, :] // CHUNK) <= (q_pos[:, None] // CHUNK)


def t5_bucket(rel):
    half = NUM_BUCKETS // 2
    exact = half // 2
    n = jnp.abs(rel)
    far = exact + (jnp.log(jnp.maximum(n, 1).astype(jnp.float32) / exact)
                   / math.log(MAX_DISTANCE / exact) * (half - exact)).astype(jnp.int32)
    return jnp.where(rel > 0, half, 0) + jnp.where(n < exact, n, jnp.minimum(far, half - 1))


def sweep_blocks(fn, q_args, q_pos):
    t_ = q_pos.shape[0]
    if t_ <= Q_BLOCK:
        return fn(*q_args, q_pos)
    nb = t_ // Q_BLOCK
    blk = lambda a: jnp.moveaxis(a.reshape(a.shape[0], nb, Q_BLOCK, *a.shape[2:]), 1, 0)
    out = lax.map(lambda xs: fn(*xs), tuple(blk(a) for a in q_args) + (q_pos.reshape(nb, Q_BLOCK),))
    out = jnp.moveaxis(out, 0, 1)
    return out.reshape(out.shape[0], t_, *out.shape[3:])


def with_past(new, past):
    if past is None:
        return new
    return tuple(jnp.concatenate([p_, n_], axis=1) for p_, n_ in zip(past, new))


def dsa_attend(q, qi, wi, q_pos, k, v, ki, k_pos, rel_bias, k_sel):
    b_, tq = q.shape[:2]
    f32 = jnp.float32
    dots = jnp.einsum('bqhd,bsd->bqhs', qi.astype(f32), ki.astype(f32)) * (D_IDX ** -0.5)
    score = jnp.einsum('bqhs,bqh->bqs', jax.nn.relu(dots), wi.astype(f32)) * (H_IDX ** -0.5)
    score = jnp.where(chunk_allowed(q_pos, k_pos)[None], score, -jnp.inf)
    top_score, idx = lax.top_k(score, k_sel)
    valid = jnp.isfinite(top_score)
    take = jax.vmap(lambda a, ix: a[ix])
    k_rows, v_rows = take(k, idx), take(v, idx)
    qg = q.reshape(b_, tq, HKV_A, GQA, HD)
    logits = jnp.einsum('bqgrd,bqkgd->bqgrk', qg, k_rows).astype(f32) * (HD ** -0.5)
    bias = rel_bias[t5_bucket(k_pos[idx] - q_pos[None, :, None])]
    bias = jnp.moveaxis(bias.reshape(b_, tq, k_sel, HKV_A, GQA), 2, -1)
    logits = jnp.where(valid[:, :, None, None, :], logits + bias, -jnp.inf)
    p = jax.nn.softmax(logits, axis=-1).astype(v.dtype)
    o = jnp.einsum('bqgrk,bqkgd->bqgrd', p, v_rows)
    return o.reshape(b_, tq, H_A * HD)


def stick_attend(q, q_pos, k, v, k_pos):
    z = jnp.einsum('bqhd,bshd->bhqs', q, k).astype(jnp.float32) * (HD ** -0.5)
    before = k_pos[None, :] < q_pos[:, None]
    log_keep = jnp.where(before, jax.nn.log_sigmoid(-z), 0.0)
    tail = lax.cumsum(log_keep, axis=3, reverse=True)
    tail = jnp.concatenate([tail[..., 1:], jnp.zeros_like(tail[..., :1])], axis=-1)
    log_w = jnp.where(before, jax.nn.log_sigmoid(z) + tail, -jnp.inf)
    o = jnp.einsum('bhqs,bshd->bqhd', jnp.exp(log_w).astype(v.dtype), v)
    return o.reshape(o.shape[0], o.shape[1], -1)


def diff_attend(q, q_pos, k, v, k_pos, rel_bias, lam, lambda_init, subln_g):
    logits = jnp.einsum('bqhmd,bshmd->bhmqs', q, k).astype(jnp.float32) * (HD ** -0.5)
    bias = rel_bias[t5_bucket(k_pos[None, :] - q_pos[:, None])]
    bias = jnp.moveaxis(bias, -1, 0).reshape(HC, 2, q_pos.shape[0], k_pos.shape[0])
    logits = jnp.where(chunk_allowed(q_pos, k_pos), logits + bias, -jnp.inf)
    p = jax.nn.softmax(logits, axis=-1)
    w = (p[:, :, 0] - lam * p[:, :, 1]).astype(v.dtype)
    o = jnp.einsum('bhqs,bshe->bqhe', w, v)
    o = rms_norm(o, subln_g) * (1.0 - lambda_init)
    return o.reshape(o.shape[0], o.shape[1], -1)


def memory_attend(q, mk, mv):
    logits = jnp.einsum('bqhd,bmhd->bhqm', q, mk).astype(jnp.float32) * (HD ** -0.5)
    p = jax.nn.softmax(logits, axis=-1).astype(mv.dtype)
    o = jnp.einsum('bhqm,bmhd->bqhd', p, mv)
    return o.reshape(o.shape[0], o.shape[1], -1)


def mixer_dsa(h, q_pos, k_pos, past, rel_bias, k_sel):
    b_, t_ = h.shape[:2]
    q, k, v, qi, ki, wi = split_cols(h, A_SIZES)
    q = q.reshape(b_, t_, H_A, HD)
    qi = qi.reshape(b_, t_, H_IDX, D_IDX)
    new = (k.reshape(b_, t_, HKV_A, HD), v.reshape(b_, t_, HKV_A, HD), ki)
    k_all, v_all, ki_all = with_past(new, past)
    fn = lambda qb, qib, wib, pos: dsa_attend(qb, qib, wib, pos, k_all, v_all, ki_all, k_pos, rel_bias, k_sel)
    return sweep_blocks(fn, (q, qi, wi), q_pos), new


def mixer_stick(h, q_pos, k_pos, past):
    b_, t_ = h.shape[:2]
    q, k, v = (a.reshape(b_, t_, H_B, HD) for a in split_cols(h, B_SIZES))
    new = (k, v)
    k_all, v_all = with_past(new, past)
    fn = lambda qb, pos: stick_attend(qb, pos, k_all, v_all, k_pos)
    return sweep_blocks(fn, (q,), q_pos), new


def mixer_diff(h, q_pos, k_pos, past, rel_bias, lam_vec, subln_g, layer_idx):
    b_, t_ = h.shape[:2]
    q, k, v = split_cols(h, C_SIZES)
    q = q.reshape(b_, t_, HC, 2, HD)
    new = (k.reshape(b_, t_, HC, 2, HD), v.reshape(b_, t_, HC, 2 * HD))
    k_all, v_all = with_past(new, past)
    lambda_init = 0.8 - 0.6 * math.exp(-0.3 * layer_idx)
    lv = lam_vec.astype(jnp.float32)
    lam = jnp.exp(jnp.sum(lv[0] * lv[1])) - jnp.exp(jnp.sum(lv[2] * lv[3])) + lambda_init
    fn = lambda qb, pos: diff_attend(qb, pos, k_all, v_all, k_pos, rel_bias, lam, lambda_init, subln_g)
    return sweep_blocks(fn, (q,), q_pos), new


def moe(x, w_router, b_router, w_gate_up, b_gate_up, w_down, b_down):
    b_, t_, d = x.shape
    xt = x.reshape(-1, d)
    n = xt.shape[0]
    logits = (xt @ w_router).astype(jnp.float32) + b_router
    top_logit, top_e = lax.top_k(logits, TOP_K)
    gate = jax.nn.softmax(top_logit, axis=-1).astype(xt.dtype)
    n_assign = n * TOP_K
    flat_e = top_e.reshape(-1)
    order = jnp.argsort(flat_e)
    sorted_e = flat_e[order]
    counts = jnp.bincount(flat_e, length=N_EXPERTS)
    padded = (counts + MOE_BLOCK - 1) // MOE_BLOCK * MOE_BLOCK
    start = jnp.cumsum(counts) - counts
    pend = jnp.cumsum(padded)
    pstart = pend - padded
    dest = pstart[sorted_e] + jnp.arange(n_assign) - start[sorted_e]
    n_blocks = -(-(n_assign + N_EXPERTS * (MOE_BLOCK - 1)) // MOE_BLOCK)
    n_rows = n_blocks * MOE_BLOCK
    row_token = jnp.full((n_rows,), n, jnp.int32).at[dest].set((order // TOP_K).astype(jnp.int32))
    row_gate = jnp.zeros((n_rows,), xt.dtype).at[dest].set(gate.reshape(-1)[order])
    block_e = jnp.minimum(jnp.searchsorted(pend, jnp.arange(n_blocks) * MOE_BLOCK, side='right'), N_EXPERTS - 1)
    rows = jnp.concatenate([xt, jnp.zeros((1, d), xt.dtype)])[row_token].reshape(n_blocks, MOE_BLOCK, d)

    def expert_block(args):
        xb, e = args
        hcat = xb @ w_gate_up[e] + b_gate_up[e]
        glu, lin = jnp.split(hcat, 2, axis=-1)
        glu = jnp.minimum(glu, SWIGLU_LIMIT)
        lin = jnp.clip(lin, -SWIGLU_LIMIT, SWIGLU_LIMIT)
        return (glu * jax.nn.sigmoid(SWIGLU_ALPHA * glu) * (lin + 1.0)) @ w_down[e] + b_down[e]

    out_rows = lax.map(expert_block, (rows, block_e)).reshape(n_rows, d)
    out = jax.ops.segment_sum(out_rows * row_gate[:, None], row_token, num_segments=n + 1)
    return out[:n].reshape(b_, t_, d)


def trunk(x, q_pos, k_pos, k_sel, pasts, mem_kv, p):
    new_rows = []
    for i in range(DEPTH):
        kind, j = i % N_MIXERS, i // N_MIXERS
        b_, t_ = x.shape[:2]
        h = x @ p['w_in'][kind][j]
        h_mix, mem_q = h[..., :-MEM_W], h[..., -MEM_W:]
        if kind == 0:
            mix, rows = mixer_dsa(h_mix, q_pos, k_pos, pasts[i], p['rel_bias'], k_sel)
        elif kind == 1:
            mix, rows = mixer_stick(h_mix, q_pos, k_pos, pasts[i])
        else:
            mix, rows = mixer_diff(h_mix, q_pos, k_pos, pasts[i], p['rel_bias'],
                                   p['diff_lambda'][j], p['diff_subln'][j], i)
        mem_o = memory_attend(mem_q.reshape(b_, t_, H_MEM, HD), *mem_kv[i])
        attn = jnp.concatenate([mix, mem_o], axis=-1) @ p['w_o'][i]
        x = layer_norm(DEEPNORM_ALPHA * x + attn, p['ln_g'][i, 0], p['ln_b'][i, 0])
        ffn = moe(x, p['w_router'][i], p['b_router'][i], p['w_gate_up'][i], p['b_gate_up'][i],
                  p['w_down'][i], p['b_down'][i])
        x = layer_norm(DEEPNORM_ALPHA * x + ffn, p['ln_g'][i, 1], p['ln_b'][i, 1])
        new_rows.append(rows)
    return x, new_rows


def stack_kind(rows, kind):
    sel = [r for i, r in enumerate(rows) if i % N_MIXERS == kind]
    return [jnp.stack(parts) for parts in zip(*sel)]


def setup_inputs(seed: int = 0) -> dict:
    key = jax.random.key(seed)
    ks = iter(jax.random.split(key, 40))

    def nrm(shape, scale=1.0):
        return jax.random.normal(next(ks), shape, jnp.float32) * scale

    s_in = D_MODEL ** -0.5
    return {
        'x_prompt': nrm((BATCH, SEQ, D_MODEL)),
        'x_sample': nrm((DEC_BATCH, DEC_SEQ, D_MODEL)),
        'cache_a_k': nrm((N_A, DEC_BATCH, PAST_LEN, HKV_A, HD)),
        'cache_a_v': nrm((N_A, DEC_BATCH, PAST_LEN, HKV_A, HD)),
        'cache_a_kidx': nrm((N_A, DEC_BATCH, PAST_LEN, D_IDX)),
        'cache_b_k': nrm((N_B, DEC_BATCH, PAST_LEN, H_B, HD)),
        'cache_b_v': nrm((N_B, DEC_BATCH, PAST_LEN, H_B, HD)),
        'cache_c_k': nrm((N_C, DEC_BATCH, PAST_LEN, HC, 2, HD)),
        'cache_c_v': nrm((N_C, DEC_BATCH, PAST_LEN, HC, 2 * HD)),
        'cache_mem_k': nrm((DEPTH, DEC_BATCH, N_MEM, H_MEM, HD)),
        'cache_mem_v': nrm((DEPTH, DEC_BATCH, N_MEM, H_MEM, HD)),
        'mem_prompt': nrm((BATCH, N_MEM, D_MODEL)),
        'w_in_a': nrm((N_A, D_MODEL, COLS_A), s_in),
        'w_in_b': nrm((N_B, D_MODEL, COLS_B), s_in),
        'w_in_c': nrm((N_C, D_MODEL, COLS_C), s_in),
        'diff_lambda': nrm((N_C, 4, HD), 0.1),
        'diff_subln': 1.0 + nrm((N_C, 2 * HD), 0.02),
        'rel_bias': nrm((NUM_BUCKETS, N_BIAS), 0.2),
        'w_mem_kv': nrm((DEPTH, D_MODEL, 2 * MEM_W), s_in),
        'w_o': nrm((DEPTH, D_MODEL, D_MODEL), s_in * DEEPNORM_BETA),
        'ln_g': 1.0 + nrm((DEPTH, 2, D_MODEL), 0.02),
        'ln_b': nrm((DEPTH, 2, D_MODEL), 0.02),
        'w_router': nrm((DEPTH, D_MODEL, N_EXPERTS), s_in),
        'b_router': nrm((DEPTH, N_EXPERTS), 0.01),
        'w_gate_up': nrm((DEPTH, N_EXPERTS, D_MODEL, 2 * D_FF), s_in),
        'b_gate_up': nrm((DEPTH, N_EXPERTS, 2 * D_FF), 0.01),
        'w_down': nrm((DEPTH, N_EXPERTS, D_FF, D_MODEL), D_FF ** -0.5 * DEEPNORM_BETA),
        'b_down': nrm((DEPTH, N_EXPERTS, D_MODEL), 0.01),
    }


def reference(x_prompt, x_sample, cache_a_k, cache_a_v, cache_a_kidx, cache_b_k, cache_b_v,
              cache_c_k, cache_c_v, cache_mem_k, cache_mem_v, mem_prompt,
              w_in_a, w_in_b, w_in_c, diff_lambda, diff_subln, rel_bias, w_mem_kv, w_o,
              ln_g, ln_b, w_router, b_router, w_gate_up, b_gate_up, w_down, b_down):
    p = {'w_in': (w_in_a, w_in_b, w_in_c), 'diff_lambda': diff_lambda, 'diff_subln': diff_subln,
         'rel_bias': rel_bias, 'w_o': w_o, 'ln_g': ln_g, 'ln_b': ln_b, 'w_router': w_router,
         'b_router': b_router, 'w_gate_up': w_gate_up, 'b_gate_up': b_gate_up,
         'w_down': w_down, 'b_down': b_down}

    seq = x_prompt.shape[1]
    pos_p = jnp.arange(seq, dtype=jnp.int32)
    mem_kv_p = []
    for i in range(DEPTH):
        mk, mv = jnp.split(mem_prompt @ w_mem_kv[i], 2, axis=-1)
        mem_kv_p.append((mk.reshape(mk.shape[0], mk.shape[1], H_MEM, HD),
                         mv.reshape(mv.shape[0], mv.shape[1], H_MEM, HD)))
    y_prompt, rows_p = trunk(x_prompt, pos_p, pos_p, min(K_SEL_MAX, seq // 4),
                             [None] * DEPTH, mem_kv_p, p)

    past_len, dec_seq = cache_a_k.shape[2], x_sample.shape[1]
    k_pos_s = jnp.arange(past_len + dec_seq, dtype=jnp.int32)
    caches = ((cache_a_k, cache_a_v, cache_a_kidx), (cache_b_k, cache_b_v), (cache_c_k, cache_c_v))
    pasts = [tuple(c[i // N_MIXERS] for c in caches[i % N_MIXERS]) for i in range(DEPTH)]
    mem_kv_s = [(cache_mem_k[i], cache_mem_v[i]) for i in range(DEPTH)]
    y_sample, rows_s = trunk(x_sample, k_pos_s[past_len:], k_pos_s,
                             min(K_SEL_MAX, (past_len + dec_seq) // 4), pasts, mem_kv_s, p)

    a_k_p, a_v_p, a_kidx_p = stack_kind(rows_p, 0)
    b_k_p, b_v_p = stack_kind(rows_p, 1)
    c_k_p, c_v_p = stack_kind(rows_p, 2)
    mem_k_p = jnp.stack([kv[0] for kv in mem_kv_p])
    mem_v_p = jnp.stack([kv[1] for kv in mem_kv_p])
    a_k_s, a_v_s, a_kidx_s = stack_kind(rows_s, 0)
    b_k_s, b_v_s = stack_kind(rows_s, 1)
    c_k_s, c_v_s = stack_kind(rows_s, 2)
    return (y_prompt, y_sample, a_k_p, a_v_p, a_kidx_p, b_k_p, b_v_p, c_k_p, c_v_p, mem_k_p, mem_v_p,
            a_k_s, a_v_s, a_kidx_s, b_k_s, b_v_s, c_k_s, c_v_s)
```

```python
import functools
import math

import jax
import jax.numpy as jnp
from jax import lax
from jax.experimental import pallas as pl
from jax.experimental.pallas import tpu as pltpu

F32 = jnp.float32
BF16 = jnp.bfloat16
I32 = jnp.int32

CHUNK = 64
K_SEL_MAX = 256
MAX_DISTANCE = 128
TOP_K = 4
SWIGLU_LIMIT = 7.0
SWIGLU_ALPHA = 1.702
LN_EPS = 1e-5
N_MIXERS = 3

LANE = 128
VMEM_LIMIT = 56 * 1024 * 1024
MASKED = -1e30
EXP_UNDERFLOW = -104.0
INT_MIN = -2 ** 31
KEY_NEG_INF = -2139095041


def _cparams(n_axes):
    return pltpu.CompilerParams(dimension_semantics=("arbitrary",) * n_axes,
                                vmem_limit_bytes=VMEM_LIMIT)


def _round_up(x, m):
    return -(-x // m) * m


def _pick_tile(n, candidates):
    for c in candidates:
        if n % c == 0:
            return c
    return n


def _dot_nt(a, b):
    return lax.dot_general(a, b, (((1,), (1,)), ((), ())), preferred_element_type=F32)


def _mm_kernel(x_ref, w_ref, of_ref, ob_ref):
    acc = jnp.dot(x_ref[...], w_ref[...], preferred_element_type=F32)
    of_ref[...] = acc
    ob_ref[...] = acc.astype(BF16)


def _matmul(x, w):
    m, k = x.shape
    n = w.shape[1]
    tm = _pick_tile(m, (640, 512, 256, 128, 64, 32, 16))
    tn = _pick_tile(n, (1280, 1024, 768, 512, 384, 256, 128))
    return pl.pallas_call(
        _mm_kernel,
        grid=(n // tn, m // tm),
        in_specs=[pl.BlockSpec((tm, k), lambda j, i: (i, 0)),
                  pl.BlockSpec((k, tn), lambda j, i: (0, j))],
        out_specs=[pl.BlockSpec((tm, tn), lambda j, i: (i, j)),
                   pl.BlockSpec((tm, tn), lambda j, i: (i, j))],
        out_shape=[jax.ShapeDtypeStruct((m, n), F32), jax.ShapeDtypeStruct((m, n), BF16)],
        compiler_params=_cparams(2),
        name="matmul",
    )(x, w)


def _layer_norm_rows(z, g, b):
    mu = jnp.mean(z, axis=-1, keepdims=True)
    zc = z - mu
    var = jnp.mean(zc * zc, axis=-1, keepdims=True)
    return zc * lax.rsqrt(var + LN_EPS) * g + b


def _proj_ln_kernel(a1_ref, a2_ref, w1_ref, w2_ref, r_ref, g_ref, b_ref, of_ref, ob_ref, *, alpha):
    y = jnp.dot(a1_ref[...], w1_ref[...], preferred_element_type=F32)
    y = y + jnp.dot(a2_ref[...], w2_ref[...], preferred_element_type=F32)
    out = _layer_norm_rows(alpha * r_ref[...] + y, g_ref[...], b_ref[...])
    of_ref[...] = out
    ob_ref[...] = out.astype(BF16)


def _proj_ln(a1, a2, w1, w2, res, g, b, alpha):
    m, k1 = a1.shape
    k2 = a2.shape[1]
    d = w1.shape[1]
    tm = _pick_tile(m, (256, 128, 64, 32, 16))
    row = lambda i: (i, 0)
    fix = lambda i: (0, 0)
    return pl.pallas_call(
        functools.partial(_proj_ln_kernel, alpha=alpha),
        grid=(m // tm,),
        in_specs=[pl.BlockSpec((tm, k1), row), pl.BlockSpec((tm, k2), row),
                  pl.BlockSpec((k1, d), fix), pl.BlockSpec((k2, d), fix),
                  pl.BlockSpec((tm, d), row), pl.BlockSpec((1, d), fix), pl.BlockSpec((1, d), fix)],
        out_specs=[pl.BlockSpec((tm, d), row), pl.BlockSpec((tm, d), row)],
        out_shape=[jax.ShapeDtypeStruct((m, d), F32), jax.ShapeDtypeStruct((m, d), BF16)],
        compiler_params=_cparams(1),
        name="proj_ln",
    )(a1, a2, w1, w2, res, g, b)


def _combine_ln_kernel(x_ref, y_ref, g_ref, b_ref, of_ref, ob_ref, *, alpha):
    d = x_ref.shape[1]
    ffn = y_ref[:, :d]
    for k in range(1, TOP_K):
        ffn = ffn + y_ref[:, k * d:(k + 1) * d]
    out = _layer_norm_rows(alpha * x_ref[...] + ffn, g_ref[...], b_ref[...])
    of_ref[...] = out
    ob_ref[...] = out.astype(BF16)


def _combine_ln(x, y4, g, b, alpha):
    m, d = x.shape
    tm = _pick_tile(m, (128, 64, 32, 16))
    row = lambda i: (i, 0)
    fix = lambda i: (0, 0)
    return pl.pallas_call(
        functools.partial(_combine_ln_kernel, alpha=alpha),
        grid=(m // tm,),
        in_specs=[pl.BlockSpec((tm, d), row), pl.BlockSpec((tm, TOP_K * d), row),
                  pl.BlockSpec((1, d), fix), pl.BlockSpec((1, d), fix)],
        out_specs=[pl.BlockSpec((tm, d), row), pl.BlockSpec((tm, d), row)],
        out_shape=[jax.ShapeDtypeStruct((m, d), F32), jax.ShapeDtypeStruct((m, d), BF16)],
        compiler_params=_cparams(1),
        name="combine_ln",
    )(x, y4, g, b)


def _mem_attn_kernel(q_ref, k_ref, v_ref, o_ref, *, n_heads, hd):
    scale = hd ** -0.5
    for h in range(n_heads):
        cols = slice(h * hd, (h + 1) * hd)
        s = _dot_nt(q_ref[:, cols], k_ref[:, cols]) * scale
        p = jnp.exp(s - jnp.max(s, axis=-1, keepdims=True))
        p = p / jnp.sum(p, axis=-1, keepdims=True)
        o_ref[:, cols] = jnp.dot(p.astype(BF16), v_ref[:, cols],
                                 preferred_element_type=F32).astype(BF16)


def _mem_attn(qarr, q_rb0, q_cb, mk, mv, n_batch, t, tq, n_heads, hd):
    w = n_heads * hd
    n_mem = mk.shape[0] // n_batch
    nqb = t // tq
    return pl.pallas_call(
        functools.partial(_mem_attn_kernel, n_heads=n_heads, hd=hd),
        grid=(n_batch, nqb),
        in_specs=[pl.BlockSpec((tq, w), lambda b, i: (q_rb0 + b * nqb + i, q_cb)),
                  pl.BlockSpec((n_mem, w), lambda b, i: (b, 0)),
                  pl.BlockSpec((n_mem, w), lambda b, i: (b, 0))],
        out_specs=pl.BlockSpec((tq, w), lambda b, i: (b * nqb + i, 0)),
        out_shape=jax.ShapeDtypeStruct((n_batch * t, w), BF16),
        compiler_params=_cparams(2),
        name="mem_attn",
    )(qarr, mk, mv)


def _t5_bucket(rel, num_buckets):
    half = num_buckets // 2
    exact = half // 2
    n = jnp.abs(rel)
    far = exact + (jnp.log(jnp.maximum(n, 1).astype(F32) / exact)
                   / math.log(MAX_DISTANCE / exact) * (half - exact)).astype(I32)
    return jnp.where(rel > 0, half, 0) + jnp.where(n < exact, n, jnp.minimum(far, half - 1))


def _near_phase(q_off, tq, tk, nqb, i):
    full = tk // tq
    return (q_off // tq + i) % full if nqb >= full else i


def _near_bias(rel_bias, tq, tk, q_off, nqb):
    full = tk // tq
    i = jnp.arange(tq, dtype=I32)[:, None]
    j = jnp.arange(tk, dtype=I32)[None, :]
    tiles = []
    for p in range(min(full, nqb)):
        q_in_block = (p if nqb >= full else (q_off // tq + p) % full) * tq
        per_near = []
        for near in range(2):
            rel = j - near * tk - (q_in_block + i)
            per_near.append(jnp.moveaxis(rel_bias[_t5_bucket(rel, rel_bias.shape[0])], -1, 0))
        tiles.append(jnp.stack(per_near, axis=1))
    return jnp.stack(tiles).astype(F32)


def _far_bias(rel_bias):
    return rel_bias[rel_bias.shape[0] // 2 - 1].astype(F32)


def _softmax_step(s, v, m_ref, l_ref, acc_ref, idx):
    m_old = m_ref[idx]
    m_new = jnp.maximum(m_old, jnp.max(s, axis=-1, keepdims=True))
    alpha = jnp.exp(m_old - m_new)
    p = jnp.exp(s - m_new)
    l_ref[idx] = alpha * l_ref[idx] + jnp.sum(p, axis=-1, keepdims=True)
    acc_ref[idx] = alpha * acc_ref[idx] + jnp.dot(p.astype(BF16), v, preferred_element_type=F32)
    m_ref[idx] = m_new


def _dsa_kernel(cfar_ref, q_ref, qi_ref, qw_ref, k_ref, v_ref, ki_ref, bd_ref, o_ref,
                key_scr, m_scr, l_scr, acc_scr, *,
                tq, tk, q_off, s_valid, k_sel, n_heads, n_kv, hd, n_idx, d_idx, idx_bits):
    gqa = n_heads // n_kv
    i = pl.program_id(1)
    q0 = q_off + i * tq
    kb_diag = q0 // tk
    nkb = kb_diag + 1
    q_pos = q0 + lax.broadcasted_iota(I32, (tq, 1), 0)
    lane = lax.broadcasted_iota(I32, (1, tk), 1)

    qi = qi_ref[...]
    qw = qw_ref[...].astype(F32)
    qi_heads = [qi[:, h * d_idx:(h + 1) * d_idx] for h in range(n_idx)]
    w_cols = [qw[:, d_idx + h:d_idx + h + 1] * (d_idx ** -0.5) for h in range(n_idx)]

    def score_block(kb, carry):
        ks = pl.multiple_of(kb * tk, tk)
        ki = ki_ref[pl.ds(ks, tk), :][:, :d_idx]
        sc = jnp.zeros((tq, tk), F32)
        for h in range(n_idx):
            sc = sc + jnp.maximum(_dot_nt(qi_heads[h], ki), 0.0) * w_cols[h]
        sc = sc * (n_idx ** -0.5)
        sc = jnp.where(sc == 0.0, 0.0, sc)
        k_pos = ks + lane
        allowed = ((k_pos >> 6) <= (q_pos >> 6)) & (k_pos < s_valid)
        sc = jnp.where(allowed, sc, -jnp.inf)
        bits = pltpu.bitcast(sc, I32)
        key_scr[kb] = bits ^ ((bits >> 31) & 0x7FFFFFFF)
        return carry

    lax.fori_loop(0, nkb, score_block, 0)

    def count(pred):
        def body(kb, acc):
            return acc + jnp.where(pred(kb, key_scr[kb]), 1.0, 0.0)
        acc = lax.fori_loop(0, nkb, body, jnp.zeros((tq, tk), F32))
        return jnp.sum(acc, axis=1, keepdims=True)

    def value_bit(it, t_u):
        cand_u = t_u | (jnp.int32(1) << (31 - it))
        cand_s = cand_u ^ INT_MIN
        cnt = count(lambda kb, key: key >= cand_s)
        return jnp.where(cnt >= k_sel, cand_u, t_u)

    thr = lax.fori_loop(0, 32, value_bit, jnp.zeros((tq, 1), I32)) ^ INT_MIN
    need = k_sel - count(lambda kb, key: key > thr)

    def index_bit(it, j_hi):
        cand = j_hi | (jnp.int32(1) << (idx_bits - 1 - it))
        cnt = count(lambda kb, key: (key == thr) & (kb * tk + lane < cand))
        return jnp.where(cnt < need, cand, j_hi)

    tie_hi = lax.fori_loop(0, idx_bits, index_bit, jnp.zeros((tq, 1), I32))

    m_scr[...] = jnp.full(m_scr.shape, MASKED, F32)
    l_scr[...] = jnp.zeros(l_scr.shape, F32)
    acc_scr[...] = jnp.zeros(acc_scr.shape, F32)
    scale = hd ** -0.5
    q_heads = [q_ref[:, h * hd:(h + 1) * hd] for h in range(n_heads)]

    def attend_block(kb, near):
        ks = pl.multiple_of(kb * tk, tk)
        key = key_scr[kb]
        sel = (key > thr) | ((key == thr) & (ks + lane <= tie_hi))
        sel = sel & (key > KEY_NEG_INF)
        for g in range(n_kv):
            kg = k_ref[pl.ds(ks, tk), g * hd:(g + 1) * hd]
            vg = v_ref[pl.ds(ks, tk), g * hd:(g + 1) * hd]
            for r in range(gqa):
                h = g * gqa + r
                s = _dot_nt(q_heads[h], kg) * scale
                s = s + (cfar_ref[h] if near is None else bd_ref[h, near])
                _softmax_step(jnp.where(sel, s, MASKED), vg, m_scr, l_scr, acc_scr, h)

    def far_block(kb, carry):
        attend_block(kb, None)
        return carry

    lax.fori_loop(0, jnp.maximum(kb_diag - 1, 0), far_block, 0)

    @pl.when(kb_diag >= 1)
    def _():
        attend_block(kb_diag - 1, 1)

    attend_block(kb_diag, 0)

    for h in range(n_heads):
        o_ref[:, h * hd:(h + 1) * hd] = (acc_scr[h] / l_scr[h]).astype(BF16)


def _dsa_attn(q_src, k_src, bias_near, bias_far, *, n_batch, t, s_pad, s_valid, q_off, tq, tk,
              k_sel, n_heads, n_kv, hd, n_idx, d_idx):
    qarr, q_rb0, q_cb, qi_cb, qw_cb = q_src
    karr, k_cb, varr, v_cb, kiarr, ki_cb = k_src
    assert tk >= MAX_DISTANCE and tk % tq == 0 and q_off % tq == 0 and s_pad % tk == 0
    nqb = t // tq
    nkb_max = s_pad // tk
    wq, wkv, wqi = n_heads * hd, n_kv * hd, n_idx * d_idx
    qrow = lambda b, i: q_rb0 + b * nqb + i
    kernel = functools.partial(
        _dsa_kernel, tq=tq, tk=tk, q_off=q_off, s_valid=s_valid, k_sel=k_sel, n_heads=n_heads,
        n_kv=n_kv, hd=hd, n_idx=n_idx, d_idx=d_idx, idx_bits=max((s_pad - 1).bit_length(), 1))
    return pl.pallas_call(
        kernel,
        grid=(n_batch, nqb),
        in_specs=[pl.BlockSpec(memory_space=pltpu.SMEM),
                  pl.BlockSpec((tq, wq), lambda b, i: (qrow(b, i), q_cb)),
                  pl.BlockSpec((tq, wqi), lambda b, i: (qrow(b, i), qi_cb)),
                  pl.BlockSpec((tq, LANE), lambda b, i: (qrow(b, i), qw_cb)),
                  pl.BlockSpec((s_pad, wkv), lambda b, i: (b, k_cb)),
                  pl.BlockSpec((s_pad, wkv), lambda b, i: (b, v_cb)),
                  pl.BlockSpec((s_pad, LANE), lambda b, i: (b, ki_cb)),
                  pl.BlockSpec((None, n_heads, 2, tq, tk),
                               lambda b, i: (_near_phase(q_off, tq, tk, nqb, i), 0, 0, 0, 0))],
        out_specs=pl.BlockSpec((tq, wq), lambda b, i: (b * nqb + i, 0)),
        out_shape=jax.ShapeDtypeStruct((n_batch * t, wq), BF16),
        scratch_shapes=[pltpu.VMEM((nkb_max, tq, tk), I32),
                        pltpu.VMEM((n_heads, tq, 1), F32),
                        pltpu.VMEM((n_heads, tq, 1), F32),
                        pltpu.VMEM((n_heads, tq, hd), F32)],
        compiler_params=_cparams(2),
        name="dsa_attn",
    )(bias_far, qarr, qarr, qarr, karr, varr, kiarr, bias_near)


def _stick_kernel(q_ref, k_ref, v_ref, o_ref, carry_scr, acc_scr, *, tq, tk, q_off, hd):
    i = pl.program_id(2)
    q0 = q_off + i * tq
    q_pos = q0 + lax.broadcasted_iota(I32, (tq, 1), 0)
    lane = lax.broadcasted_iota(I32, (1, tk), 1)
    tri = (lax.broadcasted_iota(I32, (tk, tk), 0) > lax.broadcasted_iota(I32, (tk, tk), 1)).astype(BF16)
    q = q_ref[...]
    scale = hd ** -0.5
    carry_scr[...] = jnp.zeros(carry_scr.shape, F32)
    acc_scr[...] = jnp.zeros(acc_scr.shape, F32)

    def cond(state):
        kb, carry_max = state
        return (kb >= 0) & (carry_max > EXP_UNDERFLOW)

    def body(state):
        kb, _ = state
        ks = pl.multiple_of(kb * tk, tk)
        z = _dot_nt(q, k_ref[pl.ds(ks, tk), :]) * scale
        before = (ks + lane) < q_pos
        log_beta = jnp.minimum(z, 0.0) - jnp.log(1.0 + jnp.exp(-jnp.abs(z)))
        log_keep = jnp.where(before, log_beta - z, 0.0)
        hi = log_keep.astype(BF16)
        lo = (log_keep - hi.astype(F32)).astype(BF16)
        tail = (jnp.dot(hi, tri, preferred_element_type=F32)
                + jnp.dot(lo, tri, preferred_element_type=F32))
        carry = carry_scr[...]
        w = jnp.where(before, jnp.exp(log_beta + tail + carry), 0.0)
        acc_scr[...] += jnp.dot(w.astype(BF16), v_ref[pl.ds(ks, tk), :], preferred_element_type=F32)
        carry = carry + jnp.sum(log_keep, axis=-1, keepdims=True)
        carry_scr[...] = carry
        return kb - 1, jnp.max(carry)

    lax.while_loop(cond, body, ((q0 + tq - 1) // tk, jnp.float32(0.0)))
    o_ref[...] = acc_scr[...].astype(BF16)


def _stick_attn(q_src, k_src, *, n_batch, t, s_pad, q_off, tq, tk, n_heads, hd):
    qarr, q_rb0, q_cb0 = q_src
    karr, k_cb0, varr, v_cb0 = k_src
    nqb = t // tq
    return pl.pallas_call(
        functools.partial(_stick_kernel, tq=tq, tk=tk, q_off=q_off, hd=hd),
        grid=(n_batch, n_heads, nqb),
        in_specs=[pl.BlockSpec((tq, hd), lambda b, h, i: (q_rb0 + b * nqb + i, q_cb0 + h)),
                  pl.BlockSpec((s_pad, hd), lambda b, h, i: (b, k_cb0 + h)),
                  pl.BlockSpec((s_pad, hd), lambda b, h, i: (b, v_cb0 + h))],
        out_specs=pl.BlockSpec((tq, hd), lambda b, h, i: (b * nqb + i, h)),
        out_shape=jax.ShapeDtypeStruct((n_batch * t, n_heads * hd), BF16),
        scratch_shapes=[pltpu.VMEM((tq, 1), F32), pltpu.VMEM((tq, hd), F32)],
        compiler_params=_cparams(3),
        name="stick_attn",
    )(qarr, karr, varr)


def _diff_kernel(scal_ref, q_ref, k_ref, v_ref, bd_ref, g_ref, o_ref, m_scr, l_scr, acc_scr, *,
                 tq, tk, q_off, s_valid, hd, out_scale):
    h = pl.program_id(1)
    i = pl.program_id(2)
    q0 = q_off + i * tq
    kb_diag = q0 // tk
    q_pos = q0 + lax.broadcasted_iota(I32, (tq, 1), 0)
    lane = lax.broadcasted_iota(I32, (1, tk), 1)
    scale = hd ** -0.5
    m_scr[...] = jnp.full(m_scr.shape, MASKED, F32)
    l_scr[...] = jnp.zeros(l_scr.shape, F32)
    acc_scr[...] = jnp.zeros(acc_scr.shape, F32)
    q_maps = [q_ref[:, m * hd:(m + 1) * hd] for m in range(2)]

    def attend_block(kb, near):
        ks = pl.multiple_of(kb * tk, tk)
        v = v_ref[pl.ds(ks, tk), :]
        for m in range(2):
            s = _dot_nt(q_maps[m], k_ref[pl.ds(ks, tk), m * hd:(m + 1) * hd]) * scale
            if near is None:
                s = s + scal_ref[1 + 2 * h + m]
            else:
                k_pos = ks + lane
                allowed = ((k_pos >> 6) <= (q_pos >> 6)) & (k_pos < s_valid)
                s = jnp.where(allowed, s + bd_ref[m, near], MASKED)
            _softmax_step(s, v, m_scr, l_scr, acc_scr, m)

    def far_block(kb, carry):
        attend_block(kb, None)
        return carry

    lax.fori_loop(0, jnp.maximum(kb_diag - 1, 0), far_block, 0)

    @pl.when(kb_diag >= 1)
    def _():
        attend_block(kb_diag - 1, 1)

    attend_block(kb_diag, 0)

    o = acc_scr[0] / l_scr[0] - scal_ref[0] * (acc_scr[1] / l_scr[1])
    o = o * lax.rsqrt(jnp.mean(o * o, axis=-1, keepdims=True) + LN_EPS) * g_ref[...]
    o_ref[...] = (o * out_scale).astype(BF16)


def _diff_attn(q_src, k_src, bias_near, scalars, subln_g, *, n_batch, t, s_pad, s_valid, q_off,
               tq, tk, n_heads, hd, out_scale):
    qarr, q_rb0, q_cb0 = q_src
    karr, k_cb0, varr, v_cb0 = k_src
    assert tk >= MAX_DISTANCE and tk % tq == 0 and q_off % tq == 0 and s_pad % tk == 0
    nqb = t // tq
    w = 2 * hd
    return pl.pallas_call(
        functools.partial(_diff_kernel, tq=tq, tk=tk, q_off=q_off, s_valid=s_valid, hd=hd,
                          out_scale=out_scale),
        grid=(n_batch, n_heads, nqb),
        in_specs=[pl.BlockSpec(memory_space=pltpu.SMEM),
                  pl.BlockSpec((tq, w), lambda b, h, i: (q_rb0 + b * nqb + i, q_cb0 + h)),
                  pl.BlockSpec((s_pad, w), lambda b, h, i: (b, k_cb0 + h)),
                  pl.BlockSpec((s_pad, w), lambda b, h, i: (b, v_cb0 + h)),
                  pl.BlockSpec((None, None, 2, 2, tq, tk),
                               lambda b, h, i: (_near_phase(q_off, tq, tk, nqb, i), h, 0, 0, 0, 0)),
                  pl.BlockSpec((1, w), lambda b, h, i: (0, 0))],
        out_specs=pl.BlockSpec((tq, w), lambda b, h, i: (b * nqb + i, h)),
        out_shape=jax.ShapeDtypeStruct((n_batch * t, n_heads * w), BF16),
        scratch_shapes=[pltpu.VMEM((2, tq, 1), F32), pltpu.VMEM((2, tq, 1), F32),
                        pltpu.VMEM((2, tq, w), F32)],
        compiler_params=_cparams(3),
        name="diff_attn",
    )(scalars, qarr, karr, varr, bias_near, subln_g)


def _router_kernel(x_ref, wh_ref, wl_ref, b_ref, e_ref, g_ref):
    x = x_ref[...]
    xh = x.astype(BF16)
    xl = (x - xh.astype(F32)).astype(BF16)
    wh = wh_ref[...]
    logits = (jnp.dot(xh, wh, preferred_element_type=F32)
              + jnp.dot(xl, wh, preferred_element_type=F32)
              + jnp.dot(xh, wl_ref[...], preferred_element_type=F32)) + b_ref[...]
    lane = lax.broadcasted_iota(I32, logits.shape, 1)
    lane_f = lane.astype(F32)
    experts = jnp.zeros(logits.shape, I32)
    gates = jnp.zeros(logits.shape, F32)
    top = None
    for k in range(TOP_K):
        m = jnp.max(logits, axis=-1, keepdims=True)
        ix = jnp.min(jnp.where(logits == m, lane_f, float(LANE)), axis=-1, keepdims=True).astype(I32)
        top = m if top is None else top
        experts = jnp.where(lane == k, ix, experts)
        gates = jnp.where(lane == k, jnp.exp(m - top), gates)
        logits = jnp.where(lane == ix, -jnp.inf, logits)
    e_ref[...] = experts
    g_ref[...] = gates / jnp.sum(gates, axis=-1, keepdims=True)


def _router(x, w_hi, w_lo, bias):
    m, d = x.shape
    tm = _pick_tile(m, (256, 128, 64, 32, 16))
    row = lambda i: (i, 0)
    fix = lambda i: (0, 0)
    return pl.pallas_call(
        _router_kernel,
        grid=(m // tm,),
        in_specs=[pl.BlockSpec((tm, d), row), pl.BlockSpec((d, LANE), fix),
                  pl.BlockSpec((d, LANE), fix), pl.BlockSpec((1, LANE), fix)],
        out_specs=[pl.BlockSpec((tm, LANE), row), pl.BlockSpec((tm, LANE), row)],
        out_shape=[jax.ShapeDtypeStruct((m, LANE), I32), jax.ShapeDtypeStruct((m, LANE), F32)],
        compiler_params=_cparams(1),
        name="router",
    )(x, w_hi, w_lo, bias)


def _cast_rows(src_ref, dst_ref, rows_per_step):
    def body(c, carry):
        r = pl.multiple_of(c * rows_per_step, rows_per_step)
        dst_ref[pl.ds(r, rows_per_step), :] = src_ref[pl.ds(r, rows_per_step), :].astype(BF16)
        return carry
    lax.fori_loop(0, src_ref.shape[0] // rows_per_step, body, 0)


def _expert_changed(be_ref, j):
    return (j == 0) | (be_ref[j] != be_ref[jnp.maximum(j - 1, 0)])


def _moe_up_kernel(be_ref, nu_ref, x_ref, wg_ref, wl_ref, bg_ref, bl_ref, h_ref, wg_bf, wl_bf, *,
                   cast_rows):
    j = pl.program_id(1)

    @pl.when(_expert_changed(be_ref, j))
    def _():
        _cast_rows(wg_ref, wg_bf, cast_rows)
        _cast_rows(wl_ref, wl_bf, cast_rows)

    @pl.when(j < nu_ref[0])
    def _():
        x = x_ref[...]
        glu = jnp.dot(x, wg_bf[...], preferred_element_type=F32) + bg_ref[...]
        lin = jnp.dot(x, wl_bf[...], preferred_element_type=F32) + bl_ref[...]
        glu = jnp.minimum(glu, SWIGLU_LIMIT)
        lin = jnp.clip(lin, -SWIGLU_LIMIT, SWIGLU_LIMIT)
        h_ref[...] = (glu * jax.nn.sigmoid(SWIGLU_ALPHA * glu) * (lin + 1.0)).astype(BF16)

    @pl.when(j >= nu_ref[0])
    def _():
        h_ref[...] = jnp.zeros(h_ref.shape, BF16)


def _moe_down_kernel(be_ref, nu_ref, h_ref, wd_ref, bd_ref, gate_ref, y_ref, wd_bf, *, cast_rows):
    j = pl.program_id(1)

    @pl.when(_expert_changed(be_ref, j))
    def _():
        _cast_rows(wd_ref, wd_bf, cast_rows)

    @pl.when(j < nu_ref[0])
    def _():
        y = jnp.dot(h_ref[...], wd_bf[...], preferred_element_type=F32) + bd_ref[...]
        y_ref[...] = y * gate_ref[...]

    @pl.when(j >= nu_ref[0])
    def _():
        y_ref[...] = jnp.zeros(y_ref.shape, F32)


def _moe_experts(xs, block_e, n_used, row_gate, w_gate_up, b_gate_up, w_down, b_down, tm):
    n_rows, d = xs.shape
    n_blocks = n_rows // tm
    d_ff = w_down.shape[1]
    tf = _pick_tile(d_ff, (512, 256, 128))
    nf = d_ff // tf
    tn = _pick_tile(d, (512, 256, 128))
    cast_rows = _pick_tile(d, (256, 128, 64, 32, 16, 8))
    blk = lambda j, nu: jnp.minimum(j, nu[0] - 1)
    hidden = pl.pallas_call(
        functools.partial(_moe_up_kernel, cast_rows=cast_rows),
        grid_spec=pltpu.PrefetchScalarGridSpec(
            num_scalar_prefetch=2,
            grid=(nf, n_blocks),
            in_specs=[pl.BlockSpec((tm, d), lambda f, j, be, nu: (blk(j, nu), 0)),
                      pl.BlockSpec((None, d, tf), lambda f, j, be, nu: (be[j], 0, f)),
                      pl.BlockSpec((None, d, tf), lambda f, j, be, nu: (be[j], 0, nf + f)),
                      pl.BlockSpec((None, 1, tf), lambda f, j, be, nu: (be[j], 0, f)),
                      pl.BlockSpec((None, 1, tf), lambda f, j, be, nu: (be[j], 0, nf + f))],
            out_specs=pl.BlockSpec((tm, tf), lambda f, j, be, nu: (j, f)),
            scratch_shapes=[pltpu.VMEM((d, tf), BF16), pltpu.VMEM((d, tf), BF16)]),
        out_shape=jax.ShapeDtypeStruct((n_rows, d_ff), BF16),
        compiler_params=_cparams(2),
        name="moe_up",
    )(block_e, n_used, xs, w_gate_up, w_gate_up, b_gate_up, b_gate_up)
    cast_rows = _pick_tile(d_ff, (256, 128, 64, 32, 16, 8))
    return pl.pallas_call(
        functools.partial(_moe_down_kernel, cast_rows=cast_rows),
        grid_spec=pltpu.PrefetchScalarGridSpec(
            num_scalar_prefetch=2,
            grid=(d // tn, n_blocks),
            in_specs=[pl.BlockSpec((tm, d_ff), lambda n, j, be, nu: (blk(j, nu), 0)),
                      pl.BlockSpec((None, d_ff, tn), lambda n, j, be, nu: (be[j], 0, n)),
                      pl.BlockSpec((None, 1, tn), lambda n, j, be, nu: (be[j], 0, n)),
                      pl.BlockSpec((tm, 1), lambda n, j, be, nu: (blk(j, nu), 0))],
            out_specs=pl.BlockSpec((tm, tn), lambda n, j, be, nu: (j, n)),
            scratch_shapes=[pltpu.VMEM((d_ff, tn), BF16)]),
        out_shape=jax.ShapeDtypeStruct((n_rows, d), F32),
        compiler_params=_cparams(2),
        name="moe_down",
    )(block_e, n_used, hidden, w_down, b_down, row_gate)


def _moe(x_f32, x_bf16, w_router_hi, w_router_lo, b_router, w_gate_up, b_gate_up, w_down, b_down):
    n, d = x_f32.shape
    n_experts = w_gate_up.shape[0]
    tm = _pick_tile(n * TOP_K, (256, 128, 64, 32, 16))
    experts, gates = _router(x_f32, w_router_hi, w_router_lo, b_router)
    flat_e = experts[:, :TOP_K].reshape(-1)
    flat_g = gates[:, :TOP_K].reshape(-1)
    n_assign = n * TOP_K
    order = jnp.argsort(flat_e)
    sorted_e = flat_e[order]
    counts = jnp.bincount(flat_e, length=n_experts)
    padded = (counts + tm - 1) // tm * tm
    start = jnp.cumsum(counts) - counts
    pend = jnp.cumsum(padded)
    pstart = pend - padded
    dest = (pstart[sorted_e] + jnp.arange(n_assign) - start[sorted_e]).astype(I32)
    n_blocks = -(-(n_assign + n_experts * (tm - 1)) // tm)
    n_rows = n_blocks * tm
    row_token = jnp.zeros((n_rows,), I32).at[dest].set((order // TOP_K).astype(I32))
    row_gate = jnp.zeros((n_rows,), F32).at[dest].set(flat_g[order])
    block_e = jnp.minimum(jnp.searchsorted(pend, jnp.arange(n_blocks) * tm, side='right'),
                          n_experts - 1).astype(I32)
    n_used = (pend[-1] // tm).astype(I32).reshape(1)
    row_of_assign = jnp.zeros((n_assign,), I32).at[order].set(dest)
    xs = jnp.take(x_bf16, row_token, axis=0)
    y = _moe_experts(xs, block_e, n_used, row_gate[:, None],
                     w_gate_up, b_gate_up[:, None, :], w_down, b_down[:, None, :], tm)
    return jnp.take(y, row_of_assign, axis=0).reshape(n, TOP_K * d)


def _pad_cols(w, n):
    return jnp.pad(w, ((0, 0), (0, n - w.shape[1])))


def _keys_with_past(past, new, s_pad):
    b, p, w = past.shape
    pad = jnp.zeros((b, s_pad - p - new.shape[1], w), BF16)
    return jnp.concatenate([past.astype(BF16), new.astype(BF16), pad], axis=1).reshape(b * s_pad, w)


def kernel(x_prompt, x_sample, cache_a_k, cache_a_v, cache_a_kidx, cache_b_k, cache_b_v,
           cache_c_k, cache_c_v, cache_mem_k, cache_mem_v, mem_prompt,
           w_in_a, w_in_b, w_in_c, diff_lambda, diff_subln, rel_bias, w_mem_kv, w_o,
           ln_g, ln_b, w_router, b_router, w_gate_up, b_gate_up, w_down, b_down):
    bp, tp, d = x_prompt.shape
    bs, ts, _ = x_sample.shape
    depth = w_o.shape[0]
    past_len = cache_a_k.shape[2]
    n_kv, hd = cache_a_k.shape[3], cache_a_k.shape[4]
    d_idx = cache_a_kidx.shape[3]
    n_mem, h_mem = cache_mem_k.shape[2], cache_mem_k.shape[3]
    mem_w = h_mem * hd
    tok_w = d - mem_w
    h_a = tok_w // hd
    h_b = cache_b_k.shape[3]
    hc = cache_c_k.shape[3]
    n_idx = (w_in_a.shape[2] - tok_w - 2 * n_kv * hd - d_idx - mem_w) // (d_idx + 1)
    n_experts = w_router.shape[2]
    alpha = (2 * depth) ** 0.25
    mp, ms = bp * tp, bs * ts
    s_new = past_len + ts
    kv_w, qi_w = n_kv * hd, n_idx * d_idx

    o_k, o_v, o_qi = tok_w, tok_w + kv_w, tok_w + 2 * kv_w
    o_ki = o_qi + qi_w
    o_wi = o_ki + d_idx
    o_mq = o_wi + n_idx
    a_mq = o_qi + qi_w
    a_kw = a_mq + mem_w
    cols_a = _round_up(a_kw + LANE, 5 * LANE if d >= 5 * LANE else LANE)
    w_a = [_pad_cols(jnp.concatenate([w[:, :o_ki], w[:, o_mq:], w[:, o_ki:o_mq]], axis=1),
                     cols_a).astype(BF16) for w in w_in_a]
    w_b = [w.astype(BF16) for w in w_in_b]
    w_c = [w.astype(BF16) for w in w_in_c]
    w_o_bf = w_o.astype(BF16)
    w_r_pad = jnp.pad(w_router, ((0, 0), (0, 0), (0, LANE - n_experts)))
    w_r_hi = w_r_pad.astype(BF16)
    w_r_lo = (w_r_pad - w_r_hi.astype(F32)).astype(BF16)
    b_r_pad = jnp.pad(b_router, ((0, 0), (0, LANE - n_experts)), constant_values=MASKED)[:, None, :]

    tq_a = _pick_tile(tp, (128, 64))
    tk_a = max(2 * tq_a, MAX_DISTANCE)
    tq_b = _pick_tile(tp, (256, 128, 64))
    tk_b = min(tq_b, 128)
    tq_c = _pick_tile(tp, (256, 128))
    tk_c = tq_c
    tq_m = _pick_tile(tp, (512, 256, 128))
    sp_a, sp_b, sp_c = _round_up(s_new, tk_a), _round_up(s_new, tk_b), _round_up(s_new, tk_c)

    far = _far_bias(rel_bias)
    near_a_p = _near_bias(rel_bias, tq_a, tk_a, 0, tp // tq_a)
    near_a_s = _near_bias(rel_bias, ts, tk_a, past_len, 1)
    near_c_p = _near_bias(rel_bias, tq_c, tk_c, 0, tp // tq_c)
    near_c_s = _near_bias(rel_bias, ts, tk_c, past_len, 1)
    pair = lambda nb: nb.reshape(nb.shape[0], hc, 2, *nb.shape[2:])

    w_mem = jnp.moveaxis(w_mem_kv, 0, 1).reshape(d, depth * 2 * mem_w).astype(BF16)
    mem_f, mem_b = _matmul(mem_prompt.reshape(bp * n_mem, d).astype(BF16), w_mem)
    mem_f = mem_f.reshape(bp, n_mem, depth, 2, h_mem, hd)
    mem_k_p = jnp.moveaxis(mem_f[:, :, :, 0], 2, 0)
    mem_v_p = jnp.moveaxis(mem_f[:, :, :, 1], 2, 0)
    mem_s_k = cache_mem_k.reshape(depth, bs * n_mem, mem_w).astype(BF16)
    mem_s_v = cache_mem_v.reshape(depth, bs * n_mem, mem_w).astype(BF16)

    x_f = jnp.concatenate([x_prompt.reshape(mp, d), x_sample.reshape(ms, d)], axis=0)
    x_b = x_f.astype(BF16)
    rows_p = [None] * depth
    rows_s = [None] * depth

    for i in range(depth):
        kind, j = i % N_MIXERS, i // N_MIXERS
        w_in = (w_a, w_b, w_c)[kind][j]
        h_f, h_b_ = _matmul(x_b, w_in)
        hs_f = h_f[mp:].reshape(bs, ts, -1)
        hp = lambda lo, hi: h_f[:mp, lo:hi]
        rb_s = mp // ts

        if kind == 0:
            k_sel_p = min(K_SEL_MAX, tp // 4)
            k_sel_s = min(K_SEL_MAX, s_new // 4)
            dims = dict(n_heads=h_a, n_kv=n_kv, hd=hd, n_idx=n_idx, d_idx=d_idx)
            mix_p = _dsa_attn(
                (h_b_, 0, 0, o_qi // qi_w, a_kw // LANE),
                (h_b_, o_k // kv_w, h_b_, o_v // kv_w, h_b_, a_kw // LANE),
                near_a_p, far, n_batch=bp, t=tp, s_pad=tp, s_valid=tp, q_off=0,
                tq=tq_a, tk=tk_a, k_sel=k_sel_p, **dims)
            kiw_new = jnp.pad(hs_f[..., a_kw:a_kw + d_idx], ((0, 0), (0, 0), (0, LANE - d_idx)))
            kiw_past = jnp.pad(cache_a_kidx[j], ((0, 0), (0, 0), (0, LANE - d_idx)))
            k_all = _keys_with_past(cache_a_k[j].reshape(bs, past_len, kv_w), hs_f[..., o_k:o_v], sp_a)
            v_all = _keys_with_past(cache_a_v[j].reshape(bs, past_len, kv_w), hs_f[..., o_v:o_qi], sp_a)
            ki_all = _keys_with_past(kiw_past, kiw_new, sp_a)
            mix_s = _dsa_attn(
                (h_b_, rb_s, 0, o_qi // qi_w, a_kw // LANE),
                (k_all, 0, v_all, 0, ki_all, 0),
                near_a_s, far, n_batch=bs, t=ts, s_pad=sp_a, s_valid=s_new, q_off=past_len,
                tq=ts, tk=tk_a, k_sel=k_sel_s, **dims)
            mq_cb = a_mq // mem_w
            rows_p[i] = (hp(o_k, o_v).reshape(bp, tp, n_kv, hd), hp(o_v, o_qi).reshape(bp, tp, n_kv, hd),
                         hp(a_kw, a_kw + d_idx).reshape(bp, tp, d_idx))
            rows_s[i] = (hs_f[..., o_k:o_v].reshape(bs, ts, n_kv, hd),
                         hs_f[..., o_v:o_qi].reshape(bs, ts, n_kv, hd), hs_f[..., a_kw:a_kw + d_idx])
        elif kind == 1:
            dims = dict(n_heads=h_b, hd=hd)
            mix_p = _stick_attn((h_b_, 0, 0), (h_b_, h_b, h_b_, 2 * h_b),
                                n_batch=bp, t=tp, s_pad=tp, q_off=0, tq=tq_b, tk=tk_b, **dims)
            k_all = _keys_with_past(cache_b_k[j].reshape(bs, past_len, tok_w), hs_f[..., tok_w:2 * tok_w], sp_b)
            v_all = _keys_with_past(cache_b_v[j].reshape(bs, past_len, tok_w), hs_f[..., 2 * tok_w:3 * tok_w], sp_b)
            mix_s = _stick_attn((h_b_, rb_s, 0), (k_all, 0, v_all, 0),
                                n_batch=bs, t=ts, s_pad=sp_b, q_off=past_len, tq=ts, tk=tk_b, **dims)
            mq_cb = 3 * tok_w // mem_w
            rows_p[i] = (hp(tok_w, 2 * tok_w).reshape(bp, tp, h_b, hd),
                         hp(2 * tok_w, 3 * tok_w).reshape(bp, tp, h_b, hd))
            rows_s[i] = (hs_f[..., tok_w:2 * tok_w].reshape(bs, ts, h_b, hd),
                         hs_f[..., 2 * tok_w:3 * tok_w].reshape(bs, ts, h_b, hd))
        else:
            lambda_init = 0.8 - 0.6 * math.exp(-0.3 * i)
            lv = diff_lambda[j].astype(F32)
            lam = jnp.exp(jnp.sum(lv[0] * lv[1])) - jnp.exp(jnp.sum(lv[2] * lv[3])) + lambda_init
            scalars = jnp.concatenate([lam.reshape(1), far]).astype(F32)
            g = diff_subln[j].reshape(1, 2 * hd).astype(F32)
            dims = dict(n_heads=hc, hd=hd, out_scale=1.0 - lambda_init)
            mix_p = _diff_attn((h_b_, 0, 0), (h_b_, hc, h_b_, 2 * hc), pair(near_c_p), scalars, g,
                               n_batch=bp, t=tp, s_pad=tp, s_valid=tp, q_off=0, tq=tq_c, tk=tk_c, **dims)
            k_all = _keys_with_past(cache_c_k[j].reshape(bs, past_len, tok_w), hs_f[..., tok_w:2 * tok_w], sp_c)
            v_all = _keys_with_past(cache_c_v[j].reshape(bs, past_len, tok_w), hs_f[..., 2 * tok_w:3 * tok_w], sp_c)
            mix_s = _diff_attn((h_b_, rb_s, 0), (k_all, 0, v_all, 0), pair(near_c_s), scalars, g,
                               n_batch=bs, t=ts, s_pad=sp_c, s_valid=s_new, q_off=past_len,
                               tq=ts, tk=tk_c, **dims)
            mq_cb = 3 * tok_w // mem_w
            rows_p[i] = (hp(tok_w, 2 * tok_w).reshape(bp, tp, hc, 2, hd),
                         hp(2 * tok_w, 3 * tok_w).reshape(bp, tp, hc, 2 * hd))
            rows_s[i] = (hs_f[..., tok_w:2 * tok_w].reshape(bs, ts, hc, 2, hd),
                         hs_f[..., 2 * tok_w:3 * tok_w].reshape(bs, ts, hc, 2 * hd))

        mem_p = _mem_attn(h_b_, 0, mq_cb, mem_b[:, (2 * i) * mem_w:(2 * i + 1) * mem_w],
                          mem_b[:, (2 * i + 1) * mem_w:(2 * i + 2) * mem_w],
                          bp, tp, tq_m, h_mem, hd)
        mem_s = _mem_attn(h_b_, rb_s, mq_cb, mem_s_k[i], mem_s_v[i], bs, ts, ts, h_mem, hd)
        mix = jnp.concatenate([mix_p, mix_s], axis=0)
        mem_o = jnp.concatenate([mem_p, mem_s], axis=0)
        x_f, x_b = _proj_ln(mix, mem_o, w_o_bf[i, :tok_w], w_o_bf[i, tok_w:], x_f,
                            ln_g[i, 0][None], ln_b[i, 0][None], alpha)
        y4 = _moe(x_f, x_b, w_r_hi[i], w_r_lo[i], b_r_pad[i], w_gate_up[i], b_gate_up[i],
                  w_down[i], b_down[i])
        x_f, x_b = _combine_ln(x_f, y4, ln_g[i, 1][None], ln_b[i, 1][None], alpha)

    def stack_kind(rows, kind):
        sel = [r for i, r in enumerate(rows) if i % N_MIXERS == kind]
        return [jnp.stack(parts) for parts in zip(*sel)]

    y_prompt = x_f[:mp].reshape(bp, tp, d)
    y_sample = x_f[mp:].reshape(bs, ts, d)
    a_k_p, a_v_p, a_kidx_p = stack_kind(rows_p, 0)
    b_k_p, b_v_p = stack_kind(rows_p, 1)
    c_k_p, c_v_p = stack_kind(rows_p, 2)
    a_k_s, a_v_s, a_kidx_s = stack_kind(rows_s, 0)
    b_k_s, b_v_s = stack_kind(rows_s, 1)
    c_k_s, c_v_s = stack_kind(rows_s, 2)
    return (y_prompt, y_sample, a_k_p, a_v_p, a_kidx_p, b_k_p, b_v_p, c_k_p, c_v_p, mem_k_p, mem_v_p,
            a_k_s, a_v_s, a_kidx_s, b_k_s, b_v_s, c_k_s, c_v_s)
```

```python
import functools
import math

import jax
import jax.numpy as jnp
from jax import lax
from jax.experimental import pallas as pl
from jax.experimental.pallas import tpu as pltpu

F32 = jnp.float32
BF16 = jnp.bfloat16
I32 = jnp.int32

CHUNK = 64
CHUNK_SHIFT = CHUNK.bit_length() - 1
K_SEL_MAX = 256
MAX_DISTANCE = 128
TOP_K = 4
SWIGLU_LIMIT = 7.0
SWIGLU_ALPHA = 1.702
LN_EPS = 1e-5
N_MIXERS = 3

LANE = 128
VMEM_LIMIT = 56 * 1024 * 1024
MASKED = -1e30
EXP_UNDERFLOW = -104.0
LOG2E = 1.4426950408889634
INT_MIN = -2 ** 31
KEY_NEG_INF = -2139095041
HI16 = -65536


def _cparams(n_axes):
    return pltpu.CompilerParams(dimension_semantics=("arbitrary",) * n_axes,
                                vmem_limit_bytes=VMEM_LIMIT)


def _round_up(x, m):
    return -(-x // m) * m


def _pick_tile(n, candidates):
    for c in candidates:
        if n % c == 0:
            return c
    return n


def _dot_nt(a, b):
    return lax.dot_general(a, b, (((1,), (1,)), ((), ())), preferred_element_type=F32)


def _resident(block_shape, index_map):
    return pl.BlockSpec(block_shape, index_map, pipeline_mode=pl.Buffered(1))


def _mm_kernel(x_ref, w_ref, of_ref, ob_ref):
    acc = jnp.dot(x_ref[...], w_ref[...], preferred_element_type=F32)
    of_ref[...] = acc
    ob_ref[...] = acc.astype(BF16)


def _matmul(x, w):
    m, k = x.shape
    n = w.shape[1]
    tm = _pick_tile(m, (640, 512, 256, 128, 64, 32, 16))
    tn = _pick_tile(n, (1280, 1024, 768, 512, 384, 256, 128))
    return pl.pallas_call(
        _mm_kernel,
        grid=(n // tn, m // tm),
        in_specs=[pl.BlockSpec((tm, k), lambda j, i: (i, 0)),
                  pl.BlockSpec((k, tn), lambda j, i: (0, j))],
        out_specs=[pl.BlockSpec((tm, tn), lambda j, i: (i, j)),
                   pl.BlockSpec((tm, tn), lambda j, i: (i, j))],
        out_shape=[jax.ShapeDtypeStruct((m, n), F32), jax.ShapeDtypeStruct((m, n), BF16)],
        compiler_params=_cparams(2),
        name="matmul",
    )(x, w)


def _layer_norm_rows(z, g, b):
    mu = jnp.mean(z, axis=-1, keepdims=True)
    zc = z - mu
    var = jnp.mean(zc * zc, axis=-1, keepdims=True)
    return zc * lax.rsqrt(var + LN_EPS) * g + b


def _pack_halves(x):
    half = x.shape[1] // 2
    xb = x.astype(BF16).astype(F32)
    lo = lax.shift_right_logical(pltpu.bitcast(xb[:, :half], I32), 16)
    return lo | (pltpu.bitcast(xb[:, half:], I32) & HI16)


def _unpack_halves(p):
    lo = pltpu.bitcast(p << 16, F32).astype(BF16)
    hi = pltpu.bitcast(p & HI16, F32).astype(BF16)
    return jnp.concatenate([lo, hi], axis=1)


def _proj_ln_kernel(a1_ref, a2_ref, w1_ref, w2_ref, r_ref, g_ref, b_ref, of_ref, op_ref, *, alpha):
    y = jnp.dot(a1_ref[...], w1_ref[...], preferred_element_type=F32)
    y = y + jnp.dot(a2_ref[...], w2_ref[...], preferred_element_type=F32)
    out = _layer_norm_rows(alpha * r_ref[...] + y, g_ref[...], b_ref[...])
    of_ref[...] = out
    op_ref[...] = _pack_halves(out)


def _proj_ln(a1, a2, w1, w2, res, g, b, alpha):
    m, k1 = a1.shape
    k2 = a2.shape[1]
    d = w1.shape[1]
    tm = _pick_tile(m, (256, 128, 64, 32, 16))
    row = lambda i: (i, 0)
    fix = lambda i: (0, 0)
    return pl.pallas_call(
        functools.partial(_proj_ln_kernel, alpha=alpha),
        grid=(m // tm,),
        in_specs=[pl.BlockSpec((tm, k1), row), pl.BlockSpec((tm, k2), row),
                  pl.BlockSpec((k1, d), fix), pl.BlockSpec((k2, d), fix),
                  pl.BlockSpec((tm, d), row), pl.BlockSpec((1, d), fix), pl.BlockSpec((1, d), fix)],
        out_specs=[pl.BlockSpec((tm, d), row), pl.BlockSpec((tm, d // 2), row)],
        out_shape=[jax.ShapeDtypeStruct((m, d), F32), jax.ShapeDtypeStruct((m, d // 2), I32)],
        compiler_params=_cparams(1),
        name="proj_ln",
    )(a1, a2, w1, w2, res, g, b)


def _row_copy(src_ref, src_row, dst_ref, dst_row, sem):
    return pltpu.make_async_copy(src_ref.at[pl.ds(src_row, 1)], dst_ref.at[pl.ds(dst_row, 1)], sem)


def _combine_ln_kernel(dest_ref, next_ref, x_ref, gate_ref, g_ref, b_ref, y_hbm, of_ref, ob_ref,
                       ybuf, sems, *, alpha, tm, n_steps):
    i = pl.program_id(0)
    slot = i % 2

    def gather(idx_ref, to_slot):
        def body(r, carry):
            for k in range(TOP_K):
                _row_copy(y_hbm, idx_ref[0, r * TOP_K + k], ybuf.at[to_slot, k], r,
                          sems.at[to_slot]).start()
            return carry
        lax.fori_loop(0, tm, body, 0)

    @pl.when(i == 0)
    def _():
        gather(dest_ref, 0)

    @pl.when(i + 1 < n_steps)
    def _():
        gather(next_ref, 1 - slot)

    def drain(r, carry):
        for k in range(TOP_K):
            _row_copy(y_hbm, 0, ybuf.at[slot, k], 0, sems.at[slot]).wait()
        return carry
    lax.fori_loop(0, tm, drain, 0)

    gates = gate_ref[...]
    ffn = gates[:, 0:1] * ybuf[slot, 0]
    for k in range(1, TOP_K):
        ffn = ffn + gates[:, k:k + 1] * ybuf[slot, k]
    out = _layer_norm_rows(alpha * x_ref[...] + ffn, g_ref[...], b_ref[...])
    of_ref[...] = out
    ob_ref[...] = out.astype(BF16)


def _combine_ln(x, y, dest, gates, g, b, alpha):
    m, d = x.shape
    tm = _pick_tile(m, (64, 32, 16))
    n_steps = m // tm
    dest2 = dest.reshape(n_steps, 1, tm * TOP_K)
    row = lambda i: (i, 0)
    fix = lambda i: (0, 0)
    return pl.pallas_call(
        functools.partial(_combine_ln_kernel, alpha=alpha, tm=tm, n_steps=n_steps),
        grid=(n_steps,),
        in_specs=[pl.BlockSpec((None, 1, tm * TOP_K), lambda i: (i, 0, 0), memory_space=pltpu.SMEM),
                  pl.BlockSpec((None, 1, tm * TOP_K), lambda i: (jnp.minimum(i + 1, n_steps - 1), 0, 0),
                               memory_space=pltpu.SMEM),
                  pl.BlockSpec((tm, d), row), pl.BlockSpec((tm, LANE), row),
                  pl.BlockSpec((1, d), fix), pl.BlockSpec((1, d), fix),
                  pl.BlockSpec(memory_space=pl.ANY)],
        out_specs=[pl.BlockSpec((tm, d), row), pl.BlockSpec((tm, d), row)],
        out_shape=[jax.ShapeDtypeStruct((m, d), F32), jax.ShapeDtypeStruct((m, d), BF16)],
        scratch_shapes=[pltpu.VMEM((2, TOP_K, tm, d), F32), pltpu.SemaphoreType.DMA((2,))],
        compiler_params=_cparams(1),
        name="combine_ln",
    )(dest2, dest2, x, gates, g, b, y)


def _mem_attn_kernel(q_ref, k_ref, v_ref, o_ref, *, n_heads, hd):
    scale = hd ** -0.5
    for h in range(n_heads):
        cols = slice(h * hd, (h + 1) * hd)
        s = _dot_nt(q_ref[:, cols], k_ref[:, cols]) * scale
        p = jnp.exp(s - jnp.max(s, axis=-1, keepdims=True))
        p = p / jnp.sum(p, axis=-1, keepdims=True)
        o_ref[:, cols] = jnp.dot(p.astype(BF16), v_ref[:, cols],
                                 preferred_element_type=F32).astype(BF16)


def _mem_attn(qarr, q_rb0, q_cb, mk, mv, n_batch, t, tq, n_heads, hd):
    w = n_heads * hd
    n_mem = mk.shape[0] // n_batch
    nqb = t // tq
    return pl.pallas_call(
        functools.partial(_mem_attn_kernel, n_heads=n_heads, hd=hd),
        grid=(n_batch, nqb),
        in_specs=[pl.BlockSpec((tq, w), lambda b, i: (q_rb0 + b * nqb + i, q_cb)),
                  pl.BlockSpec((n_mem, w), lambda b, i: (b, 0)),
                  pl.BlockSpec((n_mem, w), lambda b, i: (b, 0))],
        out_specs=pl.BlockSpec((tq, w), lambda b, i: (b * nqb + i, 0)),
        out_shape=jax.ShapeDtypeStruct((n_batch * t, w), BF16),
        compiler_params=_cparams(2),
        name="mem_attn",
    )(qarr, mk, mv)


def _t5_bucket(rel, num_buckets):
    half = num_buckets // 2
    exact = half // 2
    n = jnp.abs(rel)
    far = exact + (jnp.log(jnp.maximum(n, 1).astype(F32) / exact)
                   / math.log(MAX_DISTANCE / exact) * (half - exact)).astype(I32)
    return jnp.where(rel > 0, half, 0) + jnp.where(n < exact, n, jnp.minimum(far, half - 1))


def _near_phase(q_off, tq, tk, nqb, i):
    full = tk // tq
    return (q_off // tq + i) % full if nqb >= full else i


def _near_bias(rel_bias, tq, tk, q_off, nqb):
    full = tk // tq
    i = jnp.arange(tq, dtype=I32)[:, None]
    j = jnp.arange(tk, dtype=I32)[None, :]
    tiles = []
    for p in range(min(full, nqb)):
        q_in_block = (p if nqb >= full else (q_off // tq + p) % full) * tq
        per_near = []
        for near in range(2):
            rel = j - near * tk - (q_in_block + i)
            per_near.append(jnp.moveaxis(rel_bias[_t5_bucket(rel, rel_bias.shape[0])], -1, 0))
        tiles.append(jnp.stack(per_near, axis=1))
    return jnp.stack(tiles).astype(F32)


def _far_bias(rel_bias):
    return rel_bias[rel_bias.shape[0] // 2 - 1].astype(F32)


def _dsa_kernel(far_ref, q_ref, qi_ref, qw_ref, k_ref, v_ref, ki_ref, bd_ref, o_ref,
                key_scr, wb_scr, m_scr, acc_scr, *,
                tq, tk, q_off, s_valid, k_sel, n_heads, n_kv, hd, n_idx, d_idx, idx_bits):
    gqa = n_heads // n_kv
    reps = tk // LANE
    i = pl.program_id(1)
    q0 = q_off + i * tq
    kb_diag = q0 // tk
    nkb = kb_diag + 1
    q_pos = q0 + lax.broadcasted_iota(I32, (tq, 1), 0)
    lane = lax.broadcasted_iota(I32, (1, tk), 1)
    lane1 = lax.broadcasted_iota(I32, (1, LANE), 1)
    tile = lambda x: jnp.concatenate([x] * reps, axis=1)

    qi = qi_ref[...]
    qw = qw_ref[...].astype(F32)
    qi_heads = [qi[:, h * d_idx:(h + 1) * d_idx] for h in range(n_idx)]
    for h in range(n_idx):
        wb_scr[h] = jnp.broadcast_to(qw[:, d_idx + h:d_idx + h + 1] * (d_idx ** -0.5), (tq, LANE))

    def score_block(kb, carry):
        ks = pl.multiple_of(kb * tk, tk)
        ki = ki_ref[pl.ds(ks, tk), :][:, :d_idx]
        sc = jnp.zeros((tq, tk), F32)
        for h in range(n_idx):
            sc = sc + jnp.maximum(_dot_nt(qi_heads[h], ki), 0.0) * tile(wb_scr[h])
        sc = sc * (n_idx ** -0.5)
        sc = jnp.where(sc == 0.0, 0.0, sc)
        k_pos = ks + lane
        allowed = ((k_pos >> CHUNK_SHIFT) <= (q_pos >> CHUNK_SHIFT)) & (k_pos < s_valid)
        sc = jnp.where(allowed, sc, -jnp.inf)
        bits = pltpu.bitcast(sc, I32)
        key_scr[kb] = bits ^ ((bits >> 31) & 0x7FFFFFFF)
        return carry

    lax.fori_loop(0, nkb, score_block, 0)

    def count(pred):
        def body(kb, acc):
            key = key_scr[kb]
            for c in range(reps):
                acc = acc + jnp.where(pred(kb, c, key[:, c * LANE:(c + 1) * LANE]), 1.0, 0.0)
            return acc
        acc = lax.fori_loop(0, nkb, body, jnp.zeros((tq, LANE), F32))
        return jnp.sum(acc, axis=1, keepdims=True)

    def bisect_cond(state):
        it, _, _, open_rows = state
        return (it < 32) & (open_rows > 0.0)

    def bisect_body(state):
        it, t_u, cnt_t, _ = state
        cand_u = t_u | (jnp.int32(1) << (31 - it))
        cand_b = jnp.broadcast_to(cand_u ^ INT_MIN, (tq, LANE))
        cnt = count(lambda kb, c, key: key >= cand_b)
        take = cnt >= k_sel
        t_u = jnp.where(take, cand_u, t_u)
        cnt_t = jnp.where(take, cnt, cnt_t)
        return it + 1, t_u, cnt_t, jnp.max(jnp.where(cnt_t > k_sel, 1.0, 0.0))

    total = (nkb * tk).astype(F32)
    _, t_u, cnt_t, _ = lax.while_loop(
        bisect_cond, bisect_body,
        (jnp.int32(0), jnp.zeros((tq, 1), I32), jnp.full((tq, 1), total, F32),
         jnp.where(total > k_sel, 1.0, 0.0)))
    thr = t_u ^ INT_MIN
    thr_b = jnp.broadcast_to(thr, (tq, LANE))

    def tie_search():
        need = k_sel - count(lambda kb, c, key: key > thr_b)

        def index_bit(it, j_hi):
            cand = j_hi | (jnp.int32(1) << (idx_bits - 1 - it))
            cand_b = jnp.broadcast_to(cand, (tq, LANE))
            cnt = count(lambda kb, c, key: (key == thr_b) & (kb * tk + c * LANE + lane1 < cand_b))
            return jnp.where(cnt < need, cand, j_hi)

        return lax.fori_loop(0, idx_bits, index_bit, jnp.zeros((tq, 1), I32))

    tie_hi = lax.cond(jnp.max(cnt_t) > k_sel, tie_search, lambda: jnp.full((tq, 1), 2 ** 30, I32))

    m_scr[...] = jnp.full(m_scr.shape, MASKED, F32)
    acc_scr[...] = jnp.zeros(acc_scr.shape, F32)
    scale2 = (hd ** -0.5) * LOG2E
    q_groups = [jnp.concatenate([q_ref[:, (g * gqa + r) * hd:(g * gqa + r + 1) * hd]
                                 for r in range(gqa)], axis=0) for g in range(n_kv)]
    far_rows = [jnp.concatenate([jnp.full((tq, LANE), far_ref[g * gqa + r], F32)
                                 for r in range(gqa)], axis=0) for g in range(n_kv)]
    ones = jnp.ones((tk, hd), BF16)

    def attend_block(kb, near):
        ks = pl.multiple_of(kb * tk, tk)
        key = key_scr[kb]
        sel = (key > thr) | ((key == thr) & (ks + lane <= tie_hi))
        sel = sel & (key > KEY_NEG_INF)
        mbias = jnp.where(sel, 0.0, MASKED)
        for g in range(n_kv):
            kg = k_ref[pl.ds(ks, tk), g * hd:(g + 1) * hd]
            vext = jnp.concatenate([v_ref[pl.ds(ks, tk), g * hd:(g + 1) * hd], ones], axis=1)
            s = _dot_nt(q_groups[g], kg) * scale2
            m_old = m_scr[g]
            if near is None:
                t = s + jnp.concatenate([mbias] * gqa, axis=0)
                m_new = jnp.maximum(m_old, jnp.max(t, axis=-1, keepdims=True) + far_rows[g])
                p = jnp.exp2(t - tile(m_new - far_rows[g]))
            else:
                t = s + jnp.concatenate([mbias + bd_ref[g * gqa + r, near] for r in range(gqa)], axis=0)
                m_new = jnp.maximum(m_old, jnp.max(t, axis=-1, keepdims=True))
                p = jnp.exp2(t - tile(m_new))
            alpha = jnp.exp2(m_old - m_new)
            acc_scr[g] = (jnp.concatenate([alpha, alpha], axis=1) * acc_scr[g]
                          + jnp.dot(p.astype(BF16), vext, preferred_element_type=F32))
            m_scr[g] = m_new

    def far_block(kb, carry):
        attend_block(kb, None)
        return carry

    lax.fori_loop(0, jnp.maximum(kb_diag - 1, 0), far_block, 0)

    @pl.when(kb_diag >= 1)
    def _():
        attend_block(kb_diag - 1, 1)

    attend_block(kb_diag, 0)

    for g in range(n_kv):
        acc = acc_scr[g]
        o = acc[:, :hd] / acc[:, hd:]
        for r in range(gqa):
            h = g * gqa + r
            o_ref[:, h * hd:(h + 1) * hd] = o[r * tq:(r + 1) * tq].astype(BF16)


def _dsa_attn(q_src, k_src, bias_near, bias_far, *, n_batch, t, s_pad, s_valid, q_off, tq, tk,
              k_sel, n_heads, n_kv, hd, n_idx, d_idx):
    qarr, q_rb0, q_cb, qi_cb, qw_cb = q_src
    karr, k_cb, varr, v_cb, kiarr, ki_cb = k_src
    assert tk >= MAX_DISTANCE and tk % tq == 0 and q_off % tq == 0 and s_pad % tk == 0
    nqb = t // tq
    nkb_max = s_pad // tk
    gqa = n_heads // n_kv
    wq, wkv, wqi = n_heads * hd, n_kv * hd, n_idx * d_idx
    qrow = lambda b, i: q_rb0 + b * nqb + i
    kernel = functools.partial(
        _dsa_kernel, tq=tq, tk=tk, q_off=q_off, s_valid=s_valid, k_sel=k_sel, n_heads=n_heads,
        n_kv=n_kv, hd=hd, n_idx=n_idx, d_idx=d_idx, idx_bits=max((s_pad - 1).bit_length(), 1))
    return pl.pallas_call(
        kernel,
        grid=(n_batch, nqb),
        in_specs=[pl.BlockSpec(memory_space=pltpu.SMEM),
                  pl.BlockSpec((tq, wq), lambda b, i: (qrow(b, i), q_cb)),
                  pl.BlockSpec((tq, wqi), lambda b, i: (qrow(b, i), qi_cb)),
                  pl.BlockSpec((tq, LANE), lambda b, i: (qrow(b, i), qw_cb)),
                  _resident((s_pad, wkv), lambda b, i: (b, k_cb)),
                  _resident((s_pad, wkv), lambda b, i: (b, v_cb)),
                  _resident((s_pad, LANE), lambda b, i: (b, ki_cb)),
                  pl.BlockSpec((None, n_heads, 2, tq, tk),
                               lambda b, i: (_near_phase(q_off, tq, tk, nqb, i), 0, 0, 0, 0))],
        out_specs=pl.BlockSpec((tq, wq), lambda b, i: (b * nqb + i, 0)),
        out_shape=jax.ShapeDtypeStruct((n_batch * t, wq), BF16),
        scratch_shapes=[pltpu.VMEM((nkb_max, tq, tk), I32),
                        pltpu.VMEM((n_idx, tq, LANE), F32),
                        pltpu.VMEM((n_kv, gqa * tq, LANE), F32),
                        pltpu.VMEM((n_kv, gqa * tq, 2 * hd), F32)],
        compiler_params=_cparams(2),
        name="dsa_attn",
    )(bias_far, qarr, qarr, qarr, karr, varr, kiarr, bias_near)


def _stick_kernel(q_ref, k_ref, v_ref, o_ref, carry_scr, acc_scr, *, tq, tk, q_off, hd):
    i = pl.program_id(2)
    q0 = q_off + i * tq
    q_pos = q0 + lax.broadcasted_iota(I32, (tq, 1), 0)
    lane = lax.broadcasted_iota(I32, (1, tk), 1)
    tri = (lax.broadcasted_iota(I32, (tk, tk), 0) > lax.broadcasted_iota(I32, (tk, tk), 1)).astype(BF16)
    q = q_ref[...]
    scale = hd ** -0.5
    carry_scr[...] = jnp.zeros(carry_scr.shape, F32)
    acc_scr[...] = jnp.zeros(acc_scr.shape, F32)

    def cond(state):
        kb, carry_max = state
        return (kb >= 0) & (carry_max > EXP_UNDERFLOW)

    def body(state):
        kb, _ = state
        ks = pl.multiple_of(kb * tk, tk)
        z = _dot_nt(q, k_ref[pl.ds(ks, tk), :]) * scale
        before = (ks + lane) < q_pos
        log_beta = jnp.minimum(z, 0.0) - jnp.log(1.0 + jnp.exp(-jnp.abs(z)))
        log_keep = jnp.where(before, log_beta - z, 0.0)
        hi = log_keep.astype(BF16)
        lo = (log_keep - hi.astype(F32)).astype(BF16)
        tail = (jnp.dot(hi, tri, preferred_element_type=F32)
                + jnp.dot(lo, tri, preferred_element_type=F32))
        carry = carry_scr[...]
        w = jnp.where(before, jnp.exp(log_beta + tail + carry), 0.0)
        acc_scr[...] += jnp.dot(w.astype(BF16), v_ref[pl.ds(ks, tk), :], preferred_element_type=F32)
        carry = carry + jnp.sum(log_keep, axis=-1, keepdims=True)
        carry_scr[...] = carry
        return kb - 1, jnp.max(carry)

    lax.while_loop(cond, body, ((q0 + tq - 1) // tk, jnp.float32(0.0)))
    o_ref[...] = acc_scr[...].astype(BF16)


def _stick_attn(q_src, k_src, *, n_batch, t, s_pad, q_off, tq, tk, n_heads, hd):
    qarr, q_rb0, q_cb0 = q_src
    karr, k_cb0, varr, v_cb0 = k_src
    nqb = t // tq
    return pl.pallas_call(
        functools.partial(_stick_kernel, tq=tq, tk=tk, q_off=q_off, hd=hd),
        grid=(n_batch, n_heads, nqb),
        in_specs=[pl.BlockSpec((tq, hd), lambda b, h, i: (q_rb0 + b * nqb + i, q_cb0 + h)),
                  pl.BlockSpec((s_pad, hd), lambda b, h, i: (b, k_cb0 + h)),
                  pl.BlockSpec((s_pad, hd), lambda b, h, i: (b, v_cb0 + h))],
        out_specs=pl.BlockSpec((tq, hd), lambda b, h, i: (b * nqb + i, h)),
        out_shape=jax.ShapeDtypeStruct((n_batch * t, n_heads * hd), BF16),
        scratch_shapes=[pltpu.VMEM((tq, 1), F32), pltpu.VMEM((tq, hd), F32)],
        compiler_params=_cparams(3),
        name="stick_attn",
    )(qarr, karr, varr)


def _diff_kernel(scal_ref, q_ref, k_ref, v_ref, bd_ref, g_ref, o_ref, m_scr, acc_scr, *,
                 tq, tk, q_off, s_valid, hd, out_scale):
    h = pl.program_id(1)
    i = pl.program_id(2)
    reps = tk // LANE
    q0 = q_off + i * tq
    kb_diag = q0 // tk
    q_pos = q0 + lax.broadcasted_iota(I32, (tq, 1), 0)
    lane = lax.broadcasted_iota(I32, (1, tk), 1)
    tile = lambda x: jnp.concatenate([x] * reps, axis=1)
    scale2 = (hd ** -0.5) * LOG2E
    m_scr[...] = jnp.full(m_scr.shape, MASKED, F32)
    acc_scr[...] = jnp.zeros(acc_scr.shape, F32)
    q_maps = [q_ref[:, m * hd:(m + 1) * hd] for m in range(2)]
    ones = jnp.ones((tk, hd), BF16)

    def attend_block(kb, near):
        ks = pl.multiple_of(kb * tk, tk)
        vext = jnp.concatenate([v_ref[pl.ds(ks, tk), :], ones], axis=1)
        if near is not None:
            k_pos = ks + lane
            allowed = ((k_pos >> CHUNK_SHIFT) <= (q_pos >> CHUNK_SHIFT)) & (k_pos < s_valid)
            mbias = jnp.where(allowed, 0.0, MASKED)
        for m in range(2):
            s = _dot_nt(q_maps[m], k_ref[pl.ds(ks, tk), m * hd:(m + 1) * hd]) * scale2
            m_old = m_scr[m]
            if near is None:
                far = scal_ref[1 + 2 * h + m]
                m_new = jnp.maximum(m_old, jnp.max(s, axis=-1, keepdims=True) + far)
                p = jnp.exp2(s - tile(m_new - far))
            else:
                t = s + (mbias + bd_ref[m, near])
                m_new = jnp.maximum(m_old, jnp.max(t, axis=-1, keepdims=True))
                p = jnp.exp2(t - tile(m_new))
            alpha = jnp.exp2(m_old - m_new)
            acc_scr[m] = (jnp.concatenate([alpha] * 3, axis=1) * acc_scr[m]
                          + jnp.dot(p.astype(BF16), vext, preferred_element_type=F32))
            m_scr[m] = m_new

    def far_block(kb, carry):
        attend_block(kb, None)
        return carry

    lax.fori_loop(0, jnp.maximum(kb_diag - 1, 0), far_block, 0)

    @pl.when(kb_diag >= 1)
    def _():
        attend_block(kb_diag - 1, 1)

    attend_block(kb_diag, 0)

    def normalised(m):
        acc = acc_scr[m]
        return acc[:, :2 * hd] / jnp.concatenate([acc[:, 2 * hd:]] * 2, axis=1)

    o = normalised(0) - scal_ref[0] * normalised(1)
    o = o * lax.rsqrt(jnp.mean(o * o, axis=-1, keepdims=True) + LN_EPS) * g_ref[...]
    o_ref[...] = (o * out_scale).astype(BF16)


def _diff_attn(q_src, k_src, bias_near, scalars, subln_g, *, n_batch, t, s_pad, s_valid, q_off,
               tq, tk, n_heads, hd, out_scale):
    qarr, q_rb0, q_cb0 = q_src
    karr, k_cb0, varr, v_cb0 = k_src
    assert tk >= MAX_DISTANCE and tk % tq == 0 and q_off % tq == 0 and s_pad % tk == 0
    nqb = t // tq
    w = 2 * hd
    return pl.pallas_call(
        functools.partial(_diff_kernel, tq=tq, tk=tk, q_off=q_off, s_valid=s_valid, hd=hd,
                          out_scale=out_scale),
        grid=(n_batch, n_heads, nqb),
        in_specs=[pl.BlockSpec(memory_space=pltpu.SMEM),
                  pl.BlockSpec((tq, w), lambda b, h, i: (q_rb0 + b * nqb + i, q_cb0 + h)),
                  pl.BlockSpec((s_pad, w), lambda b, h, i: (b, k_cb0 + h)),
                  pl.BlockSpec((s_pad, w), lambda b, h, i: (b, v_cb0 + h)),
                  pl.BlockSpec((None, None, 2, 2, tq, tk),
                               lambda b, h, i: (_near_phase(q_off, tq, tk, nqb, i), h, 0, 0, 0, 0)),
                  pl.BlockSpec((1, w), lambda b, h, i: (0, 0))],
        out_specs=pl.BlockSpec((tq, w), lambda b, h, i: (b * nqb + i, h)),
        out_shape=jax.ShapeDtypeStruct((n_batch * t, n_heads * w), BF16),
        scratch_shapes=[pltpu.VMEM((2, tq, LANE), F32), pltpu.VMEM((2, tq, w + hd), F32)],
        compiler_params=_cparams(3),
        name="diff_attn",
    )(scalars, qarr, karr, varr, bias_near, subln_g)


def _router_kernel(x_ref, wh_ref, wl_ref, b_ref, e_ref, g_ref, r_ref, c_ref, seen_scr):
    @pl.when(pl.program_id(0) == 0)
    def _():
        seen_scr[...] = jnp.zeros(seen_scr.shape, F32)

    x = x_ref[...]
    xh = x.astype(BF16)
    xl = (x - xh.astype(F32)).astype(BF16)
    wh = wh_ref[...]
    logits = (jnp.dot(xh, wh, preferred_element_type=F32)
              + jnp.dot(xl, wh, preferred_element_type=F32)
              + jnp.dot(xh, wl_ref[...], preferred_element_type=F32)) + b_ref[...]
    tm = logits.shape[0]
    lane = lax.broadcasted_iota(I32, logits.shape, 1)
    lane_f = lane.astype(F32)
    experts = jnp.zeros(logits.shape, I32)
    gates = jnp.zeros(logits.shape, F32)
    chosen = []
    top = None
    for k in range(TOP_K):
        m = jnp.max(logits, axis=-1, keepdims=True)
        ix = jnp.min(jnp.where(logits == m, lane_f, float(LANE)), axis=-1, keepdims=True).astype(I32)
        top = m if top is None else top
        experts = jnp.where(lane == k, ix, experts)
        gates = jnp.where(lane == k, jnp.exp(m - top), gates)
        chosen.append(lane == ix)
        logits = jnp.where(chosen[-1], -jnp.inf, logits)
    e_ref[...] = experts
    g_ref[...] = gates / jnp.sum(gates, axis=-1, keepdims=True)

    hot = functools.reduce(jnp.logical_or, chosen)
    hot_f = jnp.where(hot, 1.0, 0.0)
    lower = (lax.broadcasted_iota(I32, (tm, tm), 0) > lax.broadcasted_iota(I32, (tm, tm), 1)).astype(BF16)
    earlier = jnp.dot(lower, hot_f.astype(BF16), preferred_element_type=F32) + seen_scr[...]
    ranks = jnp.zeros(logits.shape, F32)
    for k in range(TOP_K):
        rank_k = jnp.sum(jnp.where(chosen[k], earlier, 0.0), axis=-1, keepdims=True)
        ranks = jnp.where(lane == k, rank_k, ranks)
    r_ref[...] = ranks.astype(I32)
    seen = seen_scr[...] + jnp.sum(hot_f, axis=0, keepdims=True)
    seen_scr[...] = seen
    c_ref[...] = seen


def _router(x, w_hi, w_lo, bias):
    m, d = x.shape
    tm = _pick_tile(m, (256, 128, 64, 32, 16))
    row = lambda i: (i, 0)
    fix = lambda i: (0, 0)
    return pl.pallas_call(
        _router_kernel,
        grid=(m // tm,),
        in_specs=[pl.BlockSpec((tm, d), row), pl.BlockSpec((d, LANE), fix),
                  pl.BlockSpec((d, LANE), fix), pl.BlockSpec((1, LANE), fix)],
        out_specs=[pl.BlockSpec((tm, LANE), row), pl.BlockSpec((tm, LANE), row),
                   pl.BlockSpec((tm, LANE), row), pl.BlockSpec((1, LANE), fix)],
        out_shape=[jax.ShapeDtypeStruct((m, LANE), I32), jax.ShapeDtypeStruct((m, LANE), F32),
                   jax.ShapeDtypeStruct((m, LANE), I32), jax.ShapeDtypeStruct((1, LANE), F32)],
        scratch_shapes=[pltpu.VMEM((1, LANE), F32)],
        compiler_params=_cparams(1),
        name="router",
    )(x, w_hi, w_lo, bias)


def _dispatch_kernel(dest_ref, x_ref, rows_in, rows_out, sem, *, tm):
    del rows_in

    def send(r, carry):
        for k in range(TOP_K):
            _row_copy(x_ref, r, rows_out, dest_ref[0, r * TOP_K + k], sem).start()
        return carry
    lax.fori_loop(0, tm, send, 0)

    def drain(r, carry):
        for k in range(TOP_K):
            _row_copy(x_ref, 0, rows_out, 0, sem).wait()
        return carry
    lax.fori_loop(0, tm, drain, 0)


def _dispatch(x_packed, dest, n_rows):
    m, w = x_packed.shape
    tm = _pick_tile(m, (256, 128, 64, 32, 16))
    n_steps = m // tm
    return pl.pallas_call(
        functools.partial(_dispatch_kernel, tm=tm),
        grid=(n_steps,),
        in_specs=[pl.BlockSpec((None, 1, tm * TOP_K), lambda i: (i, 0, 0), memory_space=pltpu.SMEM),
                  pl.BlockSpec((tm, w), lambda i: (i, 0)),
                  pl.BlockSpec(memory_space=pl.ANY)],
        out_specs=pl.BlockSpec(memory_space=pl.ANY),
        out_shape=jax.ShapeDtypeStruct((n_rows, w), I32),
        scratch_shapes=[pltpu.SemaphoreType.DMA(())],
        input_output_aliases={2: 0},
        compiler_params=_cparams(1),
        name="dispatch",
    )(dest.reshape(n_steps, 1, tm * TOP_K), x_packed, jnp.zeros((n_rows, w), I32))


def _cast_rows(src_ref, dst_ref, rows_per_step):
    def body(c, carry):
        r = pl.multiple_of(c * rows_per_step, rows_per_step)
        dst_ref[pl.ds(r, rows_per_step), :] = src_ref[pl.ds(r, rows_per_step), :].astype(BF16)
        return carry
    lax.fori_loop(0, src_ref.shape[0] // rows_per_step, body, 0)


def _expert_changed(be_ref, j):
    return (j == 0) | (be_ref[j] != be_ref[jnp.maximum(j - 1, 0)])


def _moe_up_kernel(be_ref, nu_ref, x_ref, wg_ref, wl_ref, bg_ref, bl_ref, h_ref, wg_bf, wl_bf, *,
                   cast_rows):
    j = pl.program_id(1)

    @pl.when(_expert_changed(be_ref, j))
    def _():
        _cast_rows(wg_ref, wg_bf, cast_rows)
        _cast_rows(wl_ref, wl_bf, cast_rows)

    @pl.when(j < nu_ref[0])
    def _():
        x = _unpack_halves(x_ref[...])
        glu = jnp.dot(x, wg_bf[...], preferred_element_type=F32) + bg_ref[...]
        lin = jnp.dot(x, wl_bf[...], preferred_element_type=F32) + bl_ref[...]
        glu = jnp.minimum(glu, SWIGLU_LIMIT)
        lin = jnp.clip(lin, -SWIGLU_LIMIT, SWIGLU_LIMIT)
        h_ref[...] = (glu * jax.nn.sigmoid(SWIGLU_ALPHA * glu) * (lin + 1.0)).astype(BF16)

    @pl.when(j >= nu_ref[0])
    def _():
        h_ref[...] = jnp.zeros(h_ref.shape, BF16)


def _moe_down_kernel(be_ref, nu_ref, h_ref, wd_ref, bd_ref, y_ref, wd_bf, *, cast_rows):
    j = pl.program_id(1)

    @pl.when(_expert_changed(be_ref, j))
    def _():
        _cast_rows(wd_ref, wd_bf, cast_rows)

    @pl.when(j < nu_ref[0])
    def _():
        y_ref[...] = jnp.dot(h_ref[...], wd_bf[...], preferred_element_type=F32) + bd_ref[...]

    @pl.when(j >= nu_ref[0])
    def _():
        y_ref[...] = jnp.zeros(y_ref.shape, F32)


def _moe_experts(xs, block_e, n_used, layer, w_gate_up, b_gate_up, w_down, b_down, tm):
    n_rows = xs.shape[0]
    d = 2 * xs.shape[1]
    n_blocks = n_rows // tm
    d_ff = w_down.shape[2]
    tf = _pick_tile(d_ff, (512, 256, 128))
    nf = d_ff // tf
    tn = _pick_tile(d, (512, 256, 128))
    blk = lambda j, nu: jnp.minimum(j, nu[0] - 1)
    hidden = pl.pallas_call(
        functools.partial(_moe_up_kernel, cast_rows=_pick_tile(d, (256, 128, 64, 32, 16, 8))),
        grid_spec=pltpu.PrefetchScalarGridSpec(
            num_scalar_prefetch=2,
            grid=(nf, n_blocks),
            in_specs=[pl.BlockSpec((tm, d // 2), lambda f, j, be, nu: (blk(j, nu), 0)),
                      pl.BlockSpec((None, None, d, tf), lambda f, j, be, nu: (layer, be[j], 0, f)),
                      pl.BlockSpec((None, None, d, tf), lambda f, j, be, nu: (layer, be[j], 0, nf + f)),
                      pl.BlockSpec((None, None, 1, tf), lambda f, j, be, nu: (layer, be[j], 0, f)),
                      pl.BlockSpec((None, None, 1, tf), lambda f, j, be, nu: (layer, be[j], 0, nf + f))],
            out_specs=pl.BlockSpec((tm, tf), lambda f, j, be, nu: (j, f)),
            scratch_shapes=[pltpu.VMEM((d, tf), BF16), pltpu.VMEM((d, tf), BF16)]),
        out_shape=jax.ShapeDtypeStruct((n_rows, d_ff), BF16),
        compiler_params=_cparams(2),
        name="moe_up",
    )(block_e, n_used, xs, w_gate_up, w_gate_up, b_gate_up, b_gate_up)
    return pl.pallas_call(
        functools.partial(_moe_down_kernel, cast_rows=_pick_tile(d_ff, (256, 128, 64, 32, 16, 8))),
        grid_spec=pltpu.PrefetchScalarGridSpec(
            num_scalar_prefetch=2,
            grid=(d // tn, n_blocks),
            in_specs=[pl.BlockSpec((tm, d_ff), lambda n, j, be, nu: (blk(j, nu), 0)),
                      pl.BlockSpec((None, None, d_ff, tn), lambda n, j, be, nu: (layer, be[j], 0, n)),
                      pl.BlockSpec((None, None, 1, tn), lambda n, j, be, nu: (layer, be[j], 0, n))],
            out_specs=pl.BlockSpec((tm, tn), lambda n, j, be, nu: (j, n)),
            scratch_shapes=[pltpu.VMEM((d_ff, tn), BF16)]),
        out_shape=jax.ShapeDtypeStruct((n_rows, d), F32),
        compiler_params=_cparams(2),
        name="moe_down",
    )(block_e, n_used, hidden, w_down, b_down)


def _moe(x_f32, x_packed, layer, w_router_hi, w_router_lo, b_router, w_gate_up, b_gate_up,
         w_down, b_down):
    n = x_f32.shape[0]
    n_experts = w_gate_up.shape[1]
    tm = _pick_tile(n * TOP_K, (256, 128, 64, 32, 16))
    experts, gates, ranks, counts = _router(x_f32, w_router_hi, w_router_lo, b_router)
    counts = counts[0, :n_experts].astype(I32)
    pend = jnp.cumsum((counts + tm - 1) // tm * tm)
    pstart = pend - (counts + tm - 1) // tm * tm
    n_blocks = -(-(n * TOP_K + n_experts * (tm - 1)) // tm)
    block_e = jnp.minimum(jnp.searchsorted(pend, jnp.arange(n_blocks) * tm, side='right'),
                          n_experts - 1).astype(I32)
    n_used = (pend[-1] // tm).astype(I32).reshape(1)
    e4 = experts[:, :TOP_K]
    onehot = e4[..., None] == jnp.arange(n_experts, dtype=I32)
    dest = (ranks[:, :TOP_K] + jnp.sum(jnp.where(onehot, pstart.astype(I32), 0), axis=-1)).reshape(-1)
    xs = _dispatch(x_packed, dest, n_blocks * tm)
    y = _moe_experts(xs, block_e, n_used, layer, w_gate_up, b_gate_up, w_down, b_down, tm)
    return y, dest, gates


def _pad_cols(w, n):
    return jnp.pad(w, ((0, 0), (0, n - w.shape[1])))


def _keys_with_past(past, new, s_pad):
    b, p, w = past.shape
    pad = jnp.zeros((b, s_pad - p - new.shape[1], w), BF16)
    return jnp.concatenate([past.astype(BF16), new.astype(BF16), pad], axis=1).reshape(b * s_pad, w)


def kernel(x_prompt, x_sample, cache_a_k, cache_a_v, cache_a_kidx, cache_b_k, cache_b_v,
           cache_c_k, cache_c_v, cache_mem_k, cache_mem_v, mem_prompt,
           w_in_a, w_in_b, w_in_c, diff_lambda, diff_subln, rel_bias, w_mem_kv, w_o,
           ln_g, ln_b, w_router, b_router, w_gate_up, b_gate_up, w_down, b_down):
    bp, tp, d = x_prompt.shape
    bs, ts, _ = x_sample.shape
    depth = w_o.shape[0]
    past_len = cache_a_k.shape[2]
    n_kv, hd = cache_a_k.shape[3], cache_a_k.shape[4]
    d_idx = cache_a_kidx.shape[3]
    n_mem, h_mem = cache_mem_k.shape[2], cache_mem_k.shape[3]
    mem_w = h_mem * hd
    tok_w = d - mem_w
    h_a = tok_w // hd
    h_b = cache_b_k.shape[3]
    hc = cache_c_k.shape[3]
    n_idx = (w_in_a.shape[2] - tok_w - 2 * n_kv * hd - d_idx - mem_w) // (d_idx + 1)
    n_experts = w_router.shape[2]
    alpha = (2 * depth) ** 0.25
    mp, ms = bp * tp, bs * ts
    s_new = past_len + ts
    kv_w, qi_w = n_kv * hd, n_idx * d_idx

    o_k, o_v, o_qi = tok_w, tok_w + kv_w, tok_w + 2 * kv_w
    o_ki = o_qi + qi_w
    o_wi = o_ki + d_idx
    o_mq = o_wi + n_idx
    a_mq = o_qi + qi_w
    a_kw = a_mq + mem_w
    cols_a = _round_up(a_kw + LANE, 5 * LANE if d >= 5 * LANE else LANE)
    w_a = [_pad_cols(jnp.concatenate([w[:, :o_ki], w[:, o_mq:], w[:, o_ki:o_mq]], axis=1),
                     cols_a).astype(BF16) for w in w_in_a]
    w_b = [w.astype(BF16) for w in w_in_b]
    w_c = [w.astype(BF16) for w in w_in_c]
    w_o_bf = w_o.astype(BF16)
    w_r_pad = jnp.pad(w_router, ((0, 0), (0, 0), (0, LANE - n_experts)))
    w_r_hi = w_r_pad.astype(BF16)
    w_r_lo = (w_r_pad - w_r_hi.astype(F32)).astype(BF16)
    b_r_pad = jnp.pad(b_router, ((0, 0), (0, LANE - n_experts)), constant_values=MASKED)[:, None, :]
    b_gu = b_gate_up[:, :, None, :]
    b_dn = b_down[:, :, None, :]

    tq_a = _pick_tile(tp, (128, 64))
    tk_a = _pick_tile(tp, (512, 256, 128))
    tq_b = _pick_tile(tp, (256, 128, 64))
    tk_b = min(tq_b, 128)
    tq_c = _pick_tile(tp, (256, 128))
    tk_c = _pick_tile(tp, (512, 256, 128))
    tq_m = _pick_tile(tp, (512, 256, 128))
    sp_a, sp_b, sp_c = _round_up(s_new, tk_a), _round_up(s_new, tk_b), _round_up(s_new, tk_c)

    far2 = _far_bias(rel_bias) * LOG2E
    near_a_p = _near_bias(rel_bias, tq_a, tk_a, 0, tp // tq_a) * LOG2E
    near_a_s = _near_bias(rel_bias, ts, tk_a, past_len, 1) * LOG2E
    near_c_p = _near_bias(rel_bias, tq_c, tk_c, 0, tp // tq_c) * LOG2E
    near_c_s = _near_bias(rel_bias, ts, tk_c, past_len, 1) * LOG2E
    pair = lambda nb: nb.reshape(nb.shape[0], hc, 2, *nb.shape[2:])

    w_mem = jnp.moveaxis(w_mem_kv, 0, 1).reshape(d, depth * 2 * mem_w).astype(BF16)
    mem_f, mem_b = _matmul(mem_prompt.reshape(bp * n_mem, d).astype(BF16), w_mem)
    mem_f = mem_f.reshape(bp, n_mem, depth, 2, h_mem, hd)
    mem_k_p = jnp.moveaxis(mem_f[:, :, :, 0], 2, 0)
    mem_v_p = jnp.moveaxis(mem_f[:, :, :, 1], 2, 0)
    mem_s_k = cache_mem_k.reshape(depth, bs * n_mem, mem_w).astype(BF16)
    mem_s_v = cache_mem_v.reshape(depth, bs * n_mem, mem_w).astype(BF16)

    x_f = jnp.concatenate([x_prompt.reshape(mp, d), x_sample.reshape(ms, d)], axis=0)
    x_b = x_f.astype(BF16)
    rows_p = [None] * depth
    rows_s = [None] * depth

    for i in range(depth):
        kind, j = i % N_MIXERS, i // N_MIXERS
        w_in = (w_a, w_b, w_c)[kind][j]
        h_f, h_b_ = _matmul(x_b, w_in)
        hs_f = h_f[mp:].reshape(bs, ts, -1)
        hp = lambda lo, hi: h_f[:mp, lo:hi]
        rb_s = mp // ts

        if kind == 0:
            k_sel_p = min(K_SEL_MAX, tp // 4)
            k_sel_s = min(K_SEL_MAX, s_new // 4)
            dims = dict(n_heads=h_a, n_kv=n_kv, hd=hd, n_idx=n_idx, d_idx=d_idx)
            mix_p = _dsa_attn(
                (h_b_, 0, 0, o_qi // qi_w, a_kw // LANE),
                (h_b_, o_k // kv_w, h_b_, o_v // kv_w, h_b_, a_kw // LANE),
                near_a_p, far2, n_batch=bp, t=tp, s_pad=tp, s_valid=tp, q_off=0,
                tq=tq_a, tk=tk_a, k_sel=k_sel_p, **dims)
            kiw_new = jnp.pad(hs_f[..., a_kw:a_kw + d_idx], ((0, 0), (0, 0), (0, LANE - d_idx)))
            kiw_past = jnp.pad(cache_a_kidx[j], ((0, 0), (0, 0), (0, LANE - d_idx)))
            k_all = _keys_with_past(cache_a_k[j].reshape(bs, past_len, kv_w), hs_f[..., o_k:o_v], sp_a)
            v_all = _keys_with_past(cache_a_v[j].reshape(bs, past_len, kv_w), hs_f[..., o_v:o_qi], sp_a)
            ki_all = _keys_with_past(kiw_past, kiw_new, sp_a)
            mix_s = _dsa_attn(
                (h_b_, rb_s, 0, o_qi // qi_w, a_kw // LANE),
                (k_all, 0, v_all, 0, ki_all, 0),
                near_a_s, far2, n_batch=bs, t=ts, s_pad=sp_a, s_valid=s_new, q_off=past_len,
                tq=ts, tk=tk_a, k_sel=k_sel_s, **dims)
            mq_cb = a_mq // mem_w
            rows_p[i] = (hp(o_k, o_v).reshape(bp, tp, n_kv, hd), hp(o_v, o_qi).reshape(bp, tp, n_kv, hd),
                         hp(a_kw, a_kw + d_idx).reshape(bp, tp, d_idx))
            rows_s[i] = (hs_f[..., o_k:o_v].reshape(bs, ts, n_kv, hd),
                         hs_f[..., o_v:o_qi].reshape(bs, ts, n_kv, hd), hs_f[..., a_kw:a_kw + d_idx])
        elif kind == 1:
            dims = dict(n_heads=h_b, hd=hd)
            mix_p = _stick_attn((h_b_, 0, 0), (h_b_, h_b, h_b_, 2 * h_b),
                                n_batch=bp, t=tp, s_pad=tp, q_off=0, tq=tq_b, tk=tk_b, **dims)
            k_all = _keys_with_past(cache_b_k[j].reshape(bs, past_len, tok_w), hs_f[..., tok_w:2 * tok_w], sp_b)
            v_all = _keys_with_past(cache_b_v[j].reshape(bs, past_len, tok_w), hs_f[..., 2 * tok_w:3 * tok_w], sp_b)
            mix_s = _stick_attn((h_b_, rb_s, 0), (k_all, 0, v_all, 0),
                                n_batch=bs, t=ts, s_pad=sp_b, q_off=past_len, tq=ts, tk=tk_b, **dims)
            mq_cb = 3 * tok_w // mem_w
            rows_p[i] = (hp(tok_w, 2 * tok_w).reshape(bp, tp, h_b, hd),
                         hp(2 * tok_w, 3 * tok_w).reshape(bp, tp, h_b, hd))
            rows_s[i] = (hs_f[..., tok_w:2 * tok_w].reshape(bs, ts, h_b, hd),
                         hs_f[..., 2 * tok_w:3 * tok_w].reshape(bs, ts, h_b, hd))
        else:
            lambda_init = 0.8 - 0.6 * math.exp(-0.3 * i)
            lv = diff_lambda[j].astype(F32)
            lam = jnp.exp(jnp.sum(lv[0] * lv[1])) - jnp.exp(jnp.sum(lv[2] * lv[3])) + lambda_init
            scalars = jnp.concatenate([lam.reshape(1), far2]).astype(F32)
            g = diff_subln[j].reshape(1, 2 * hd).astype(F32)
            dims = dict(n_heads=hc, hd=hd, out_scale=1.0 - lambda_init)
            mix_p = _diff_attn((h_b_, 0, 0), (h_b_, hc, h_b_, 2 * hc), pair(near_c_p), scalars, g,
                               n_batch=bp, t=tp, s_pad=tp, s_valid=tp, q_off=0, tq=tq_c, tk=tk_c, **dims)
            k_all = _keys_with_past(cache_c_k[j].reshape(bs, past_len, tok_w), hs_f[..., tok_w:2 * tok_w], sp_c)
            v_all = _keys_with_past(cache_c_v[j].reshape(bs, past_len, tok_w), hs_f[..., 2 * tok_w:3 * tok_w], sp_c)
            mix_s = _diff_attn((h_b_, rb_s, 0), (k_all, 0, v_all, 0), pair(near_c_s), scalars, g,
                               n_batch=bs, t=ts, s_pad=sp_c, s_valid=s_new, q_off=past_len,
                               tq=ts, tk=tk_c, **dims)
            mq_cb = 3 * tok_w // mem_w
            rows_p[i] = (hp(tok_w, 2 * tok_w).reshape(bp, tp, hc, 2, hd),
                         hp(2 * tok_w, 3 * tok_w).reshape(bp, tp, hc, 2 * hd))
            rows_s[i] = (hs_f[..., tok_w:2 * tok_w].reshape(bs, ts, hc, 2, hd),
                         hs_f[..., 2 * tok_w:3 * tok_w].reshape(bs, ts, hc, 2 * hd))

        mem_p = _mem_attn(h_b_, 0, mq_cb, mem_b[:, (2 * i) * mem_w:(2 * i + 1) * mem_w],
                          mem_b[:, (2 * i + 1) * mem_w:(2 * i + 2) * mem_w],
                          bp, tp, tq_m, h_mem, hd)
        mem_s = _mem_attn(h_b_, rb_s, mq_cb, mem_s_k[i], mem_s_v[i], bs, ts, ts, h_mem, hd)
        mix = jnp.concatenate([mix_p, mix_s], axis=0)
        mem_o = jnp.concatenate([mem_p, mem_s], axis=0)
        x_f, x_pk = _proj_ln(mix, mem_o, w_o_bf[i, :tok_w], w_o_bf[i, tok_w:], x_f,
                             ln_g[i, 0][None], ln_b[i, 0][None], alpha)
        y, dest, gates = _moe(x_f, x_pk, i, w_r_hi[i], w_r_lo[i], b_r_pad[i],
                              w_gate_up, b_gu, w_down, b_dn)
        x_f, x_b = _combine_ln(x_f, y, dest, gates, ln_g[i, 1][None], ln_b[i, 1][None], alpha)

    def stack_kind(rows, kind):
        sel = [r for i, r in enumerate(rows) if i % N_MIXERS == kind]
        return [jnp.stack(parts) for parts in zip(*sel)]

    y_prompt = x_f[:mp].reshape(bp, tp, d)
    y_sample = x_f[mp:].reshape(bs, ts, d)
    a_k_p, a_v_p, a_kidx_p = stack_kind(rows_p, 0)
    b_k_p, b_v_p = stack_kind(rows_p, 1)
    c_k_p, c_v_p = stack_kind(rows_p, 2)
    a_k_s, a_v_s, a_kidx_s = stack_kind(rows_s, 0)
    b_k_s, b_v_s = stack_kind(rows_s, 1)
    c_k_s, c_v_s = stack_kind(rows_s, 2)
    return (y_prompt, y_sample, a_k_p, a_v_p, a_kidx_p, b_k_p, b_v_p, c_k_p, c_v_p, mem_k_p, mem_v_p,
            a_k_s, a_v_s, a_kidx_s, b_k_s, b_v_s, c_k_s, c_v_s)
```

```python
import functools
import math

import jax
import jax.numpy as jnp
from jax import lax
from jax.experimental import pallas as pl
from jax.experimental.pallas import tpu as pltpu

F32 = jnp.float32
BF16 = jnp.bfloat16
I32 = jnp.int32

CHUNK = 64
CHUNK_SHIFT = CHUNK.bit_length() - 1
K_SEL_MAX = 256
MAX_DISTANCE = 128
TOP_K = 4
SWIGLU_LIMIT = 7.0
SWIGLU_ALPHA = 1.702
LN_EPS = 1e-5
N_MIXERS = 3

LANE = 128
VMEM_LIMIT = 56 * 1024 * 1024
MASKED = -1e30
EXP_UNDERFLOW = -104.0
LOG2E = 1.4426950408889634
INT_MIN = -2 ** 31
KEY_NEG_INF = -2139095041
HI16 = -65536

def _cparams(n_axes):
    return pltpu.CompilerParams(dimension_semantics=("arbitrary",) * n_axes,
                                vmem_limit_bytes=VMEM_LIMIT)


def _round_up(x, m):
    return -(-x // m) * m


def _pick_tile(n, candidates):
    for c in candidates:
        if n % c == 0:
            return c
    return n


def _dot_nt(a, b):
    return lax.dot_general(a, b, (((1,), (1,)), ((), ())), preferred_element_type=F32)


def _resident(block_shape, index_map):
    return pl.BlockSpec(block_shape, index_map, pipeline_mode=pl.Buffered(1))


def _mm_kernel(x_ref, w_ref, of_ref, ob_ref):
    acc = jnp.dot(x_ref[...], w_ref[...], preferred_element_type=F32)
    of_ref[...] = acc
    ob_ref[...] = acc.astype(BF16)


def _matmul(x, w):
    m, k = x.shape
    n = w.shape[1]
    tm = _pick_tile(m, (640, 512, 256, 128, 64, 32, 16))
    tn = _pick_tile(n, (1280, 1024, 768, 512, 384, 256, 128))
    return pl.pallas_call(
        _mm_kernel,
        grid=(n // tn, m // tm),
        in_specs=[pl.BlockSpec((tm, k), lambda j, i: (i, 0)),
                  pl.BlockSpec((k, tn), lambda j, i: (0, j))],
        out_specs=[pl.BlockSpec((tm, tn), lambda j, i: (i, j)),
                   pl.BlockSpec((tm, tn), lambda j, i: (i, j))],
        out_shape=[jax.ShapeDtypeStruct((m, n), F32), jax.ShapeDtypeStruct((m, n), BF16)],
        compiler_params=_cparams(2),
        name="matmul",
    )(x, w)


def _layer_norm_rows(z, g, b):
    mu = jnp.mean(z, axis=-1, keepdims=True)
    zc = z - mu
    var = jnp.mean(zc * zc, axis=-1, keepdims=True)
    return zc * lax.rsqrt(var + LN_EPS) * g + b


def _pack_halves(x):
    half = x.shape[1] // 2
    xb = x.astype(BF16).astype(F32)
    lo = lax.shift_right_logical(pltpu.bitcast(xb[:, :half], I32), 16)
    return lo | (pltpu.bitcast(xb[:, half:], I32) & HI16)


def _unpack_halves(p):
    lo = pltpu.bitcast(p << 16, F32).astype(BF16)
    hi = pltpu.bitcast(p & HI16, F32).astype(BF16)
    return jnp.concatenate([lo, hi], axis=1)


def _proj_ln_kernel(a1_ref, a2_ref, w1_ref, w2_ref, r_ref, g_ref, b_ref, of_ref, op_ref, *, alpha):
    y = jnp.dot(a1_ref[...], w1_ref[...], preferred_element_type=F32)
    y = y + jnp.dot(a2_ref[...], w2_ref[...], preferred_element_type=F32)
    out = _layer_norm_rows(alpha * r_ref[...] + y, g_ref[...], b_ref[...])
    of_ref[...] = out
    op_ref[...] = _pack_halves(out)


def _proj_ln(a1, a2, w1, w2, res, g, b, alpha):
    m, k1 = a1.shape
    k2 = a2.shape[1]
    d = w1.shape[1]
    tm = _pick_tile(m, (256, 128, 64, 32, 16))
    row = lambda i: (i, 0)
    fix = lambda i: (0, 0)
    return pl.pallas_call(
        functools.partial(_proj_ln_kernel, alpha=alpha),
        grid=(m // tm,),
        in_specs=[pl.BlockSpec((tm, k1), row), pl.BlockSpec((tm, k2), row),
                  pl.BlockSpec((k1, d), fix), pl.BlockSpec((k2, d), fix),
                  pl.BlockSpec((tm, d), row), pl.BlockSpec((1, d), fix), pl.BlockSpec((1, d), fix)],
        out_specs=[pl.BlockSpec((tm, d), row), pl.BlockSpec((tm, d // 2), row)],
        out_shape=[jax.ShapeDtypeStruct((m, d), F32), jax.ShapeDtypeStruct((m, d // 2), I32)],
        compiler_params=_cparams(1),
        name="proj_ln",
    )(a1, a2, w1, w2, res, g, b)


def _row_copy(src_ref, src_row, dst_ref, dst_row, sem):
    return pltpu.make_async_copy(src_ref.at[pl.ds(src_row, 1)], dst_ref.at[pl.ds(dst_row, 1)], sem)


def _combine_ln_kernel(dest_ref, next_ref, x_ref, gate_ref, g_ref, b_ref, y_hbm, of_ref, ob_ref,
                       ybuf, sems, *, alpha, tm, n_steps):
    i = pl.program_id(0)
    slot = i % 2

    def gather(idx_ref, to_slot):
        def body(r, carry):
            for k in range(TOP_K):
                _row_copy(y_hbm, idx_ref[0, r * TOP_K + k], ybuf.at[to_slot, k], r,
                          sems.at[to_slot]).start()
            return carry
        lax.fori_loop(0, tm, body, 0)

    @pl.when(i == 0)
    def _():
        gather(dest_ref, 0)

    @pl.when(i + 1 < n_steps)
    def _():
        gather(next_ref, 1 - slot)

    def drain(r, carry):
        for k in range(TOP_K):
            _row_copy(y_hbm, 0, ybuf.at[slot, k], 0, sems.at[slot]).wait()
        return carry
    lax.fori_loop(0, tm, drain, 0)

    gates = gate_ref[...]
    ffn = gates[:, 0:1] * ybuf[slot, 0]
    for k in range(1, TOP_K):
        ffn = ffn + gates[:, k:k + 1] * ybuf[slot, k]
    out = _layer_norm_rows(alpha * x_ref[...] + ffn, g_ref[...], b_ref[...])
    of_ref[...] = out
    ob_ref[...] = out.astype(BF16)


def _combine_ln(x, y, dest, gates, g, b, alpha):
    m, d = x.shape
    tm = _pick_tile(m, (64, 32, 16))
    n_steps = m // tm
    dest2 = dest.reshape(n_steps, 1, tm * TOP_K)
    row = lambda i: (i, 0)
    fix = lambda i: (0, 0)
    return pl.pallas_call(
        functools.partial(_combine_ln_kernel, alpha=alpha, tm=tm, n_steps=n_steps),
        grid=(n_steps,),
        in_specs=[pl.BlockSpec((None, 1, tm * TOP_K), lambda i: (i, 0, 0), memory_space=pltpu.SMEM),
                  pl.BlockSpec((None, 1, tm * TOP_K), lambda i: (jnp.minimum(i + 1, n_steps - 1), 0, 0),
                               memory_space=pltpu.SMEM),
                  pl.BlockSpec((tm, d), row), pl.BlockSpec((tm, LANE), row),
                  pl.BlockSpec((1, d), fix), pl.BlockSpec((1, d), fix),
                  pl.BlockSpec(memory_space=pl.ANY)],
        out_specs=[pl.BlockSpec((tm, d), row), pl.BlockSpec((tm, d), row)],
        out_shape=[jax.ShapeDtypeStruct((m, d), F32), jax.ShapeDtypeStruct((m, d), BF16)],
        scratch_shapes=[pltpu.VMEM((2, TOP_K, tm, d), F32), pltpu.SemaphoreType.DMA((2,))],
        compiler_params=_cparams(1),
        name="combine_ln",
    )(dest2, dest2, x, gates, g, b, y)


def _mem_attn_kernel(q_ref, k_ref, v_ref, o_ref, *, n_heads, hd):
    scale = hd ** -0.5
    for h in range(n_heads):
        cols = slice(h * hd, (h + 1) * hd)
        s = _dot_nt(q_ref[:, cols], k_ref[:, cols]) * scale
        p = jnp.exp(s - jnp.max(s, axis=-1, keepdims=True))
        p = p / jnp.sum(p, axis=-1, keepdims=True)
        o_ref[:, cols] = jnp.dot(p.astype(BF16), v_ref[:, cols],
                                 preferred_element_type=F32).astype(BF16)


def _mem_attn(qarr, q_rb0, q_cb, mk, mv, n_batch, t, tq, n_heads, hd):
    w = n_heads * hd
    n_mem = mk.shape[0] // n_batch
    nqb = t // tq
    return pl.pallas_call(
        functools.partial(_mem_attn_kernel, n_heads=n_heads, hd=hd),
        grid=(n_batch, nqb),
        in_specs=[pl.BlockSpec((tq, w), lambda b, i: (q_rb0 + b * nqb + i, q_cb)),
                  pl.BlockSpec((n_mem, w), lambda b, i: (b, 0)),
                  pl.BlockSpec((n_mem, w), lambda b, i: (b, 0))],
        out_specs=pl.BlockSpec((tq, w), lambda b, i: (b * nqb + i, 0)),
        out_shape=jax.ShapeDtypeStruct((n_batch * t, w), BF16),
        compiler_params=_cparams(2),
        name="mem_attn",
    )(qarr, mk, mv)


def _t5_bucket(rel, num_buckets):
    half = num_buckets // 2
    exact = half // 2
    n = jnp.abs(rel)
    far = exact + (jnp.log(jnp.maximum(n, 1).astype(F32) / exact)
                   / math.log(MAX_DISTANCE / exact) * (half - exact)).astype(I32)
    return jnp.where(rel > 0, half, 0) + jnp.where(n < exact, n, jnp.minimum(far, half - 1))


def _near_phase(q_off, tq, tk, nqb, i):
    full = tk // tq
    return (q_off // tq + i) % full if nqb >= full else i


def _near_bias(rel_bias, tq, tk, q_off, nqb):
    full = tk // tq
    n_maps = rel_bias.shape[1]
    rel_lo = -(2 * tk + tq)
    rels = jnp.arange(rel_lo, tk + 1, dtype=I32)
    by_rel = jnp.moveaxis(rel_bias[_t5_bucket(rels, rel_bias.shape[0])], -1, 0).astype(F32)
    n = tq + tk
    tiles = []
    for p in range(min(full, nqb)):
        q_in_block = (p if nqb >= full else (q_off // tq + p) % full) * tq
        per_near = []
        for near in range(2):
            rel0 = -near * tk - q_in_block
            start = rel0 - (tq - 1) - rel_lo
            v = by_rel[:, start:start + n]
            skew = jnp.tile(v, (1, tq + 1))[:, :tq * (n + 1)].reshape(n_maps, tq, n + 1)
            per_near.append(skew[:, ::-1, :tk])
        tiles.append(jnp.stack(per_near, axis=1))
    return jnp.stack(tiles)


def _far_bias(rel_bias):
    return rel_bias[rel_bias.shape[0] // 2 - 1].astype(F32)


def _dsa_kernel(far_ref, q_ref, qi_ref, qw_ref, k_ref, v_ref, ki_ref, bd_ref, o_ref,
                key_scr, wb_scr, m_scr, acc_scr, *,
                tq, tk, q_off, s_valid, k_sel, n_heads, n_kv, hd, n_idx, d_idx, idx_bits):
    gqa = n_heads // n_kv
    reps = tk // LANE
    i = pl.program_id(1)
    q0 = q_off + i * tq
    kb_diag = q0 // tk
    nkb = kb_diag + 1
    q_pos = q0 + lax.broadcasted_iota(I32, (tq, 1), 0)
    lane = lax.broadcasted_iota(I32, (1, tk), 1)
    lane1 = lax.broadcasted_iota(I32, (1, LANE), 1)
    tile = lambda x: jnp.concatenate([x] * reps, axis=1)

    qi = qi_ref[...]
    qw = qw_ref[...].astype(F32)
    qi_heads = [qi[:, h * d_idx:(h + 1) * d_idx] for h in range(n_idx)]
    for h in range(n_idx):
        wb_scr[h] = jnp.broadcast_to(qw[:, d_idx + h:d_idx + h + 1] * (d_idx ** -0.5), (tq, LANE))

    def score_block(kb, carry):
        ks = pl.multiple_of(kb * tk, tk)
        ki = ki_ref[pl.ds(ks, tk), :][:, :d_idx]
        sc = jnp.zeros((tq, tk), F32)
        for h in range(n_idx):
            sc = sc + jnp.maximum(_dot_nt(qi_heads[h], ki), 0.0) * tile(wb_scr[h])
        sc = sc * (n_idx ** -0.5)
        sc = jnp.where(sc == 0.0, 0.0, sc)
        k_pos = ks + lane
        allowed = ((k_pos >> CHUNK_SHIFT) <= (q_pos >> CHUNK_SHIFT)) & (k_pos < s_valid)
        sc = jnp.where(allowed, sc, -jnp.inf)
        bits = pltpu.bitcast(sc, I32)
        key_scr[kb] = bits ^ ((bits >> 31) & 0x7FFFFFFF)
        return carry

    lax.fori_loop(0, nkb, score_block, 0)

    def count(pred):
        def body(kb, acc):
            key = key_scr[kb]
            for c in range(reps):
                acc = acc + jnp.where(pred(kb, c, key[:, c * LANE:(c + 1) * LANE]), 1.0, 0.0)
            return acc
        acc = lax.fori_loop(0, nkb, body, jnp.zeros((tq, LANE), F32))
        return jnp.sum(acc, axis=1, keepdims=True)

    def bisect_cond(state):
        it, _, _, open_rows = state
        return (it < 32) & (open_rows > 0.0)

    def bisect_body(state):
        it, t_u, cnt_t, _ = state
        cand_u = t_u | (jnp.int32(1) << (31 - it))
        cand_b = jnp.broadcast_to(cand_u ^ INT_MIN, (tq, LANE))
        cnt = count(lambda kb, c, key: key >= cand_b)
        take = cnt >= k_sel
        t_u = jnp.where(take, cand_u, t_u)
        cnt_t = jnp.where(take, cnt, cnt_t)
        return it + 1, t_u, cnt_t, jnp.max(jnp.where(cnt_t > k_sel, 1.0, 0.0))

    total = (nkb * tk).astype(F32)
    _, t_u, cnt_t, _ = lax.while_loop(
        bisect_cond, bisect_body,
        (jnp.int32(0), jnp.zeros((tq, 1), I32), jnp.full((tq, 1), total, F32),
         jnp.where(total > k_sel, 1.0, 0.0)))
    thr = t_u ^ INT_MIN
    thr_b = jnp.broadcast_to(thr, (tq, LANE))

    def tie_search():
        need = k_sel - count(lambda kb, c, key: key > thr_b)

        def index_bit(it, j_hi):
            cand = j_hi | (jnp.int32(1) << (idx_bits - 1 - it))
            cand_b = jnp.broadcast_to(cand, (tq, LANE))
            cnt = count(lambda kb, c, key: (key == thr_b) & (kb * tk + c * LANE + lane1 < cand_b))
            return jnp.where(cnt < need, cand, j_hi)

        return lax.fori_loop(0, idx_bits, index_bit, jnp.zeros((tq, 1), I32))

    tie_hi = lax.cond(jnp.max(cnt_t) > k_sel, tie_search, lambda: jnp.full((tq, 1), 2 ** 30, I32))

    m_scr[...] = jnp.full(m_scr.shape, MASKED, F32)
    acc_scr[...] = jnp.zeros(acc_scr.shape, F32)
    scale2 = (hd ** -0.5) * LOG2E
    q_groups = [jnp.concatenate([q_ref[:, (g * gqa + r) * hd:(g * gqa + r + 1) * hd]
                                 for r in range(gqa)], axis=0) for g in range(n_kv)]
    far_rows = [jnp.concatenate([jnp.full((tq, LANE), far_ref[g * gqa + r], F32)
                                 for r in range(gqa)], axis=0) for g in range(n_kv)]
    ones = jnp.ones((tk, hd), BF16)

    def attend_block(kb, near):
        ks = pl.multiple_of(kb * tk, tk)
        key = key_scr[kb]
        sel = (key > thr) | ((key == thr) & (ks + lane <= tie_hi))
        sel = sel & (key > KEY_NEG_INF)
        mbias = jnp.where(sel, 0.0, MASKED)
        for g in range(n_kv):
            kg = k_ref[pl.ds(ks, tk), g * hd:(g + 1) * hd]
            vext = jnp.concatenate([v_ref[pl.ds(ks, tk), g * hd:(g + 1) * hd], ones], axis=1)
            s = _dot_nt(q_groups[g], kg) * scale2
            m_old = m_scr[g]
            if near is None:
                t = s + jnp.concatenate([mbias] * gqa, axis=0)
                m_new = jnp.maximum(m_old, jnp.max(t, axis=-1, keepdims=True) + far_rows[g])
                p = jnp.exp2(t - tile(m_new - far_rows[g]))
            else:
                t = s + jnp.concatenate([mbias + bd_ref[g * gqa + r, near] for r in range(gqa)], axis=0)
                m_new = jnp.maximum(m_old, jnp.max(t, axis=-1, keepdims=True))
                p = jnp.exp2(t - tile(m_new))
            alpha = jnp.exp2(m_old - m_new)
            acc_scr[g] = (jnp.concatenate([alpha, alpha], axis=1) * acc_scr[g]
                          + jnp.dot(p.astype(BF16), vext, preferred_element_type=F32))
            m_scr[g] = m_new

    def far_block(kb, carry):
        attend_block(kb, None)
        return carry

    lax.fori_loop(0, jnp.maximum(kb_diag - 1, 0), far_block, 0)

    @pl.when(kb_diag >= 1)
    def _():
        attend_block(kb_diag - 1, 1)

    attend_block(kb_diag, 0)

    for g in range(n_kv):
        acc = acc_scr[g]
        o = acc[:, :hd] / acc[:, hd:]
        for r in range(gqa):
            h = g * gqa + r
            o_ref[:, h * hd:(h + 1) * hd] = o[r * tq:(r + 1) * tq].astype(BF16)


def _dsa_attn(q_src, k_src, bias_near, bias_far, *, n_batch, t, s_pad, s_valid, q_off, tq, tk,
              k_sel, n_heads, n_kv, hd, n_idx, d_idx):
    qarr, q_rb0, q_cb, qi_cb, qw_cb = q_src
    karr, k_cb, varr, v_cb, kiarr, ki_cb = k_src
    assert tk >= MAX_DISTANCE and tk % tq == 0 and q_off % tq == 0 and s_pad % tk == 0
    nqb = t // tq
    nkb_max = s_pad // tk
    gqa = n_heads // n_kv
    wq, wkv, wqi = n_heads * hd, n_kv * hd, n_idx * d_idx
    qrow = lambda b, i: q_rb0 + b * nqb + i
    kernel = functools.partial(
        _dsa_kernel, tq=tq, tk=tk, q_off=q_off, s_valid=s_valid, k_sel=k_sel, n_heads=n_heads,
        n_kv=n_kv, hd=hd, n_idx=n_idx, d_idx=d_idx, idx_bits=max((s_pad - 1).bit_length(), 1))
    return pl.pallas_call(
        kernel,
        grid=(n_batch, nqb),
        in_specs=[pl.BlockSpec(memory_space=pltpu.SMEM),
                  pl.BlockSpec((tq, wq), lambda b, i: (qrow(b, i), q_cb)),
                  pl.BlockSpec((tq, wqi), lambda b, i: (qrow(b, i), qi_cb)),
                  pl.BlockSpec((tq, LANE), lambda b, i: (qrow(b, i), qw_cb)),
                  _resident((s_pad, wkv), lambda b, i: (b, k_cb)),
                  _resident((s_pad, wkv), lambda b, i: (b, v_cb)),
                  _resident((s_pad, LANE), lambda b, i: (b, ki_cb)),
                  pl.BlockSpec((None, n_heads, 2, tq, tk),
                               lambda b, i: (_near_phase(q_off, tq, tk, nqb, i), 0, 0, 0, 0))],
        out_specs=pl.BlockSpec((tq, wq), lambda b, i: (b * nqb + i, 0)),
        out_shape=jax.ShapeDtypeStruct((n_batch * t, wq), BF16),
        scratch_shapes=[pltpu.VMEM((nkb_max, tq, tk), I32),
                        pltpu.VMEM((n_idx, tq, LANE), F32),
                        pltpu.VMEM((n_kv, gqa * tq, LANE), F32),
                        pltpu.VMEM((n_kv, gqa * tq, 2 * hd), F32)],
        compiler_params=_cparams(2),
        name="dsa_attn",
    )(bias_far, qarr, qarr, qarr, karr, varr, kiarr, bias_near)


def _stick_kernel(q_ref, k_ref, v_ref, o_ref, carry_scr, acc_scr, *, tq, tk, q_off, hd):
    i = pl.program_id(2)
    q0 = q_off + i * tq
    q_pos = q0 + lax.broadcasted_iota(I32, (tq, 1), 0)
    lane = lax.broadcasted_iota(I32, (1, tk), 1)
    tri = (lax.broadcasted_iota(I32, (tk, tk), 0) > lax.broadcasted_iota(I32, (tk, tk), 1)).astype(BF16)
    q = q_ref[...]
    scale = hd ** -0.5
    carry_scr[...] = jnp.zeros(carry_scr.shape, F32)
    acc_scr[...] = jnp.zeros(acc_scr.shape, F32)

    def cond(state):
        kb, carry_max = state
        return (kb >= 0) & (carry_max > EXP_UNDERFLOW)

    def body(state):
        kb, _ = state
        ks = pl.multiple_of(kb * tk, tk)
        z = _dot_nt(q, k_ref[pl.ds(ks, tk), :]) * scale
        before = (ks + lane) < q_pos
        log_beta = jnp.minimum(z, 0.0) - jnp.log(1.0 + jnp.exp(-jnp.abs(z)))
        log_keep = jnp.where(before, log_beta - z, 0.0)
        hi = log_keep.astype(BF16)
        lo = (log_keep - hi.astype(F32)).astype(BF16)
        tail = (jnp.dot(hi, tri, preferred_element_type=F32)
                + jnp.dot(lo, tri, preferred_element_type=F32))
        carry = carry_scr[...]
        w = jnp.where(before, jnp.exp(log_beta + tail + carry), 0.0)
        acc_scr[...] += jnp.dot(w.astype(BF16), v_ref[pl.ds(ks, tk), :], preferred_element_type=F32)
        carry = carry + jnp.sum(log_keep, axis=-1, keepdims=True)
        carry_scr[...] = carry
        return kb - 1, jnp.max(carry)

    lax.while_loop(cond, body, ((q0 + tq - 1) // tk, jnp.float32(0.0)))
    o_ref[...] = acc_scr[...].astype(BF16)


def _stick_attn(q_src, k_src, *, n_batch, t, s_pad, q_off, tq, tk, n_heads, hd):
    qarr, q_rb0, q_cb0 = q_src
    karr, k_cb0, varr, v_cb0 = k_src
    nqb = t // tq
    return pl.pallas_call(
        functools.partial(_stick_kernel, tq=tq, tk=tk, q_off=q_off, hd=hd),
        grid=(n_batch, n_heads, nqb),
        in_specs=[pl.BlockSpec((tq, hd), lambda b, h, i: (q_rb0 + b * nqb + i, q_cb0 + h)),
                  pl.BlockSpec((s_pad, hd), lambda b, h, i: (b, k_cb0 + h)),
                  pl.BlockSpec((s_pad, hd), lambda b, h, i: (b, v_cb0 + h))],
        out_specs=pl.BlockSpec((tq, hd), lambda b, h, i: (b * nqb + i, h)),
        out_shape=jax.ShapeDtypeStruct((n_batch * t, n_heads * hd), BF16),
        scratch_shapes=[pltpu.VMEM((tq, 1), F32), pltpu.VMEM((tq, hd), F32)],
        compiler_params=_cparams(3),
        name="stick_attn",
    )(qarr, karr, varr)


def _diff_kernel(scal_ref, q_ref, k_ref, v_ref, bd_ref, g_ref, o_ref, m_scr, acc_scr, *,
                 tq, tk, q_off, s_valid, hd, out_scale):
    h = pl.program_id(1)
    i = pl.program_id(2)
    reps = tk // LANE
    q0 = q_off + i * tq
    kb_diag = q0 // tk
    q_pos = q0 + lax.broadcasted_iota(I32, (tq, 1), 0)
    lane = lax.broadcasted_iota(I32, (1, tk), 1)
    tile = lambda x: jnp.concatenate([x] * reps, axis=1)
    scale2 = (hd ** -0.5) * LOG2E
    m_scr[...] = jnp.full(m_scr.shape, MASKED, F32)
    acc_scr[...] = jnp.zeros(acc_scr.shape, F32)
    q_maps = [q_ref[:, m * hd:(m + 1) * hd] for m in range(2)]
    ones = jnp.ones((tk, hd), BF16)

    def attend_block(kb, near):
        ks = pl.multiple_of(kb * tk, tk)
        vext = jnp.concatenate([v_ref[pl.ds(ks, tk), :], ones], axis=1)
        if near is not None:
            k_pos = ks + lane
            allowed = ((k_pos >> CHUNK_SHIFT) <= (q_pos >> CHUNK_SHIFT)) & (k_pos < s_valid)
            mbias = jnp.where(allowed, 0.0, MASKED)
        for m in range(2):
            s = _dot_nt(q_maps[m], k_ref[pl.ds(ks, tk), m * hd:(m + 1) * hd]) * scale2
            m_old = m_scr[m]
            if near is None:
                far = scal_ref[1 + 2 * h + m]
                m_new = jnp.maximum(m_old, jnp.max(s, axis=-1, keepdims=True) + far)
                p = jnp.exp2(s - tile(m_new - far))
            else:
                t = s + (mbias + bd_ref[m, near])
                m_new = jnp.maximum(m_old, jnp.max(t, axis=-1, keepdims=True))
                p = jnp.exp2(t - tile(m_new))
            alpha = jnp.exp2(m_old - m_new)
            acc_scr[m] = (jnp.concatenate([alpha] * 3, axis=1) * acc_scr[m]
                          + jnp.dot(p.astype(BF16), vext, preferred_element_type=F32))
            m_scr[m] = m_new

    n_far = jnp.maximum(kb_diag - 1, 0)

    def far_pair(pair, carry):
        attend_block(2 * pair, None)
        attend_block(2 * pair + 1, None)
        return carry

    lax.fori_loop(0, n_far // 2, far_pair, 0)

    @pl.when(n_far % 2 == 1)
    def _():
        attend_block(n_far - 1, None)

    @pl.when(kb_diag >= 1)
    def _():
        attend_block(kb_diag - 1, 1)

    attend_block(kb_diag, 0)

    def normalised(m):
        acc = acc_scr[m]
        return acc[:, :2 * hd] / jnp.concatenate([acc[:, 2 * hd:]] * 2, axis=1)

    o = normalised(0) - scal_ref[0] * normalised(1)
    o = o * lax.rsqrt(jnp.mean(o * o, axis=-1, keepdims=True) + LN_EPS) * g_ref[...]
    o_ref[...] = (o * out_scale).astype(BF16)


def _diff_attn(q_src, k_src, bias_near, scalars, subln_g, *, n_batch, t, s_pad, s_valid, q_off,
               tq, tk, n_heads, hd, out_scale):
    qarr, q_rb0, q_cb0 = q_src
    karr, k_cb0, varr, v_cb0 = k_src
    assert tk >= MAX_DISTANCE and tk % tq == 0 and q_off % tq == 0 and s_pad % tk == 0
    nqb = t // tq
    w = 2 * hd
    return pl.pallas_call(
        functools.partial(_diff_kernel, tq=tq, tk=tk, q_off=q_off, s_valid=s_valid, hd=hd,
                          out_scale=out_scale),
        grid=(n_batch, n_heads, nqb),
        in_specs=[pl.BlockSpec(memory_space=pltpu.SMEM),
                  pl.BlockSpec((tq, w), lambda b, h, i: (q_rb0 + b * nqb + i, q_cb0 + h)),
                  pl.BlockSpec((s_pad, w), lambda b, h, i: (b, k_cb0 + h)),
                  pl.BlockSpec((s_pad, w), lambda b, h, i: (b, v_cb0 + h)),
                  pl.BlockSpec((None, None, 2, 2, tq, tk),
                               lambda b, h, i: (_near_phase(q_off, tq, tk, nqb, i), h, 0, 0, 0, 0)),
                  pl.BlockSpec((1, w), lambda b, h, i: (0, 0))],
        out_specs=pl.BlockSpec((tq, w), lambda b, h, i: (b * nqb + i, h)),
        out_shape=jax.ShapeDtypeStruct((n_batch * t, n_heads * w), BF16),
        scratch_shapes=[pltpu.VMEM((2, tq, LANE), F32), pltpu.VMEM((2, tq, w + hd), F32)],
        compiler_params=_cparams(3),
        name="diff_attn",
    )(scalars, qarr, karr, varr, bias_near, subln_g)


def _router_kernel(x_ref, wh_ref, wl_ref, b_ref, e_ref, g_ref, r_ref, c_ref, seen_scr):
    @pl.when(pl.program_id(0) == 0)
    def _():
        seen_scr[...] = jnp.zeros(seen_scr.shape, F32)

    x = x_ref[...]
    xh = x.astype(BF16)
    xl = (x - xh.astype(F32)).astype(BF16)
    wh = wh_ref[...]
    logits = (jnp.dot(xh, wh, preferred_element_type=F32)
              + jnp.dot(xl, wh, preferred_element_type=F32)
              + jnp.dot(xh, wl_ref[...], preferred_element_type=F32)) + b_ref[...]
    tm = logits.shape[0]
    lane = lax.broadcasted_iota(I32, logits.shape, 1)
    lane_f = lane.astype(F32)
    experts = jnp.zeros(logits.shape, I32)
    gates = jnp.zeros(logits.shape, F32)
    chosen = []
    top = None
    for k in range(TOP_K):
        m = jnp.max(logits, axis=-1, keepdims=True)
        ix = jnp.min(jnp.where(logits == m, lane_f, float(LANE)), axis=-1, keepdims=True).astype(I32)
        top = m if top is None else top
        experts = jnp.where(lane == k, ix, experts)
        gates = jnp.where(lane == k, jnp.exp(m - top), gates)
        chosen.append(lane == ix)
        logits = jnp.where(chosen[-1], -jnp.inf, logits)
    e_ref[...] = experts
    g_ref[...] = gates / jnp.sum(gates, axis=-1, keepdims=True)

    hot = functools.reduce(jnp.logical_or, chosen)
    hot_f = jnp.where(hot, 1.0, 0.0)
    lower = (lax.broadcasted_iota(I32, (tm, tm), 0) > lax.broadcasted_iota(I32, (tm, tm), 1)).astype(BF16)
    earlier = jnp.dot(lower, hot_f.astype(BF16), preferred_element_type=F32) + seen_scr[...]
    ranks = jnp.zeros(logits.shape, F32)
    for k in range(TOP_K):
        rank_k = jnp.sum(jnp.where(chosen[k], earlier, 0.0), axis=-1, keepdims=True)
        ranks = jnp.where(lane == k, rank_k, ranks)
    r_ref[...] = ranks.astype(I32)
    seen = seen_scr[...] + jnp.sum(hot_f, axis=0, keepdims=True)
    seen_scr[...] = seen
    c_ref[...] = seen


def _router(x, w_hi, w_lo, bias):
    m, d = x.shape
    tm = _pick_tile(m, (256, 128, 64, 32, 16))
    row = lambda i: (i, 0)
    fix = lambda i: (0, 0)
    return pl.pallas_call(
        _router_kernel,
        grid=(m // tm,),
        in_specs=[pl.BlockSpec((tm, d), row), pl.BlockSpec((d, LANE), fix),
                  pl.BlockSpec((d, LANE), fix), pl.BlockSpec((1, LANE), fix)],
        out_specs=[pl.BlockSpec((tm, LANE), row), pl.BlockSpec((tm, LANE), row),
                   pl.BlockSpec((tm, LANE), row), pl.BlockSpec((1, LANE), fix)],
        out_shape=[jax.ShapeDtypeStruct((m, LANE), I32), jax.ShapeDtypeStruct((m, LANE), F32),
                   jax.ShapeDtypeStruct((m, LANE), I32), jax.ShapeDtypeStruct((1, LANE), F32)],
        scratch_shapes=[pltpu.VMEM((1, LANE), F32)],
        compiler_params=_cparams(1),
        name="router",
    )(x, w_hi, w_lo, bias)


def _dispatch_kernel(dest_ref, x_ref, rows_in, rows_out, sem, *, tm):
    del rows_in

    def send(r, carry):
        for k in range(TOP_K):
            _row_copy(x_ref, r, rows_out, dest_ref[0, r * TOP_K + k], sem).start()
        return carry
    lax.fori_loop(0, tm, send, 0)

    def drain(r, carry):
        for k in range(TOP_K):
            _row_copy(x_ref, 0, rows_out, 0, sem).wait()
        return carry
    lax.fori_loop(0, tm, drain, 0)


def _dispatch(x_packed, dest, n_rows):
    m, w = x_packed.shape
    tm = _pick_tile(m, (256, 128, 64, 32, 16))
    n_steps = m // tm
    return pl.pallas_call(
        functools.partial(_dispatch_kernel, tm=tm),
        grid=(n_steps,),
        in_specs=[pl.BlockSpec((None, 1, tm * TOP_K), lambda i: (i, 0, 0), memory_space=pltpu.SMEM),
                  pl.BlockSpec((tm, w), lambda i: (i, 0)),
                  pl.BlockSpec(memory_space=pl.ANY)],
        out_specs=pl.BlockSpec(memory_space=pl.ANY),
        out_shape=jax.ShapeDtypeStruct((n_rows, w), I32),
        scratch_shapes=[pltpu.SemaphoreType.DMA(())],
        input_output_aliases={2: 0},
        compiler_params=_cparams(1),
        name="dispatch",
    )(dest.reshape(n_steps, 1, tm * TOP_K), x_packed, jnp.zeros((n_rows, w), I32))


def _cast_rows(src_ref, dst_ref, rows_per_step):
    def body(c, carry):
        r = pl.multiple_of(c * rows_per_step, rows_per_step)
        dst_ref[pl.ds(r, rows_per_step), :] = src_ref[pl.ds(r, rows_per_step), :].astype(BF16)
        return carry
    lax.fori_loop(0, src_ref.shape[0] // rows_per_step, body, 0)


def _expert_changed(be_ref, j):
    return (j == 0) | (be_ref[j] != be_ref[jnp.maximum(j - 1, 0)])


def _moe_up_kernel(be_ref, nu_ref, x_ref, wg_ref, wl_ref, bg_ref, bl_ref, h_ref, wg_bf, wl_bf, *,
                   cast_rows):
    j = pl.program_id(1)

    @pl.when(_expert_changed(be_ref, j))
    def _():
        _cast_rows(wg_ref, wg_bf, cast_rows)
        _cast_rows(wl_ref, wl_bf, cast_rows)

    @pl.when(j < nu_ref[0])
    def _():
        x = _unpack_halves(x_ref[...])
        glu = jnp.dot(x, wg_bf[...], preferred_element_type=F32) + bg_ref[...]
        lin = jnp.dot(x, wl_bf[...], preferred_element_type=F32) + bl_ref[...]
        glu = jnp.minimum(glu, SWIGLU_LIMIT)
        lin = jnp.clip(lin, -SWIGLU_LIMIT, SWIGLU_LIMIT)
        h_ref[...] = (glu * jax.nn.sigmoid(SWIGLU_ALPHA * glu) * (lin + 1.0)).astype(BF16)

    @pl.when(j >= nu_ref[0])
    def _():
        h_ref[...] = jnp.zeros(h_ref.shape, BF16)


def _moe_down_kernel(be_ref, nu_ref, h_ref, wd_ref, bd_ref, y_ref, wd_bf, *, cast_rows):
    j = pl.program_id(1)

    @pl.when(_expert_changed(be_ref, j))
    def _():
        _cast_rows(wd_ref, wd_bf, cast_rows)

    @pl.when(j < nu_ref[0])
    def _():
        y_ref[...] = jnp.dot(h_ref[...], wd_bf[...], preferred_element_type=F32) + bd_ref[...]

    @pl.when(j >= nu_ref[0])
    def _():
        y_ref[...] = jnp.zeros(y_ref.shape, F32)


def _moe_experts(xs, block_e, n_used, layer, w_gate_up, b_gate_up, w_down, b_down, tm):
    n_rows = xs.shape[0]
    d = 2 * xs.shape[1]
    n_blocks = n_rows // tm
    d_ff = w_down.shape[2]
    tf = _pick_tile(d_ff, (512, 256, 128))
    nf = d_ff // tf
    tn = _pick_tile(d, (1024, 512, 256, 128))
    blk = lambda j, nu: jnp.minimum(j, nu[0] - 1)
    hidden = pl.pallas_call(
        functools.partial(_moe_up_kernel, cast_rows=_pick_tile(d, (256, 128, 64, 32, 16, 8))),
        grid_spec=pltpu.PrefetchScalarGridSpec(
            num_scalar_prefetch=2,
            grid=(nf, n_blocks),
            in_specs=[pl.BlockSpec((tm, d // 2), lambda f, j, be, nu: (blk(j, nu), 0)),
                      pl.BlockSpec((None, None, d, tf), lambda f, j, be, nu: (layer, be[j], 0, f)),
                      pl.BlockSpec((None, None, d, tf), lambda f, j, be, nu: (layer, be[j], 0, nf + f)),
                      pl.BlockSpec((None, None, 1, tf), lambda f, j, be, nu: (layer, be[j], 0, f)),
                      pl.BlockSpec((None, None, 1, tf), lambda f, j, be, nu: (layer, be[j], 0, nf + f))],
            out_specs=pl.BlockSpec((tm, tf), lambda f, j, be, nu: (j, f)),
            scratch_shapes=[pltpu.VMEM((d, tf), BF16), pltpu.VMEM((d, tf), BF16)]),
        out_shape=jax.ShapeDtypeStruct((n_rows, d_ff), BF16),
        compiler_params=_cparams(2),
        name="moe_up",
    )(block_e, n_used, xs, w_gate_up, w_gate_up, b_gate_up, b_gate_up)
    return pl.pallas_call(
        functools.partial(_moe_down_kernel, cast_rows=_pick_tile(d_ff, (256, 128, 64, 32, 16, 8))),
        grid_spec=pltpu.PrefetchScalarGridSpec(
            num_scalar_prefetch=2,
            grid=(d // tn, n_blocks),
            in_specs=[pl.BlockSpec((tm, d_ff), lambda n, j, be, nu: (blk(j, nu), 0)),
                      pl.BlockSpec((None, None, d_ff, tn), lambda n, j, be, nu: (layer, be[j], 0, n)),
                      pl.BlockSpec((None, None, 1, tn), lambda n, j, be, nu: (layer, be[j], 0, n))],
            out_specs=pl.BlockSpec((tm, tn), lambda n, j, be, nu: (j, n)),
            scratch_shapes=[pltpu.VMEM((d_ff, tn), BF16)]),
        out_shape=jax.ShapeDtypeStruct((n_rows, d), F32),
        compiler_params=_cparams(2),
        name="moe_down",
    )(block_e, n_used, hidden, w_down, b_down)


def _moe(x_f32, x_packed, layer, w_router_hi, w_router_lo, b_router, w_gate_up, b_gate_up,
         w_down, b_down):
    n = x_f32.shape[0]
    n_experts = w_gate_up.shape[1]
    tm = 512 if n * TOP_K >= 1024 * n_experts else 128
    experts, gates, ranks, counts = _router(x_f32, w_router_hi, w_router_lo, b_router)
    counts = counts[0, :n_experts].astype(I32)
    pend = jnp.cumsum((counts + tm - 1) // tm * tm)
    pstart = pend - (counts + tm - 1) // tm * tm
    n_blocks = -(-(n * TOP_K + n_experts * (tm - 1)) // tm)
    block_e = jnp.minimum(jnp.sum(pend[None, :] <= (jnp.arange(n_blocks) * tm)[:, None], axis=1),
                          n_experts - 1).astype(I32)
    n_used = (pend[-1] // tm).astype(I32).reshape(1)
    e4 = experts[:, :TOP_K]
    onehot = e4[..., None] == jnp.arange(n_experts, dtype=I32)
    dest = (ranks[:, :TOP_K] + jnp.sum(jnp.where(onehot, pstart.astype(I32), 0), axis=-1)).reshape(-1)
    xs = _dispatch(x_packed, dest, n_blocks * tm)
    y = _moe_experts(xs, block_e, n_used, layer, w_gate_up, b_gate_up, w_down, b_down, tm)
    return y, dest, gates


def _pad_cols(w, n):
    return jnp.pad(w, ((0, 0), (0, n - w.shape[1])))


def _keys_with_past(past, new, s_pad):
    b, p, w = past.shape
    pad = jnp.zeros((b, s_pad - p - new.shape[1], w), BF16)
    return jnp.concatenate([past.astype(BF16), new.astype(BF16), pad], axis=1).reshape(b * s_pad, w)


def kernel(x_prompt, x_sample, cache_a_k, cache_a_v, cache_a_kidx, cache_b_k, cache_b_v,
           cache_c_k, cache_c_v, cache_mem_k, cache_mem_v, mem_prompt,
           w_in_a, w_in_b, w_in_c, diff_lambda, diff_subln, rel_bias, w_mem_kv, w_o,
           ln_g, ln_b, w_router, b_router, w_gate_up, b_gate_up, w_down, b_down):
    bp, tp, d = x_prompt.shape
    bs, ts, _ = x_sample.shape
    depth = w_o.shape[0]
    past_len = cache_a_k.shape[2]
    n_kv, hd = cache_a_k.shape[3], cache_a_k.shape[4]
    d_idx = cache_a_kidx.shape[3]
    n_mem, h_mem = cache_mem_k.shape[2], cache_mem_k.shape[3]
    mem_w = h_mem * hd
    tok_w = d - mem_w
    h_a = tok_w // hd
    h_b = cache_b_k.shape[3]
    hc = cache_c_k.shape[3]
    n_idx = (w_in_a.shape[2] - tok_w - 2 * n_kv * hd - d_idx - mem_w) // (d_idx + 1)
    n_experts = w_router.shape[2]
    alpha = (2 * depth) ** 0.25
    mp, ms = bp * tp, bs * ts
    s_new = past_len + ts
    kv_w, qi_w = n_kv * hd, n_idx * d_idx

    o_k, o_v, o_qi = tok_w, tok_w + kv_w, tok_w + 2 * kv_w
    o_ki = o_qi + qi_w
    o_wi = o_ki + d_idx
    o_mq = o_wi + n_idx
    a_mq = o_qi + qi_w
    a_kw = a_mq + mem_w
    cols_a = _round_up(a_kw + LANE, 5 * LANE if d >= 5 * LANE else LANE)
    w_a = [_pad_cols(jnp.concatenate([w[:, :o_ki], w[:, o_mq:], w[:, o_ki:o_mq]], axis=1),
                     cols_a).astype(BF16) for w in w_in_a]
    w_ki = [w[:, o_ki:o_wi].astype(BF16) for w in w_in_a]
    w_b = [w.astype(BF16) for w in w_in_b]
    w_c = [w.astype(BF16) for w in w_in_c]
    w_o_bf = w_o.astype(BF16)
    w_r_pad = jnp.pad(w_router, ((0, 0), (0, 0), (0, LANE - n_experts)))
    w_r_hi = w_r_pad.astype(BF16)
    w_r_lo = (w_r_pad - w_r_hi.astype(F32)).astype(BF16)
    b_r_pad = jnp.pad(b_router, ((0, 0), (0, LANE - n_experts)), constant_values=MASKED)[:, None, :]
    b_gu = b_gate_up[:, :, None, :]
    b_dn = b_down[:, :, None, :]

    tq_a = _pick_tile(tp, (128, 64))
    tk_a = _pick_tile(tp, (512, 256, 128))
    tq_b = _pick_tile(tp, (256, 128, 64))
    tk_b = min(tq_b, 256)
    tq_c = _pick_tile(tp, (256, 128))
    tk_c = _pick_tile(tp, (512, 256, 128))
    tq_m = _pick_tile(tp, (512, 256, 128))
    sp_a, sp_b, sp_c = _round_up(s_new, tk_a), _round_up(s_new, tk_b), _round_up(s_new, tk_c)

    far2 = _far_bias(rel_bias) * LOG2E
    near_a_p = _near_bias(rel_bias, tq_a, tk_a, 0, tp // tq_a) * LOG2E
    near_a_s = _near_bias(rel_bias, ts, tk_a, past_len, 1) * LOG2E
    near_c_p = _near_bias(rel_bias, tq_c, tk_c, 0, tp // tq_c) * LOG2E
    near_c_s = _near_bias(rel_bias, ts, tk_c, past_len, 1) * LOG2E
    pair = lambda nb: nb.reshape(nb.shape[0], hc, 2, *nb.shape[2:])

    w_mem = jnp.moveaxis(w_mem_kv, 0, 1).reshape(d, depth * 2 * mem_w).astype(BF16)
    mem_f, mem_b = _matmul(mem_prompt.reshape(bp * n_mem, d).astype(BF16), w_mem)
    mem_f = mem_f.reshape(bp, n_mem, depth, 2, h_mem, hd)
    mem_k_p = jnp.moveaxis(mem_f[:, :, :, 0], 2, 0)
    mem_v_p = jnp.moveaxis(mem_f[:, :, :, 1], 2, 0)
    mem_s_k = cache_mem_k.reshape(depth, bs * n_mem, mem_w).astype(BF16)
    mem_s_v = cache_mem_v.reshape(depth, bs * n_mem, mem_w).astype(BF16)

    x_f = jnp.concatenate([x_prompt.reshape(mp, d), x_sample.reshape(ms, d)], axis=0)
    x_b = x_f.astype(BF16)
    rows_p = [None] * depth
    rows_s = [None] * depth

    for i in range(depth):
        kind, j = i % N_MIXERS, i // N_MIXERS
        w_in = (w_a, w_b, w_c)[kind][j]
        h_f, h_b_ = _matmul(x_b, w_in)
        hs_f = h_f[mp:].reshape(bs, ts, -1)
        hp = lambda lo, hi: h_f[:mp, lo:hi]
        rb_s = mp // ts

        if kind == 0:
            k_sel_p = min(K_SEL_MAX, tp // 4)
            k_sel_s = min(K_SEL_MAX, s_new // 4)
            dims = dict(n_heads=h_a, n_kv=n_kv, hd=hd, n_idx=n_idx, d_idx=d_idx)
            mix_p = _dsa_attn(
                (h_b_, 0, 0, o_qi // qi_w, a_kw // LANE),
                (h_b_, o_k // kv_w, h_b_, o_v // kv_w, h_b_, a_kw // LANE),
                near_a_p, far2, n_batch=bp, t=tp, s_pad=tp, s_valid=tp, q_off=0,
                tq=tq_a, tk=tk_a, k_sel=k_sel_p, **dims)
            kiw_new = jnp.pad(hs_f[..., a_kw:a_kw + d_idx], ((0, 0), (0, 0), (0, LANE - d_idx)))
            kiw_past = jnp.pad(cache_a_kidx[j], ((0, 0), (0, 0), (0, LANE - d_idx)))
            k_all = _keys_with_past(cache_a_k[j].reshape(bs, past_len, kv_w), hs_f[..., o_k:o_v], sp_a)
            v_all = _keys_with_past(cache_a_v[j].reshape(bs, past_len, kv_w), hs_f[..., o_v:o_qi], sp_a)
            ki_all = _keys_with_past(kiw_past, kiw_new, sp_a)
            mix_s = _dsa_attn(
                (h_b_, rb_s, 0, o_qi // qi_w, a_kw // LANE),
                (k_all, 0, v_all, 0, ki_all, 0),
                near_a_s, far2, n_batch=bs, t=ts, s_pad=sp_a, s_valid=s_new, q_off=past_len,
                tq=ts, tk=tk_a, k_sel=k_sel_s, **dims)
            mq_cb = a_mq // mem_w
            ki_f, _ = _matmul(x_b, w_ki[j])
            rows_p[i] = (hp(o_k, o_v).reshape(bp, tp, n_kv, hd), hp(o_v, o_qi).reshape(bp, tp, n_kv, hd),
                         ki_f[:mp].reshape(bp, tp, d_idx))
            rows_s[i] = (hs_f[..., o_k:o_v].reshape(bs, ts, n_kv, hd),
                         hs_f[..., o_v:o_qi].reshape(bs, ts, n_kv, hd), ki_f[mp:].reshape(bs, ts, d_idx))
        elif kind == 1:
            dims = dict(n_heads=h_b, hd=hd)
            mix_p = _stick_attn((h_b_, 0, 0), (h_b_, h_b, h_b_, 2 * h_b),
                                n_batch=bp, t=tp, s_pad=tp, q_off=0, tq=tq_b, tk=tk_b, **dims)
            k_all = _keys_with_past(cache_b_k[j].reshape(bs, past_len, tok_w), hs_f[..., tok_w:2 * tok_w], sp_b)
            v_all = _keys_with_past(cache_b_v[j].reshape(bs, past_len, tok_w), hs_f[..., 2 * tok_w:3 * tok_w], sp_b)
            mix_s = _stick_attn((h_b_, rb_s, 0), (k_all, 0, v_all, 0),
                                n_batch=bs, t=ts, s_pad=sp_b, q_off=past_len, tq=ts, tk=tk_b, **dims)
            mq_cb = 3 * tok_w // mem_w
            rows_p[i] = (hp(tok_w, 2 * tok_w).reshape(bp, tp, h_b, hd),
                         hp(2 * tok_w, 3 * tok_w).reshape(bp, tp, h_b, hd))
            rows_s[i] = (hs_f[..., tok_w:2 * tok_w].reshape(bs, ts, h_b, hd),
                         hs_f[..., 2 * tok_w:3 * tok_w].reshape(bs, ts, h_b, hd))
        else:
            lambda_init = 0.8 - 0.6 * math.exp(-0.3 * i)
            lv = diff_lambda[j].astype(F32)
            lam = jnp.exp(jnp.sum(lv[0] * lv[1])) - jnp.exp(jnp.sum(lv[2] * lv[3])) + lambda_init
            scalars = jnp.concatenate([lam.reshape(1), far2]).astype(F32)
            g = diff_subln[j].reshape(1, 2 * hd).astype(F32)
            dims = dict(n_heads=hc, hd=hd, out_scale=1.0 - lambda_init)
            mix_p = _diff_attn((h_b_, 0, 0), (h_b_, hc, h_b_, 2 * hc), pair(near_c_p), scalars, g,
                               n_batch=bp, t=tp, s_pad=tp, s_valid=tp, q_off=0, tq=tq_c, tk=tk_c, **dims)
            k_all = _keys_with_past(cache_c_k[j].reshape(bs, past_len, tok_w), hs_f[..., tok_w:2 * tok_w], sp_c)
            v_all = _keys_with_past(cache_c_v[j].reshape(bs, past_len, tok_w), hs_f[..., 2 * tok_w:3 * tok_w], sp_c)
            mix_s = _diff_attn((h_b_, rb_s, 0), (k_all, 0, v_all, 0), pair(near_c_s), scalars, g,
                               n_batch=bs, t=ts, s_pad=sp_c, s_valid=s_new, q_off=past_len,
                               tq=ts, tk=tk_c, **dims)
            mq_cb = 3 * tok_w // mem_w
            rows_p[i] = (hp(tok_w, 2 * tok_w).reshape(bp, tp, hc, 2, hd),
                         hp(2 * tok_w, 3 * tok_w).reshape(bp, tp, hc, 2 * hd))
            rows_s[i] = (hs_f[..., tok_w:2 * tok_w].reshape(bs, ts, hc, 2, hd),
                         hs_f[..., 2 * tok_w:3 * tok_w].reshape(bs, ts, hc, 2 * hd))

        mem_p = _mem_attn(h_b_, 0, mq_cb, mem_b[:, (2 * i) * mem_w:(2 * i + 1) * mem_w],
                          mem_b[:, (2 * i + 1) * mem_w:(2 * i + 2) * mem_w],
                          bp, tp, tq_m, h_mem, hd)
        mem_s = _mem_attn(h_b_, rb_s, mq_cb, mem_s_k[i], mem_s_v[i], bs, ts, ts, h_mem, hd)
        mix = jnp.concatenate([mix_p, mix_s], axis=0)
        mem_o = jnp.concatenate([mem_p, mem_s], axis=0)
        x_f, x_pk = _proj_ln(mix, mem_o, w_o_bf[i, :tok_w], w_o_bf[i, tok_w:], x_f,
                             ln_g[i, 0][None], ln_b[i, 0][None], alpha)
        y, dest, gates = _moe(x_f, x_pk, i, w_r_hi[i], w_r_lo[i], b_r_pad[i],
                              w_gate_up, b_gu, w_down, b_dn)
        x_f, x_b = _combine_ln(x_f, y, dest, gates, ln_g[i, 1][None], ln_b[i, 1][None], alpha)

    def stack_kind(rows, kind):
        sel = [r for i, r in enumerate(rows) if i % N_MIXERS == kind]
        return [jnp.stack(parts) for parts in zip(*sel)]

    y_prompt = x_f[:mp].reshape(bp, tp, d)
    y_sample = x_f[mp:].reshape(bs, ts, d)
    a_k_p, a_v_p, a_kidx_p = stack_kind(rows_p, 0)
    b_k_p, b_v_p = stack_kind(rows_p, 1)
    c_k_p, c_v_p = stack_kind(rows_p, 2)
    a_k_s, a_v_s, a_kidx_s = stack_kind(rows_s, 0)
    b_k_s, b_v_s = stack_kind(rows_s, 1)
    c_k_s, c_v_s = stack_kind(rows_s, 2)
    return (y_prompt, y_sample, a_k_p, a_v_p, a_kidx_p, b_k_p, b_v_p, c_k_p, c_v_p, mem_k_p, mem_v_p,
            a_k_s, a_v_s, a_kidx_s, b_k_s, b_v_s, c_k_s, c_v_s)
```

```python
import functools
import math

import jax
import jax.numpy as jnp
from jax import lax
from jax.experimental import pallas as pl
from jax.experimental.pallas import tpu as pltpu

F32 = jnp.float32
BF16 = jnp.bfloat16
I32 = jnp.int32

CHUNK = 64
CHUNK_SHIFT = CHUNK.bit_length() - 1
K_SEL_MAX = 256
MAX_DISTANCE = 128
TOP_K = 4
SWIGLU_LIMIT = 7.0
SWIGLU_ALPHA = 1.702
LN_EPS = 1e-5
N_MIXERS = 3

LANE = 128
VMEM_LIMIT = 56 * 1024 * 1024
MASKED = -1e30
EXP_UNDERFLOW = -104.0
LOG2E = 1.4426950408889634
INT_MIN = -2 ** 31
KEY_NEG_INF = -2139095041
HI16 = -65536

def _cparams(n_axes):
    return pltpu.CompilerParams(dimension_semantics=("arbitrary",) * n_axes,
                                vmem_limit_bytes=VMEM_LIMIT)


def _round_up(x, m):
    return -(-x // m) * m


def _pick_tile(n, candidates):
    for c in candidates:
        if n % c == 0:
            return c
    return n


def _dot_nt(a, b):
    return lax.dot_general(a, b, (((1,), (1,)), ((), ())), preferred_element_type=F32)


def _resident(block_shape, index_map):
    return pl.BlockSpec(block_shape, index_map, pipeline_mode=pl.Buffered(1))


def _mm_kernel(x_ref, w_ref, of_ref, ob_ref):
    acc = jnp.dot(x_ref[...], w_ref[...], preferred_element_type=F32)
    of_ref[...] = acc
    ob_ref[...] = acc.astype(BF16)


def _matmul(x, w):
    m, k = x.shape
    n = w.shape[1]
    tm = _pick_tile(m, (640, 512, 256, 128, 64, 32, 16))
    tn = _pick_tile(n, (1280, 1024, 768, 512, 384, 256, 128))
    return pl.pallas_call(
        _mm_kernel,
        grid=(n // tn, m // tm),
        in_specs=[pl.BlockSpec((tm, k), lambda j, i: (i, 0)),
                  pl.BlockSpec((k, tn), lambda j, i: (0, j))],
        out_specs=[pl.BlockSpec((tm, tn), lambda j, i: (i, j)),
                   pl.BlockSpec((tm, tn), lambda j, i: (i, j))],
        out_shape=[jax.ShapeDtypeStruct((m, n), F32), jax.ShapeDtypeStruct((m, n), BF16)],
        compiler_params=_cparams(2),
        name="matmul",
    )(x, w)


def _layer_norm_rows(z, g, b):
    mu = jnp.mean(z, axis=-1, keepdims=True)
    zc = z - mu
    var = jnp.mean(zc * zc, axis=-1, keepdims=True)
    return zc * lax.rsqrt(var + LN_EPS) * g + b


def _pack_halves(x):
    half = x.shape[1] // 2
    xb = x.astype(BF16).astype(F32)
    lo = lax.shift_right_logical(pltpu.bitcast(xb[:, :half], I32), 16)
    return lo | (pltpu.bitcast(xb[:, half:], I32) & HI16)


def _unpack_halves(p):
    lo = pltpu.bitcast(p << 16, F32).astype(BF16)
    hi = pltpu.bitcast(p & HI16, F32).astype(BF16)
    return jnp.concatenate([lo, hi], axis=1)


def _proj_ln_kernel(a1_ref, a2_ref, w1_ref, w2_ref, r_ref, g_ref, b_ref, of_ref, op_ref, *, alpha):
    y = jnp.dot(a1_ref[...], w1_ref[...], preferred_element_type=F32)
    y = y + jnp.dot(a2_ref[...], w2_ref[...], preferred_element_type=F32)
    out = _layer_norm_rows(alpha * r_ref[...] + y, g_ref[...], b_ref[...])
    of_ref[...] = out
    op_ref[...] = _pack_halves(out)


def _proj_ln(a1, a2, w1, w2, res, g, b, alpha):
    m, k1 = a1.shape
    k2 = a2.shape[1]
    d = w1.shape[1]
    tm = _pick_tile(m, (256, 128, 64, 32, 16))
    row = lambda i: (i, 0)
    fix = lambda i: (0, 0)
    return pl.pallas_call(
        functools.partial(_proj_ln_kernel, alpha=alpha),
        grid=(m // tm,),
        in_specs=[pl.BlockSpec((tm, k1), row), pl.BlockSpec((tm, k2), row),
                  pl.BlockSpec((k1, d), fix), pl.BlockSpec((k2, d), fix),
                  pl.BlockSpec((tm, d), row), pl.BlockSpec((1, d), fix), pl.BlockSpec((1, d), fix)],
        out_specs=[pl.BlockSpec((tm, d), row), pl.BlockSpec((tm, d // 2), row)],
        out_shape=[jax.ShapeDtypeStruct((m, d), F32), jax.ShapeDtypeStruct((m, d // 2), I32)],
        compiler_params=_cparams(1),
        name="proj_ln",
    )(a1, a2, w1, w2, res, g, b)


def _row_copy(src_ref, src_row, dst_ref, dst_row, sem):
    return pltpu.make_async_copy(src_ref.at[pl.ds(src_row, 1)], dst_ref.at[pl.ds(dst_row, 1)], sem)


def _combine_ln_kernel(dest_ref, next_ref, x_ref, gate_ref, g_ref, b_ref, y_hbm, of_ref, ob_ref,
                       ybuf, sems, *, alpha, tm, n_steps):
    i = pl.program_id(0)
    slot = i % 2

    def gather(idx_ref, to_slot):
        def body(r, carry):
            for k in range(TOP_K):
                _row_copy(y_hbm, idx_ref[0, r * TOP_K + k], ybuf.at[to_slot, k], r,
                          sems.at[to_slot]).start()
            return carry
        lax.fori_loop(0, tm, body, 0)

    @pl.when(i == 0)
    def _():
        gather(dest_ref, 0)

    @pl.when(i + 1 < n_steps)
    def _():
        gather(next_ref, 1 - slot)

    def drain(r, carry):
        for k in range(TOP_K):
            _row_copy(y_hbm, 0, ybuf.at[slot, k], 0, sems.at[slot]).wait()
        return carry
    lax.fori_loop(0, tm, drain, 0)

    gates = gate_ref[...]
    ffn = gates[:, 0:1] * ybuf[slot, 0]
    for k in range(1, TOP_K):
        ffn = ffn + gates[:, k:k + 1] * ybuf[slot, k]
    out = _layer_norm_rows(alpha * x_ref[...] + ffn, g_ref[...], b_ref[...])
    of_ref[...] = out
    ob_ref[...] = out.astype(BF16)


def _combine_ln(x, y, dest, gates, g, b, alpha):
    m, d = x.shape
    tm = _pick_tile(m, (64, 32, 16))
    n_steps = m // tm
    dest2 = dest.reshape(n_steps, 1, tm * TOP_K)
    row = lambda i: (i, 0)
    fix = lambda i: (0, 0)
    return pl.pallas_call(
        functools.partial(_combine_ln_kernel, alpha=alpha, tm=tm, n_steps=n_steps),
        grid=(n_steps,),
        in_specs=[pl.BlockSpec((None, 1, tm * TOP_K), lambda i: (i, 0, 0), memory_space=pltpu.SMEM),
                  pl.BlockSpec((None, 1, tm * TOP_K), lambda i: (jnp.minimum(i + 1, n_steps - 1), 0, 0),
                               memory_space=pltpu.SMEM),
                  pl.BlockSpec((tm, d), row), pl.BlockSpec((tm, LANE), row),
                  pl.BlockSpec((1, d), fix), pl.BlockSpec((1, d), fix),
                  pl.BlockSpec(memory_space=pl.ANY)],
        out_specs=[pl.BlockSpec((tm, d), row), pl.BlockSpec((tm, d), row)],
        out_shape=[jax.ShapeDtypeStruct((m, d), F32), jax.ShapeDtypeStruct((m, d), BF16)],
        scratch_shapes=[pltpu.VMEM((2, TOP_K, tm, d), F32), pltpu.SemaphoreType.DMA((2,))],
        compiler_params=_cparams(1),
        name="combine_ln",
    )(dest2, dest2, x, gates, g, b, y)


def _mem_attn_kernel(q_ref, k_ref, v_ref, o_ref, *, n_heads, hd):
    scale = hd ** -0.5
    for h in range(n_heads):
        cols = slice(h * hd, (h + 1) * hd)
        s = _dot_nt(q_ref[:, cols], k_ref[:, cols]) * scale
        p = jnp.exp(s - jnp.max(s, axis=-1, keepdims=True))
        p = p / jnp.sum(p, axis=-1, keepdims=True)
        o_ref[:, cols] = jnp.dot(p.astype(BF16), v_ref[:, cols],
                                 preferred_element_type=F32).astype(BF16)


def _mem_attn(qarr, q_rb0, q_cb, mk, mv, n_batch, t, tq, n_heads, hd):
    w = n_heads * hd
    n_mem = mk.shape[0] // n_batch
    nqb = t // tq
    return pl.pallas_call(
        functools.partial(_mem_attn_kernel, n_heads=n_heads, hd=hd),
        grid=(n_batch, nqb),
        in_specs=[pl.BlockSpec((tq, w), lambda b, i: (q_rb0 + b * nqb + i, q_cb)),
                  pl.BlockSpec((n_mem, w), lambda b, i: (b, 0)),
                  pl.BlockSpec((n_mem, w), lambda b, i: (b, 0))],
        out_specs=pl.BlockSpec((tq, w), lambda b, i: (b * nqb + i, 0)),
        out_shape=jax.ShapeDtypeStruct((n_batch * t, w), BF16),
        compiler_params=_cparams(2),
        name="mem_attn",
    )(qarr, mk, mv)


def _t5_bucket(rel, num_buckets):
    half = num_buckets // 2
    exact = half // 2
    n = jnp.abs(rel)
    far = exact + (jnp.log(jnp.maximum(n, 1).astype(F32) / exact)
                   / math.log(MAX_DISTANCE / exact) * (half - exact)).astype(I32)
    return jnp.where(rel > 0, half, 0) + jnp.where(n < exact, n, jnp.minimum(far, half - 1))


def _near_phase(q_off, tq, tk, nqb, i):
    full = tk // tq
    return (q_off // tq + i) % full if nqb >= full else i


def _near_bias(rel_bias, tq, tk, q_off, nqb):
    full = tk // tq
    n_maps = rel_bias.shape[1]
    rel_lo = -(2 * tk + tq)
    rels = jnp.arange(rel_lo, tk + 1, dtype=I32)
    by_rel = jnp.moveaxis(rel_bias[_t5_bucket(rels, rel_bias.shape[0])], -1, 0).astype(F32)
    n = tq + tk
    tiles = []
    for p in range(min(full, nqb)):
        q_in_block = (p if nqb >= full else (q_off // tq + p) % full) * tq
        per_near = []
        for near in range(2):
            rel0 = -near * tk - q_in_block
            start = rel0 - (tq - 1) - rel_lo
            v = by_rel[:, start:start + n]
            skew = jnp.tile(v, (1, tq + 1))[:, :tq * (n + 1)].reshape(n_maps, tq, n + 1)
            per_near.append(skew[:, ::-1, :tk])
        tiles.append(jnp.stack(per_near, axis=1))
    return jnp.stack(tiles)


def _far_bias(rel_bias):
    return rel_bias[rel_bias.shape[0] // 2 - 1].astype(F32)


def _dsa_kernel(far_ref, q_ref, qi_ref, qw_ref, k_ref, v_ref, ki_ref, bd_ref, o_ref,
                key_scr, wb_scr, m_scr, acc_scr, *,
                tq, tk, q_off, s_valid, k_sel, n_heads, n_kv, hd, n_idx, d_idx, idx_bits):
    gqa = n_heads // n_kv
    reps = tk // LANE
    i = pl.program_id(1)
    q0 = q_off + i * tq
    kb_diag = q0 // tk
    nkb = kb_diag + 1
    q_pos = q0 + lax.broadcasted_iota(I32, (tq, 1), 0)
    lane = lax.broadcasted_iota(I32, (1, tk), 1)
    lane1 = lax.broadcasted_iota(I32, (1, LANE), 1)
    tile = lambda x: jnp.concatenate([x] * reps, axis=1)

    qi = qi_ref[...]
    qw = qw_ref[...].astype(F32)
    qi_heads = [qi[:, h * d_idx:(h + 1) * d_idx] for h in range(n_idx)]
    for h in range(n_idx):
        wb_scr[h] = jnp.broadcast_to(qw[:, d_idx + h:d_idx + h + 1] * (d_idx ** -0.5), (tq, LANE))

    def score_block(kb, carry):
        ks = pl.multiple_of(kb * tk, tk)
        ki = ki_ref[pl.ds(ks, tk), :][:, :d_idx]
        sc = jnp.zeros((tq, tk), F32)
        for h in range(n_idx):
            sc = sc + jnp.maximum(_dot_nt(qi_heads[h], ki), 0.0) * tile(wb_scr[h])
        sc = sc * (n_idx ** -0.5)
        sc = jnp.where(sc == 0.0, 0.0, sc)
        k_pos = ks + lane
        allowed = ((k_pos >> CHUNK_SHIFT) <= (q_pos >> CHUNK_SHIFT)) & (k_pos < s_valid)
        sc = jnp.where(allowed, sc, -jnp.inf)
        bits = pltpu.bitcast(sc, I32)
        key_scr[kb] = bits ^ ((bits >> 31) & 0x7FFFFFFF)
        return carry

    lax.fori_loop(0, nkb, score_block, 0)

    def count(pred):
        def body(kb, acc):
            key = key_scr[kb]
            for c in range(reps):
                acc = acc + jnp.where(pred(kb, c, key[:, c * LANE:(c + 1) * LANE]), 1.0, 0.0)
            return acc
        acc = lax.fori_loop(0, nkb, body, jnp.zeros((tq, LANE), F32))
        return jnp.sum(acc, axis=1, keepdims=True)

    def bisect_cond(state):
        it, _, _, open_rows = state
        return (it < 32) & (open_rows > 0.0)

    def bisect_body(state):
        it, t_u, cnt_t, _ = state
        cand_u = t_u | (jnp.int32(1) << (31 - it))
        cand_b = jnp.broadcast_to(cand_u ^ INT_MIN, (tq, LANE))
        cnt = count(lambda kb, c, key: key >= cand_b)
        take = cnt >= k_sel
        t_u = jnp.where(take, cand_u, t_u)
        cnt_t = jnp.where(take, cnt, cnt_t)
        return it + 1, t_u, cnt_t, jnp.max(jnp.where(cnt_t > k_sel, 1.0, 0.0))

    total = (nkb * tk).astype(F32)
    _, t_u, cnt_t, _ = lax.while_loop(
        bisect_cond, bisect_body,
        (jnp.int32(0), jnp.zeros((tq, 1), I32), jnp.full((tq, 1), total, F32),
         jnp.where(total > k_sel, 1.0, 0.0)))
    thr = t_u ^ INT_MIN
    thr_b = jnp.broadcast_to(thr, (tq, LANE))

    def tie_search():
        need = k_sel - count(lambda kb, c, key: key > thr_b)

        def index_bit(it, j_hi):
            cand = j_hi | (jnp.int32(1) << (idx_bits - 1 - it))
            cand_b = jnp.broadcast_to(cand, (tq, LANE))
            cnt = count(lambda kb, c, key: (key == thr_b) & (kb * tk + c * LANE + lane1 < cand_b))
            return jnp.where(cnt < need, cand, j_hi)

        return lax.fori_loop(0, idx_bits, index_bit, jnp.zeros((tq, 1), I32))

    tie_hi = lax.cond(jnp.max(cnt_t) > k_sel, tie_search, lambda: jnp.full((tq, 1), 2 ** 30, I32))

    m_scr[...] = jnp.full(m_scr.shape, MASKED, F32)
    acc_scr[...] = jnp.zeros(acc_scr.shape, F32)
    scale2 = (hd ** -0.5) * LOG2E
    q_groups = [jnp.concatenate([q_ref[:, (g * gqa + r) * hd:(g * gqa + r + 1) * hd]
                                 for r in range(gqa)], axis=0) for g in range(n_kv)]
    far_rows = [jnp.concatenate([jnp.full((tq, LANE), far_ref[g * gqa + r], F32)
                                 for r in range(gqa)], axis=0) for g in range(n_kv)]
    ones = jnp.ones((tk, hd), BF16)

    def attend_block(kb, near):
        ks = pl.multiple_of(kb * tk, tk)
        key = key_scr[kb]
        sel = (key > thr) | ((key == thr) & (ks + lane <= tie_hi))
        sel = sel & (key > KEY_NEG_INF)
        mbias = jnp.where(sel, 0.0, MASKED)
        for g in range(n_kv):
            kg = k_ref[pl.ds(ks, tk), g * hd:(g + 1) * hd]
            vext = jnp.concatenate([v_ref[pl.ds(ks, tk), g * hd:(g + 1) * hd], ones], axis=1)
            s = _dot_nt(q_groups[g], kg) * scale2
            m_old = m_scr[g]
            if near is None:
                t = s + jnp.concatenate([mbias] * gqa, axis=0)
                m_new = jnp.maximum(m_old, jnp.max(t, axis=-1, keepdims=True) + far_rows[g])
                p = jnp.exp2(t - tile(m_new - far_rows[g]))
            else:
                t = s + jnp.concatenate([mbias + bd_ref[g * gqa + r, near] for r in range(gqa)], axis=0)
                m_new = jnp.maximum(m_old, jnp.max(t, axis=-1, keepdims=True))
                p = jnp.exp2(t - tile(m_new))
            alpha = jnp.exp2(m_old - m_new)
            acc_scr[g] = (jnp.concatenate([alpha, alpha], axis=1) * acc_scr[g]
                          + jnp.dot(p.astype(BF16), vext, preferred_element_type=F32))
            m_scr[g] = m_new

    def far_block(kb, carry):
        attend_block(kb, None)
        return carry

    lax.fori_loop(0, jnp.maximum(kb_diag - 1, 0), far_block, 0)

    @pl.when(kb_diag >= 1)
    def _():
        attend_block(kb_diag - 1, 1)

    attend_block(kb_diag, 0)

    for g in range(n_kv):
        acc = acc_scr[g]
        o = acc[:, :hd] / acc[:, hd:]
        for r in range(gqa):
            h = g * gqa + r
            o_ref[:, h * hd:(h + 1) * hd] = o[r * tq:(r + 1) * tq].astype(BF16)


def _dsa_attn(q_src, k_src, bias_near, bias_far, *, n_batch, t, s_pad, s_valid, q_off, tq, tk,
              k_sel, n_heads, n_kv, hd, n_idx, d_idx):
    qarr, q_rb0, q_cb, qi_cb, qw_cb = q_src
    karr, k_cb, varr, v_cb, kiarr, ki_cb = k_src
    assert tk >= MAX_DISTANCE and tk % tq == 0 and q_off % tq == 0 and s_pad % tk == 0
    nqb = t // tq
    nkb_max = s_pad // tk
    gqa = n_heads // n_kv
    wq, wkv, wqi = n_heads * hd, n_kv * hd, n_idx * d_idx
    qrow = lambda b, i: q_rb0 + b * nqb + i
    kernel = functools.partial(
        _dsa_kernel, tq=tq, tk=tk, q_off=q_off, s_valid=s_valid, k_sel=k_sel, n_heads=n_heads,
        n_kv=n_kv, hd=hd, n_idx=n_idx, d_idx=d_idx, idx_bits=max((s_pad - 1).bit_length(), 1))
    return pl.pallas_call(
        kernel,
        grid=(n_batch, nqb),
        in_specs=[pl.BlockSpec(memory_space=pltpu.SMEM),
                  pl.BlockSpec((tq, wq), lambda b, i: (qrow(b, i), q_cb)),
                  pl.BlockSpec((tq, wqi), lambda b, i: (qrow(b, i), qi_cb)),
                  pl.BlockSpec((tq, LANE), lambda b, i: (qrow(b, i), qw_cb)),
                  _resident((s_pad, wkv), lambda b, i: (b, k_cb)),
                  _resident((s_pad, wkv), lambda b, i: (b, v_cb)),
                  _resident((s_pad, LANE), lambda b, i: (b, ki_cb)),
                  pl.BlockSpec((None, n_heads, 2, tq, tk),
                               lambda b, i: (_near_phase(q_off, tq, tk, nqb, i), 0, 0, 0, 0))],
        out_specs=pl.BlockSpec((tq, wq), lambda b, i: (b * nqb + i, 0)),
        out_shape=jax.ShapeDtypeStruct((n_batch * t, wq), BF16),
        scratch_shapes=[pltpu.VMEM((nkb_max, tq, tk), I32),
                        pltpu.VMEM((n_idx, tq, LANE), F32),
                        pltpu.VMEM((n_kv, gqa * tq, LANE), F32),
                        pltpu.VMEM((n_kv, gqa * tq, 2 * hd), F32)],
        compiler_params=_cparams(2),
        name="dsa_attn",
    )(bias_far, qarr, qarr, qarr, karr, varr, kiarr, bias_near)


def _stick_kernel(q_ref, k_ref, v_ref, o_ref, carry_scr, acc_scr, *, tq, tk, q_off, hd):
    i = pl.program_id(2)
    q0 = q_off + i * tq
    q_pos = q0 + lax.broadcasted_iota(I32, (tq, 1), 0)
    lane = lax.broadcasted_iota(I32, (1, tk), 1)
    tri = (lax.broadcasted_iota(I32, (tk, tk), 0) > lax.broadcasted_iota(I32, (tk, tk), 1)).astype(BF16)
    q = q_ref[...]
    scale = hd ** -0.5
    carry_scr[...] = jnp.zeros(carry_scr.shape, F32)
    acc_scr[...] = jnp.zeros(acc_scr.shape, F32)

    def cond(state):
        kb, carry_max = state
        return (kb >= 0) & (carry_max > EXP_UNDERFLOW)

    def body(state):
        kb, _ = state
        ks = pl.multiple_of(kb * tk, tk)
        z = _dot_nt(q, k_ref[pl.ds(ks, tk), :]) * scale
        before = (ks + lane) < q_pos
        log_beta = jnp.minimum(z, 0.0) - jnp.log(1.0 + jnp.exp(-jnp.abs(z)))
        log_keep = jnp.where(before, log_beta - z, 0.0)
        hi = log_keep.astype(BF16)
        lo = (log_keep - hi.astype(F32)).astype(BF16)
        tail = (jnp.dot(hi, tri, preferred_element_type=F32)
                + jnp.dot(lo, tri, preferred_element_type=F32))
        carry = carry_scr[...]
        w = jnp.where(before, jnp.exp(log_beta + tail + carry), 0.0)
        acc_scr[...] += jnp.dot(w.astype(BF16), v_ref[pl.ds(ks, tk), :], preferred_element_type=F32)
        carry = carry + jnp.sum(log_keep, axis=-1, keepdims=True)
        carry_scr[...] = carry
        return kb - 1, jnp.max(carry)

    lax.while_loop(cond, body, ((q0 + tq - 1) // tk, jnp.float32(0.0)))
    o_ref[...] = acc_scr[...].astype(BF16)


def _stick_attn(q_src, k_src, *, n_batch, t, s_pad, q_off, tq, tk, n_heads, hd):
    qarr, q_rb0, q_cb0 = q_src
    karr, k_cb0, varr, v_cb0 = k_src
    nqb = t // tq
    return pl.pallas_call(
        functools.partial(_stick_kernel, tq=tq, tk=tk, q_off=q_off, hd=hd),
        grid=(n_batch, n_heads, nqb),
        in_specs=[pl.BlockSpec((tq, hd), lambda b, h, i: (q_rb0 + b * nqb + i, q_cb0 + h)),
                  pl.BlockSpec((s_pad, hd), lambda b, h, i: (b, k_cb0 + h)),
                  pl.BlockSpec((s_pad, hd), lambda b, h, i: (b, v_cb0 + h))],
        out_specs=pl.BlockSpec((tq, hd), lambda b, h, i: (b * nqb + i, h)),
        out_shape=jax.ShapeDtypeStruct((n_batch * t, n_heads * hd), BF16),
        scratch_shapes=[pltpu.VMEM((tq, 1), F32), pltpu.VMEM((tq, hd), F32)],
        compiler_params=_cparams(3),
        name="stick_attn",
    )(qarr, karr, varr)


def _diff_kernel(scal_ref, q_ref, k_ref, v_ref, bd_ref, g_ref, o_ref, m_scr, acc_scr, *,
                 tq, tk, q_off, s_valid, hd, out_scale):
    h = pl.program_id(1)
    i = pl.program_id(2)
    reps = tk // LANE
    q0 = q_off + i * tq
    kb_diag = q0 // tk
    q_pos = q0 + lax.broadcasted_iota(I32, (tq, 1), 0)
    lane = lax.broadcasted_iota(I32, (1, tk), 1)
    tile = lambda x: jnp.concatenate([x] * reps, axis=1)
    scale2 = (hd ** -0.5) * LOG2E
    m_scr[...] = jnp.full(m_scr.shape, MASKED, F32)
    acc_scr[...] = jnp.zeros(acc_scr.shape, F32)
    q_maps = [q_ref[:, m * hd:(m + 1) * hd] for m in range(2)]
    ones = jnp.ones((tk, hd), BF16)

    def attend_block(kb, near):
        ks = pl.multiple_of(kb * tk, tk)
        vext = jnp.concatenate([v_ref[pl.ds(ks, tk), :], ones], axis=1)
        if near is not None:
            k_pos = ks + lane
            allowed = ((k_pos >> CHUNK_SHIFT) <= (q_pos >> CHUNK_SHIFT)) & (k_pos < s_valid)
            mbias = jnp.where(allowed, 0.0, MASKED)
        for m in range(2):
            s = _dot_nt(q_maps[m], k_ref[pl.ds(ks, tk), m * hd:(m + 1) * hd]) * scale2
            m_old = m_scr[m]
            if near is None:
                far = scal_ref[1 + 2 * h + m]
                m_new = jnp.maximum(m_old, jnp.max(s, axis=-1, keepdims=True) + far)
                p = jnp.exp2(s - tile(m_new - far))
            else:
                t = s + (mbias + bd_ref[m, near])
                m_new = jnp.maximum(m_old, jnp.max(t, axis=-1, keepdims=True))
                p = jnp.exp2(t - tile(m_new))
            alpha = jnp.exp2(m_old - m_new)
            acc_scr[m] = (jnp.concatenate([alpha] * 3, axis=1) * acc_scr[m]
                          + jnp.dot(p.astype(BF16), vext, preferred_element_type=F32))
            m_scr[m] = m_new

    n_far = jnp.maximum(kb_diag - 1, 0)

    def far_pair(pair, carry):
        attend_block(2 * pair, None)
        attend_block(2 * pair + 1, None)
        return carry

    lax.fori_loop(0, n_far // 2, far_pair, 0)

    @pl.when(n_far % 2 == 1)
    def _():
        attend_block(n_far - 1, None)

    @pl.when(kb_diag >= 1)
    def _():
        attend_block(kb_diag - 1, 1)

    attend_block(kb_diag, 0)

    def normalised(m):
        acc = acc_scr[m]
        return acc[:, :2 * hd] / jnp.concatenate([acc[:, 2 * hd:]] * 2, axis=1)

    o = normalised(0) - scal_ref[0] * normalised(1)
    o = o * lax.rsqrt(jnp.mean(o * o, axis=-1, keepdims=True) + LN_EPS) * g_ref[...]
    o_ref[...] = (o * out_scale).astype(BF16)


def _diff_attn(q_src, k_src, bias_near, scalars, subln_g, *, n_batch, t, s_pad, s_valid, q_off,
               tq, tk, n_heads, hd, out_scale):
    qarr, q_rb0, q_cb0 = q_src
    karr, k_cb0, varr, v_cb0 = k_src
    assert tk >= MAX_DISTANCE and tk % tq == 0 and q_off % tq == 0 and s_pad % tk == 0
    nqb = t // tq
    w = 2 * hd
    return pl.pallas_call(
        functools.partial(_diff_kernel, tq=tq, tk=tk, q_off=q_off, s_valid=s_valid, hd=hd,
                          out_scale=out_scale),
        grid=(n_batch, n_heads, nqb),
        in_specs=[pl.BlockSpec(memory_space=pltpu.SMEM),
                  pl.BlockSpec((tq, w), lambda b, h, i: (q_rb0 + b * nqb + i, q_cb0 + h)),
                  pl.BlockSpec((s_pad, w), lambda b, h, i: (b, k_cb0 + h)),
                  pl.BlockSpec((s_pad, w), lambda b, h, i: (b, v_cb0 + h)),
                  pl.BlockSpec((None, None, 2, 2, tq, tk),
                               lambda b, h, i: (_near_phase(q_off, tq, tk, nqb, i), h, 0, 0, 0, 0)),
                  pl.BlockSpec((1, w), lambda b, h, i: (0, 0))],
        out_specs=pl.BlockSpec((tq, w), lambda b, h, i: (b * nqb + i, h)),
        out_shape=jax.ShapeDtypeStruct((n_batch * t, n_heads * w), BF16),
        scratch_shapes=[pltpu.VMEM((2, tq, LANE), F32), pltpu.VMEM((2, tq, w + hd), F32)],
        compiler_params=_cparams(3),
        name="diff_attn",
    )(scalars, qarr, karr, varr, bias_near, subln_g)


def _router_kernel(x_ref, wh_ref, wl_ref, b_ref, e_ref, g_ref, r_ref, c_ref, seen_scr):
    @pl.when(pl.program_id(0) == 0)
    def _():
        seen_scr[...] = jnp.zeros(seen_scr.shape, F32)

    x = x_ref[...]
    xh = x.astype(BF16)
    xl = (x - xh.astype(F32)).astype(BF16)
    wh = wh_ref[...]
    logits = (jnp.dot(xh, wh, preferred_element_type=F32)
              + jnp.dot(xl, wh, preferred_element_type=F32)
              + jnp.dot(xh, wl_ref[...], preferred_element_type=F32)) + b_ref[...]
    tm = logits.shape[0]
    lane = lax.broadcasted_iota(I32, logits.shape, 1)
    lane_f = lane.astype(F32)
    experts = jnp.zeros(logits.shape, I32)
    gates = jnp.zeros(logits.shape, F32)
    chosen = []
    top = None
    for k in range(TOP_K):
        m = jnp.max(logits, axis=-1, keepdims=True)
        ix = jnp.min(jnp.where(logits == m, lane_f, float(LANE)), axis=-1, keepdims=True).astype(I32)
        top = m if top is None else top
        experts = jnp.where(lane == k, ix, experts)
        gates = jnp.where(lane == k, jnp.exp(m - top), gates)
        chosen.append(lane == ix)
        logits = jnp.where(chosen[-1], -jnp.inf, logits)
    e_ref[...] = experts
    g_ref[...] = gates / jnp.sum(gates, axis=-1, keepdims=True)

    hot = functools.reduce(jnp.logical_or, chosen)
    hot_f = jnp.where(hot, 1.0, 0.0)
    lower = (lax.broadcasted_iota(I32, (tm, tm), 0) > lax.broadcasted_iota(I32, (tm, tm), 1)).astype(BF16)
    earlier = jnp.dot(lower, hot_f.astype(BF16), preferred_element_type=F32) + seen_scr[...]
    ranks = jnp.zeros(logits.shape, F32)
    for k in range(TOP_K):
        rank_k = jnp.sum(jnp.where(chosen[k], earlier, 0.0), axis=-1, keepdims=True)
        ranks = jnp.where(lane == k, rank_k, ranks)
    r_ref[...] = ranks.astype(I32)
    seen = seen_scr[...] + jnp.sum(hot_f, axis=0, keepdims=True)
    seen_scr[...] = seen
    c_ref[...] = seen


def _router(x, w_hi, w_lo, bias):
    m, d = x.shape
    tm = _pick_tile(m, (256, 128, 64, 32, 16))
    row = lambda i: (i, 0)
    fix = lambda i: (0, 0)
    return pl.pallas_call(
        _router_kernel,
        grid=(m // tm,),
        in_specs=[pl.BlockSpec((tm, d), row), pl.BlockSpec((d, LANE), fix),
                  pl.BlockSpec((d, LANE), fix), pl.BlockSpec((1, LANE), fix)],
        out_specs=[pl.BlockSpec((tm, LANE), row), pl.BlockSpec((tm, LANE), row),
                   pl.BlockSpec((tm, LANE), row), pl.BlockSpec((1, LANE), fix)],
        out_shape=[jax.ShapeDtypeStruct((m, LANE), I32), jax.ShapeDtypeStruct((m, LANE), F32),
                   jax.ShapeDtypeStruct((m, LANE), I32), jax.ShapeDtypeStruct((1, LANE), F32)],
        scratch_shapes=[pltpu.VMEM((1, LANE), F32)],
        compiler_params=_cparams(1),
        name="router",
    )(x, w_hi, w_lo, bias)


def _dispatch_kernel(dest_ref, x_ref, rows_in, rows_out, sem, *, tm):
    del rows_in

    def send(r, carry):
        for k in range(TOP_K):
            _row_copy(x_ref, r, rows_out, dest_ref[0, r * TOP_K + k], sem).start()
        return carry
    lax.fori_loop(0, tm, send, 0)

    def drain(r, carry):
        for k in range(TOP_K):
            _row_copy(x_ref, 0, rows_out, 0, sem).wait()
        return carry
    lax.fori_loop(0, tm, drain, 0)


def _dispatch(x_packed, dest, rows_buf):
    m, w = x_packed.shape
    tm = _pick_tile(m, (256, 128, 64, 32, 16))
    n_steps = m // tm
    return pl.pallas_call(
        functools.partial(_dispatch_kernel, tm=tm),
        grid=(n_steps,),
        in_specs=[pl.BlockSpec((None, 1, tm * TOP_K), lambda i: (i, 0, 0), memory_space=pltpu.SMEM),
                  pl.BlockSpec((tm, w), lambda i: (i, 0)),
                  pl.BlockSpec(memory_space=pl.ANY)],
        out_specs=pl.BlockSpec(memory_space=pl.ANY),
        out_shape=jax.ShapeDtypeStruct(rows_buf.shape, I32),
        scratch_shapes=[pltpu.SemaphoreType.DMA(())],
        input_output_aliases={2: 0},
        compiler_params=_cparams(1),
        name="dispatch",
    )(dest.reshape(n_steps, 1, tm * TOP_K), x_packed, rows_buf)


def _cast_rows(src_ref, dst_ref, rows_per_step):
    def body(c, carry):
        r = pl.multiple_of(c * rows_per_step, rows_per_step)
        dst_ref[pl.ds(r, rows_per_step), :] = src_ref[pl.ds(r, rows_per_step), :].astype(BF16)
        return carry
    lax.fori_loop(0, src_ref.shape[0] // rows_per_step, body, 0)


def _expert_changed(be_ref, j):
    return (j == 0) | (be_ref[j] != be_ref[jnp.maximum(j - 1, 0)])


def _moe_up_kernel(be_ref, nu_ref, x_ref, wg_ref, wl_ref, bg_ref, bl_ref, h_ref, wg_bf, wl_bf, *,
                   cast_rows):
    j = pl.program_id(1)

    @pl.when(_expert_changed(be_ref, j))
    def _():
        _cast_rows(wg_ref, wg_bf, cast_rows)
        _cast_rows(wl_ref, wl_bf, cast_rows)

    @pl.when(j < nu_ref[0])
    def _():
        x = _unpack_halves(x_ref[...])
        glu = jnp.dot(x, wg_bf[...], preferred_element_type=F32) + bg_ref[...]
        lin = jnp.dot(x, wl_bf[...], preferred_element_type=F32) + bl_ref[...]
        glu = jnp.minimum(glu, SWIGLU_LIMIT)
        lin = jnp.clip(lin, -SWIGLU_LIMIT, SWIGLU_LIMIT)
        h_ref[...] = (glu * jax.nn.sigmoid(SWIGLU_ALPHA * glu) * (lin + 1.0)).astype(BF16)

    @pl.when(j >= nu_ref[0])
    def _():
        h_ref[...] = jnp.zeros(h_ref.shape, BF16)


def _moe_down_kernel(be_ref, nu_ref, h_ref, wd_ref, bd_ref, y_ref, wd_bf, *, cast_rows):
    j = pl.program_id(1)

    @pl.when(_expert_changed(be_ref, j))
    def _():
        _cast_rows(wd_ref, wd_bf, cast_rows)

    @pl.when(j < nu_ref[0])
    def _():
        y_ref[...] = jnp.dot(h_ref[...], wd_bf[...], preferred_element_type=F32) + bd_ref[...]

    @pl.when(j >= nu_ref[0])
    def _():
        y_ref[...] = jnp.zeros(y_ref.shape, F32)


def _moe_experts(xs, block_e, n_used, layer, w_gate_up, b_gate_up, w_down, b_down, tm):
    n_rows = xs.shape[0]
    d = 2 * xs.shape[1]
    n_blocks = n_rows // tm
    d_ff = w_down.shape[2]
    tf = _pick_tile(d_ff, (1024, 512, 256, 128))
    nf = d_ff // tf
    tn = _pick_tile(d, (1024, 512, 256, 128))
    blk = lambda j, nu: jnp.minimum(j, nu[0] - 1)
    hidden = pl.pallas_call(
        functools.partial(_moe_up_kernel, cast_rows=_pick_tile(d, (256, 128, 64, 32, 16, 8))),
        grid_spec=pltpu.PrefetchScalarGridSpec(
            num_scalar_prefetch=2,
            grid=(nf, n_blocks),
            in_specs=[pl.BlockSpec((tm, d // 2), lambda f, j, be, nu: (blk(j, nu), 0)),
                      pl.BlockSpec((None, None, d, tf), lambda f, j, be, nu: (layer, be[j], 0, f)),
                      pl.BlockSpec((None, None, d, tf), lambda f, j, be, nu: (layer, be[j], 0, nf + f)),
                      pl.BlockSpec((None, None, 1, tf), lambda f, j, be, nu: (layer, be[j], 0, f)),
                      pl.BlockSpec((None, None, 1, tf), lambda f, j, be, nu: (layer, be[j], 0, nf + f))],
            out_specs=pl.BlockSpec((tm, tf), lambda f, j, be, nu: (j, f)),
            scratch_shapes=[pltpu.VMEM((d, tf), BF16), pltpu.VMEM((d, tf), BF16)]),
        out_shape=jax.ShapeDtypeStruct((n_rows, d_ff), BF16),
        compiler_params=_cparams(2),
        name="moe_up",
    )(block_e, n_used, xs, w_gate_up, w_gate_up, b_gate_up, b_gate_up)
    return pl.pallas_call(
        functools.partial(_moe_down_kernel, cast_rows=_pick_tile(d_ff, (256, 128, 64, 32, 16, 8))),
        grid_spec=pltpu.PrefetchScalarGridSpec(
            num_scalar_prefetch=2,
            grid=(d // tn, n_blocks),
            in_specs=[pl.BlockSpec((tm, d_ff), lambda n, j, be, nu: (blk(j, nu), 0)),
                      pl.BlockSpec((None, None, d_ff, tn), lambda n, j, be, nu: (layer, be[j], 0, n)),
                      pl.BlockSpec((None, None, 1, tn), lambda n, j, be, nu: (layer, be[j], 0, n))],
            out_specs=pl.BlockSpec((tm, tn), lambda n, j, be, nu: (j, n)),
            scratch_shapes=[pltpu.VMEM((d_ff, tn), BF16)]),
        out_shape=jax.ShapeDtypeStruct((n_rows, d), F32),
        compiler_params=_cparams(2),
        name="moe_down",
    )(block_e, n_used, hidden, w_down, b_down)


def _moe(x_f32, x_packed, rows_buf, layer, w_router_hi, w_router_lo, b_router, w_gate_up,
         b_gate_up, w_down, b_down):
    n = x_f32.shape[0]
    n_experts = w_gate_up.shape[1]
    tm = 512 if n * TOP_K >= 1024 * n_experts else 128
    experts, gates, ranks, counts = _router(x_f32, w_router_hi, w_router_lo, b_router)
    counts = counts[0, :n_experts].astype(I32)
    pend = jnp.cumsum((counts + tm - 1) // tm * tm)
    pstart = pend - (counts + tm - 1) // tm * tm
    n_blocks = -(-(n * TOP_K + n_experts * (tm - 1)) // tm)
    block_e = jnp.minimum(jnp.sum(pend[None, :] <= (jnp.arange(n_blocks) * tm)[:, None], axis=1),
                          n_experts - 1).astype(I32)
    n_used = (pend[-1] // tm).astype(I32).reshape(1)
    e4 = experts[:, :TOP_K]
    onehot = e4[..., None] == jnp.arange(n_experts, dtype=I32)
    dest = (ranks[:, :TOP_K] + jnp.sum(jnp.where(onehot, pstart.astype(I32), 0), axis=-1)).reshape(-1)
    if rows_buf is None:
        rows_buf = jnp.zeros((n_blocks * tm,) + x_packed.shape[1:], I32)
    xs = _dispatch(x_packed, dest, rows_buf)
    y = _moe_experts(xs, block_e, n_used, layer, w_gate_up, b_gate_up, w_down, b_down, tm)
    return y, dest, gates, xs


def _pad_cols(w, n):
    return jnp.pad(w, ((0, 0), (0, n - w.shape[1])))


def _keys_with_past(past, new, s_pad):
    b, p, w = past.shape
    rows = jnp.pad(past.astype(BF16), ((0, 0), (0, s_pad - p), (0, 0)))
    rows = lax.dynamic_update_slice(rows, new.astype(BF16), (0, p, 0))
    return rows.reshape(b * s_pad, w)


def kernel(x_prompt, x_sample, cache_a_k, cache_a_v, cache_a_kidx, cache_b_k, cache_b_v,
           cache_c_k, cache_c_v, cache_mem_k, cache_mem_v, mem_prompt,
           w_in_a, w_in_b, w_in_c, diff_lambda, diff_subln, rel_bias, w_mem_kv, w_o,
           ln_g, ln_b, w_router, b_router, w_gate_up, b_gate_up, w_down, b_down):
    bp, tp, d = x_prompt.shape
    bs, ts, _ = x_sample.shape
    depth = w_o.shape[0]
    past_len = cache_a_k.shape[2]
    n_kv, hd = cache_a_k.shape[3], cache_a_k.shape[4]
    d_idx = cache_a_kidx.shape[3]
    n_mem, h_mem = cache_mem_k.shape[2], cache_mem_k.shape[3]
    mem_w = h_mem * hd
    tok_w = d - mem_w
    h_a = tok_w // hd
    h_b = cache_b_k.shape[3]
    hc = cache_c_k.shape[3]
    n_idx = (w_in_a.shape[2] - tok_w - 2 * n_kv * hd - d_idx - mem_w) // (d_idx + 1)
    n_experts = w_router.shape[2]
    alpha = (2 * depth) ** 0.25
    mp, ms = bp * tp, bs * ts
    s_new = past_len + ts
    kv_w, qi_w = n_kv * hd, n_idx * d_idx

    o_k, o_v, o_qi = tok_w, tok_w + kv_w, tok_w + 2 * kv_w
    o_ki = o_qi + qi_w
    o_wi = o_ki + d_idx
    o_mq = o_wi + n_idx
    a_mq = o_qi + qi_w
    a_kw = a_mq + mem_w
    cols_a = _round_up(a_kw + LANE, 5 * LANE if d >= 5 * LANE else LANE)
    w_a = [_pad_cols(jnp.concatenate([w[:, :o_ki], w[:, o_mq:], w[:, o_ki:o_mq]], axis=1),
                     cols_a).astype(BF16) for w in w_in_a]
    w_ki = [w[:, o_ki:o_wi].astype(BF16) for w in w_in_a]
    w_b = [w.astype(BF16) for w in w_in_b]
    w_c = [w.astype(BF16) for w in w_in_c]
    w_o_bf = w_o.astype(BF16)
    w_r_pad = jnp.pad(w_router, ((0, 0), (0, 0), (0, LANE - n_experts)))
    w_r_hi = w_r_pad.astype(BF16)
    w_r_lo = (w_r_pad - w_r_hi.astype(F32)).astype(BF16)
    b_r_pad = jnp.pad(b_router, ((0, 0), (0, LANE - n_experts)), constant_values=MASKED)[:, None, :]
    b_gu = b_gate_up[:, :, None, :]
    b_dn = b_down[:, :, None, :]

    tq_a = _pick_tile(tp, (128, 64))
    tk_a = _pick_tile(tp, (512, 256, 128))
    tq_b = _pick_tile(tp, (256, 128, 64))
    tk_b = min(tq_b, 256)
    tq_c = _pick_tile(tp, (512, 256, 128))
    tk_c = _pick_tile(tp, (512, 256, 128))
    tq_m = _pick_tile(tp, (512, 256, 128))
    sp_a, sp_b, sp_c = _round_up(s_new, tk_a), _round_up(s_new, tk_b), _round_up(s_new, tk_c)

    far2 = _far_bias(rel_bias) * LOG2E
    near_a_p = _near_bias(rel_bias, tq_a, tk_a, 0, tp // tq_a) * LOG2E
    near_a_s = _near_bias(rel_bias, ts, tk_a, past_len, 1) * LOG2E
    near_c_p = _near_bias(rel_bias, tq_c, tk_c, 0, tp // tq_c) * LOG2E
    near_c_s = _near_bias(rel_bias, ts, tk_c, past_len, 1) * LOG2E
    pair = lambda nb: nb.reshape(nb.shape[0], hc, 2, *nb.shape[2:])

    w_mem = jnp.moveaxis(w_mem_kv, 0, 1).reshape(d, depth * 2 * mem_w).astype(BF16)
    mem_f, mem_b = _matmul(mem_prompt.reshape(bp * n_mem, d).astype(BF16), w_mem)
    mem_f = mem_f.reshape(bp, n_mem, depth, 2, h_mem, hd)
    mem_k_p = jnp.moveaxis(mem_f[:, :, :, 0], 2, 0)
    mem_v_p = jnp.moveaxis(mem_f[:, :, :, 1], 2, 0)
    mem_s_k = cache_mem_k.reshape(depth, bs * n_mem, mem_w).astype(BF16)
    mem_s_v = cache_mem_v.reshape(depth, bs * n_mem, mem_w).astype(BF16)

    x_f = jnp.concatenate([x_prompt.reshape(mp, d), x_sample.reshape(ms, d)], axis=0)
    x_b = x_f.astype(BF16)
    rows_p = [None] * depth
    rows_s = [None] * depth
    rows_buf = None

    for i in range(depth):
        kind, j = i % N_MIXERS, i // N_MIXERS
        w_in = (w_a, w_b, w_c)[kind][j]
        h_f, h_b_ = _matmul(x_b, w_in)
        hs_f = h_f[mp:].reshape(bs, ts, -1)
        hp = lambda lo, hi: h_f[:mp, lo:hi]
        rb_s = mp // ts

        if kind == 0:
            k_sel_p = min(K_SEL_MAX, tp // 4)
            k_sel_s = min(K_SEL_MAX, s_new // 4)
            dims = dict(n_heads=h_a, n_kv=n_kv, hd=hd, n_idx=n_idx, d_idx=d_idx)
            mix_p = _dsa_attn(
                (h_b_, 0, 0, o_qi // qi_w, a_kw // LANE),
                (h_b_, o_k // kv_w, h_b_, o_v // kv_w, h_b_, a_kw // LANE),
                near_a_p, far2, n_batch=bp, t=tp, s_pad=tp, s_valid=tp, q_off=0,
                tq=tq_a, tk=tk_a, k_sel=k_sel_p, **dims)
            kiw_new = jnp.pad(hs_f[..., a_kw:a_kw + d_idx], ((0, 0), (0, 0), (0, LANE - d_idx)))
            kiw_past = jnp.pad(cache_a_kidx[j], ((0, 0), (0, 0), (0, LANE - d_idx)))
            k_all = _keys_with_past(cache_a_k[j].reshape(bs, past_len, kv_w), hs_f[..., o_k:o_v], sp_a)
            v_all = _keys_with_past(cache_a_v[j].reshape(bs, past_len, kv_w), hs_f[..., o_v:o_qi], sp_a)
            ki_all = _keys_with_past(kiw_past, kiw_new, sp_a)
            mix_s = _dsa_attn(
                (h_b_, rb_s, 0, o_qi // qi_w, a_kw // LANE),
                (k_all, 0, v_all, 0, ki_all, 0),
                near_a_s, far2, n_batch=bs, t=ts, s_pad=sp_a, s_valid=s_new, q_off=past_len,
                tq=ts, tk=tk_a, k_sel=k_sel_s, **dims)
            mq_cb = a_mq // mem_w
            ki_f, _ = _matmul(x_b, w_ki[j])
            rows_p[i] = (hp(o_k, o_v).reshape(bp, tp, n_kv, hd), hp(o_v, o_qi).reshape(bp, tp, n_kv, hd),
                         ki_f[:mp].reshape(bp, tp, d_idx))
            rows_s[i] = (hs_f[..., o_k:o_v].reshape(bs, ts, n_kv, hd),
                         hs_f[..., o_v:o_qi].reshape(bs, ts, n_kv, hd), ki_f[mp:].reshape(bs, ts, d_idx))
        elif kind == 1:
            dims = dict(n_heads=h_b, hd=hd)
            mix_p = _stick_attn((h_b_, 0, 0), (h_b_, h_b, h_b_, 2 * h_b),
                                n_batch=bp, t=tp, s_pad=tp, q_off=0, tq=tq_b, tk=tk_b, **dims)
            k_all = _keys_with_past(cache_b_k[j].reshape(bs, past_len, tok_w), hs_f[..., tok_w:2 * tok_w], sp_b)
            v_all = _keys_with_past(cache_b_v[j].reshape(bs, past_len, tok_w), hs_f[..., 2 * tok_w:3 * tok_w], sp_b)
            mix_s = _stick_attn((h_b_, rb_s, 0), (k_all, 0, v_all, 0),
                                n_batch=bs, t=ts, s_pad=sp_b, q_off=past_len, tq=ts, tk=tk_b, **dims)
            mq_cb = 3 * tok_w // mem_w
            rows_p[i] = (hp(tok_w, 2 * tok_w).reshape(bp, tp, h_b, hd),
                         hp(2 * tok_w, 3 * tok_w).reshape(bp, tp, h_b, hd))
            rows_s[i] = (hs_f[..., tok_w:2 * tok_w].reshape(bs, ts, h_b, hd),
                         hs_f[..., 2 * tok_w:3 * tok_w].reshape(bs, ts, h_b, hd))
        else:
            lambda_init = 0.8 - 0.6 * math.exp(-0.3 * i)
            lv = diff_lambda[j].astype(F32)
            lam = jnp.exp(jnp.sum(lv[0] * lv[1])) - jnp.exp(jnp.sum(lv[2] * lv[3])) + lambda_init
            scalars = jnp.concatenate([lam.reshape(1), far2]).astype(F32)
            g = diff_subln[j].reshape(1, 2 * hd).astype(F32)
            dims = dict(n_heads=hc, hd=hd, out_scale=1.0 - lambda_init)
            mix_p = _diff_attn((h_b_, 0, 0), (h_b_, hc, h_b_, 2 * hc), pair(near_c_p), scalars, g,
                               n_batch=bp, t=tp, s_pad=tp, s_valid=tp, q_off=0, tq=tq_c, tk=tk_c, **dims)
            k_all = _keys_with_past(cache_c_k[j].reshape(bs, past_len, tok_w), hs_f[..., tok_w:2 * tok_w], sp_c)
            v_all = _keys_with_past(cache_c_v[j].reshape(bs, past_len, tok_w), hs_f[..., 2 * tok_w:3 * tok_w], sp_c)
            mix_s = _diff_attn((h_b_, rb_s, 0), (k_all, 0, v_all, 0), pair(near_c_s), scalars, g,
                               n_batch=bs, t=ts, s_pad=sp_c, s_valid=s_new, q_off=past_len,
                               tq=ts, tk=tk_c, **dims)
            mq_cb = 3 * tok_w // mem_w
            rows_p[i] = (hp(tok_w, 2 * tok_w).reshape(bp, tp, hc, 2, hd),
                         hp(2 * tok_w, 3 * tok_w).reshape(bp, tp, hc, 2 * hd))
            rows_s[i] = (hs_f[..., tok_w:2 * tok_w].reshape(bs, ts, hc, 2, hd),
                         hs_f[..., 2 * tok_w:3 * tok_w].reshape(bs, ts, hc, 2 * hd))

        mem_p = _mem_attn(h_b_, 0, mq_cb, mem_b[:, (2 * i) * mem_w:(2 * i + 1) * mem_w],
                          mem_b[:, (2 * i + 1) * mem_w:(2 * i + 2) * mem_w],
                          bp, tp, tq_m, h_mem, hd)
        mem_s = _mem_attn(h_b_, rb_s, mq_cb, mem_s_k[i], mem_s_v[i], bs, ts, ts, h_mem, hd)
        mix = jnp.concatenate([mix_p, mix_s], axis=0)
        mem_o = jnp.concatenate([mem_p, mem_s], axis=0)
        x_f, x_pk = _proj_ln(mix, mem_o, w_o_bf[i, :tok_w], w_o_bf[i, tok_w:], x_f,
                             ln_g[i, 0][None], ln_b[i, 0][None], alpha)
        y, dest, gates, rows_buf = _moe(x_f, x_pk, rows_buf, i, w_r_hi[i], w_r_lo[i], b_r_pad[i],
                                        w_gate_up, b_gu, w_down, b_dn)
        x_f, x_b = _combine_ln(x_f, y, dest, gates, ln_g[i, 1][None], ln_b[i, 1][None], alpha)

    def stack_kind(rows, kind):
        sel = [r for i, r in enumerate(rows) if i % N_MIXERS == kind]
        return [jnp.stack(parts) for parts in zip(*sel)]

    y_prompt = x_f[:mp].reshape(bp, tp, d)
    y_sample = x_f[mp:].reshape(bs, ts, d)
    a_k_p, a_v_p, a_kidx_p = stack_kind(rows_p, 0)
    b_k_p, b_v_p = stack_kind(rows_p, 1)
    c_k_p, c_v_p = stack_kind(rows_p, 2)
    a_k_s, a_v_s, a_kidx_s = stack_kind(rows_s, 0)
    b_k_s, b_v_s = stack_kind(rows_s, 1)
    c_k_s, c_v_s = stack_kind(rows_s, 2)
    return (y_prompt, y_sample, a_k_p, a_v_p, a_kidx_p, b_k_p, b_v_p, c_k_p, c_v_p, mem_k_p, mem_v_p,
            a_k_s, a_v_s, a_kidx_s, b_k_s, b_v_s, c_k_s, c_v_s)
```

```python
import functools
import math

import jax
import jax.numpy as jnp
from jax import lax
from jax.experimental import pallas as pl
from jax.experimental.pallas import tpu as pltpu

F32 = jnp.float32
BF16 = jnp.bfloat16
I32 = jnp.int32

CHUNK = 64
CHUNK_SHIFT = CHUNK.bit_length() - 1
K_SEL_MAX = 256
MAX_DISTANCE = 128
TOP_K = 4
SWIGLU_LIMIT = 7.0
SWIGLU_ALPHA = 1.702
LN_EPS = 1e-5
N_MIXERS = 3

LANE = 128
VMEM_LIMIT = 56 * 1024 * 1024
MASKED = -1e30
EXP_UNDERFLOW = -104.0
LOG2E = 1.4426950408889634
INT_MIN = -2 ** 31
KEY_NEG_INF = -2139095041
HI16 = -65536

def _cparams(n_axes):
    return pltpu.CompilerParams(dimension_semantics=("arbitrary",) * n_axes,
                                vmem_limit_bytes=VMEM_LIMIT)


def _round_up(x, m):
    return -(-x // m) * m


def _pick_tile(n, candidates):
    for c in candidates:
        if n % c == 0:
            return c
    return n


def _dot_nt(a, b):
    return lax.dot_general(a, b, (((1,), (1,)), ((), ())), preferred_element_type=F32)


def _resident(block_shape, index_map):
    return pl.BlockSpec(block_shape, index_map, pipeline_mode=pl.Buffered(1))


def _mm_kernel(x_ref, w_ref, of_ref, ob_ref):
    acc = jnp.dot(x_ref[...], w_ref[...], preferred_element_type=F32)
    of_ref[...] = acc
    ob_ref[...] = acc.astype(BF16)


def _matmul(x, w):
    m, k = x.shape
    n = w.shape[1]
    tm = _pick_tile(m, (640, 512, 256, 128, 64, 32, 16))
    tn = _pick_tile(n, (1280, 1024, 768, 512, 384, 256, 128))
    return pl.pallas_call(
        _mm_kernel,
        grid=(n // tn, m // tm),
        in_specs=[pl.BlockSpec((tm, k), lambda j, i: (i, 0)),
                  pl.BlockSpec((k, tn), lambda j, i: (0, j))],
        out_specs=[pl.BlockSpec((tm, tn), lambda j, i: (i, j)),
                   pl.BlockSpec((tm, tn), lambda j, i: (i, j))],
        out_shape=[jax.ShapeDtypeStruct((m, n), F32), jax.ShapeDtypeStruct((m, n), BF16)],
        compiler_params=_cparams(2),
        name="matmul",
    )(x, w)


def _mm_bf16_kernel(x_ref, w_ref, ob_ref):
    ob_ref[...] = jnp.dot(x_ref[...], w_ref[...], preferred_element_type=F32).astype(BF16)


def _matmul_bf16(x, w):
    m, k = x.shape
    n = w.shape[1]
    tm = _pick_tile(m, (640, 512, 256, 128, 64, 32, 16))
    tn = _pick_tile(n, (1280, 1024, 896, 768, 512, 384, 256, 128))
    return pl.pallas_call(
        _mm_bf16_kernel,
        grid=(n // tn, m // tm),
        in_specs=[pl.BlockSpec((tm, k), lambda j, i: (i, 0)),
                  pl.BlockSpec((k, tn), lambda j, i: (0, j))],
        out_specs=pl.BlockSpec((tm, tn), lambda j, i: (i, j)),
        out_shape=jax.ShapeDtypeStruct((m, n), BF16),
        compiler_params=_cparams(2),
        name="matmul_bf16",
    )(x, w)


def _proj_heads_kernel(x_ref, w_ref, of_ref, ob_ref, *, head_shape):
    acc = jnp.dot(x_ref[...], w_ref[...], preferred_element_type=F32)
    ob_ref[...] = acc.astype(BF16)
    slab = head_shape[-1]
    lead = head_shape[:-1]
    for c in range(math.prod(lead)):
        idx, rest = [], c
        for size in reversed(lead):
            idx.append(rest % size)
            rest //= size
        of_ref[(slice(None), *reversed(idx), slice(None))] = acc[:, c * slab:(c + 1) * slab]


def _proj_heads(x, row0, n_rows, w, head_shape):
    k = x.shape[1]
    n = w.shape[1]
    assert math.prod(head_shape) == n and head_shape[-1] % LANE == 0
    tm = _pick_tile(n_rows, (256, 128, 64, 32, 16))
    assert row0 % tm == 0
    zeros = (0,) * len(head_shape)
    return pl.pallas_call(
        functools.partial(_proj_heads_kernel, head_shape=tuple(head_shape)),
        grid=(n_rows // tm,),
        in_specs=[pl.BlockSpec((tm, k), lambda i: (row0 // tm + i, 0)),
                  pl.BlockSpec((k, n), lambda i: (0, 0))],
        out_specs=[pl.BlockSpec((tm, *head_shape), lambda i: (i, *zeros)),
                   pl.BlockSpec((tm, n), lambda i: (i, 0))],
        out_shape=[jax.ShapeDtypeStruct((n_rows, *head_shape), F32),
                   jax.ShapeDtypeStruct((n_rows, n), BF16)],
        compiler_params=_cparams(1),
        name="proj_heads",
    )(x, w)


def _layer_norm_rows(z, g, b):
    mu = jnp.mean(z, axis=-1, keepdims=True)
    zc = z - mu
    var = jnp.mean(zc * zc, axis=-1, keepdims=True)
    return zc * lax.rsqrt(var + LN_EPS) * g + b


def _pack_halves(x):
    half = x.shape[1] // 2
    xb = x.astype(BF16).astype(F32)
    lo = lax.shift_right_logical(pltpu.bitcast(xb[:, :half], I32), 16)
    return lo | (pltpu.bitcast(xb[:, half:], I32) & HI16)


def _unpack_halves(p):
    lo = pltpu.bitcast(p << 16, F32).astype(BF16)
    hi = pltpu.bitcast(p & HI16, F32).astype(BF16)
    return jnp.concatenate([lo, hi], axis=1)


def _proj_ln_kernel(a1_ref, a2_ref, w1_ref, w2_ref, r_ref, g_ref, b_ref, of_ref, op_ref, *, alpha):
    y = jnp.dot(a1_ref[...], w1_ref[...], preferred_element_type=F32)
    y = y + jnp.dot(a2_ref[...], w2_ref[...], preferred_element_type=F32)
    out = _layer_norm_rows(alpha * r_ref[...] + y, g_ref[...], b_ref[...])
    of_ref[...] = out
    op_ref[...] = _pack_halves(out)


def _proj_ln(a1, a2, w1, w2, res, g, b, alpha):
    m, k1 = a1.shape
    k2 = a2.shape[1]
    d = w1.shape[1]
    tm = _pick_tile(m, (256, 128, 64, 32, 16))
    row = lambda i: (i, 0)
    fix = lambda i: (0, 0)
    return pl.pallas_call(
        functools.partial(_proj_ln_kernel, alpha=alpha),
        grid=(m // tm,),
        in_specs=[pl.BlockSpec((tm, k1), row), pl.BlockSpec((tm, k2), row),
                  pl.BlockSpec((k1, d), fix), pl.BlockSpec((k2, d), fix),
                  pl.BlockSpec((tm, d), row), pl.BlockSpec((1, d), fix), pl.BlockSpec((1, d), fix)],
        out_specs=[pl.BlockSpec((tm, d), row), pl.BlockSpec((tm, d // 2), row)],
        out_shape=[jax.ShapeDtypeStruct((m, d), F32), jax.ShapeDtypeStruct((m, d // 2), I32)],
        compiler_params=_cparams(1),
        name="proj_ln",
    )(a1, a2, w1, w2, res, g, b)


def _row_copy(src_ref, src_row, dst_ref, dst_row, sem):
    return pltpu.make_async_copy(src_ref.at[pl.ds(src_row, 1)], dst_ref.at[pl.ds(dst_row, 1)], sem)


def _combine_ln_kernel(dest_ref, next_ref, x_ref, gate_ref, g_ref, b_ref, y_hbm, of_ref, ob_ref,
                       ybuf, sems, *, alpha, tm, n_steps):
    i = pl.program_id(0)
    slot = i % 2

    def gather(idx_ref, to_slot):
        def body(r, carry):
            for k in range(TOP_K):
                _row_copy(y_hbm, idx_ref[0, r * TOP_K + k], ybuf.at[to_slot, k], r,
                          sems.at[to_slot]).start()
            return carry
        lax.fori_loop(0, tm, body, 0)

    @pl.when(i == 0)
    def _():
        gather(dest_ref, 0)

    @pl.when(i + 1 < n_steps)
    def _():
        gather(next_ref, 1 - slot)

    def drain(r, carry):
        for k in range(TOP_K):
            _row_copy(y_hbm, 0, ybuf.at[slot, k], 0, sems.at[slot]).wait()
        return carry
    lax.fori_loop(0, tm, drain, 0)

    gates = gate_ref[...]
    ffn = gates[:, 0:1] * ybuf[slot, 0]
    for k in range(1, TOP_K):
        ffn = ffn + gates[:, k:k + 1] * ybuf[slot, k]
    out = _layer_norm_rows(alpha * x_ref[...] + ffn, g_ref[...], b_ref[...])
    of_ref[...] = out
    ob_ref[...] = out.astype(BF16)


def _combine_ln(x, y, dest, gates, g, b, alpha):
    m, d = x.shape
    tm = _pick_tile(m, (64, 32, 16))
    n_steps = m // tm
    dest2 = dest.reshape(n_steps, 1, tm * TOP_K)
    row = lambda i: (i, 0)
    fix = lambda i: (0, 0)
    return pl.pallas_call(
        functools.partial(_combine_ln_kernel, alpha=alpha, tm=tm, n_steps=n_steps),
        grid=(n_steps,),
        in_specs=[pl.BlockSpec((None, 1, tm * TOP_K), lambda i: (i, 0, 0), memory_space=pltpu.SMEM),
                  pl.BlockSpec((None, 1, tm * TOP_K), lambda i: (jnp.minimum(i + 1, n_steps - 1), 0, 0),
                               memory_space=pltpu.SMEM),
                  pl.BlockSpec((tm, d), row), pl.BlockSpec((tm, LANE), row),
                  pl.BlockSpec((1, d), fix), pl.BlockSpec((1, d), fix),
                  pl.BlockSpec(memory_space=pl.ANY)],
        out_specs=[pl.BlockSpec((tm, d), row), pl.BlockSpec((tm, d), row)],
        out_shape=[jax.ShapeDtypeStruct((m, d), F32), jax.ShapeDtypeStruct((m, d), BF16)],
        scratch_shapes=[pltpu.VMEM((2, TOP_K, tm, d), F32), pltpu.SemaphoreType.DMA((2,))],
        compiler_params=_cparams(1),
        name="combine_ln",
    )(dest2, dest2, x, gates, g, b, y)


def _mem_attn_kernel(q_ref, k_ref, v_ref, o_ref, *, n_heads, hd):
    scale = hd ** -0.5
    for h in range(n_heads):
        cols = slice(h * hd, (h + 1) * hd)
        s = _dot_nt(q_ref[:, cols], k_ref[:, cols]) * scale
        p = jnp.exp(s - jnp.max(s, axis=-1, keepdims=True))
        p = p / jnp.sum(p, axis=-1, keepdims=True)
        o_ref[:, cols] = jnp.dot(p.astype(BF16), v_ref[:, cols],
                                 preferred_element_type=F32).astype(BF16)


def _mem_attn(qarr, q_rb0, q_cb, mk, mv, n_batch, t, tq, n_heads, hd):
    w = n_heads * hd
    n_mem = mk.shape[0] // n_batch
    nqb = t // tq
    return pl.pallas_call(
        functools.partial(_mem_attn_kernel, n_heads=n_heads, hd=hd),
        grid=(n_batch, nqb),
        in_specs=[pl.BlockSpec((tq, w), lambda b, i: (q_rb0 + b * nqb + i, q_cb)),
                  pl.BlockSpec((n_mem, w), lambda b, i: (b, 0)),
                  pl.BlockSpec((n_mem, w), lambda b, i: (b, 0))],
        out_specs=pl.BlockSpec((tq, w), lambda b, i: (b * nqb + i, 0)),
        out_shape=jax.ShapeDtypeStruct((n_batch * t, w), BF16),
        compiler_params=_cparams(2),
        name="mem_attn",
    )(qarr, mk, mv)


def _t5_bucket(rel, num_buckets):
    half = num_buckets // 2
    exact = half // 2
    n = jnp.abs(rel)
    far = exact + (jnp.log(jnp.maximum(n, 1).astype(F32) / exact)
                   / math.log(MAX_DISTANCE / exact) * (half - exact)).astype(I32)
    return jnp.where(rel > 0, half, 0) + jnp.where(n < exact, n, jnp.minimum(far, half - 1))


def _near_phase(q_off, tq, tk, nqb, i):
    full = tk // tq
    return (q_off // tq + i) % full if nqb >= full else i


def _near_bias(rel_bias, tq, tk, q_off, nqb):
    full = tk // tq
    n_maps = rel_bias.shape[1]
    rel_lo = -(2 * tk + tq)
    rels = jnp.arange(rel_lo, tk + 1, dtype=I32)
    by_rel = jnp.moveaxis(rel_bias[_t5_bucket(rels, rel_bias.shape[0])], -1, 0).astype(F32)
    n = tq + tk
    tiles = []
    for p in range(min(full, nqb)):
        q_in_block = (p if nqb >= full else (q_off // tq + p) % full) * tq
        per_near = []
        for near in range(2):
            rel0 = -near * tk - q_in_block
            start = rel0 - (tq - 1) - rel_lo
            v = by_rel[:, start:start + n]
            skew = jnp.tile(v, (1, tq + 1))[:, :tq * (n + 1)].reshape(n_maps, tq, n + 1)
            per_near.append(skew[:, ::-1, :tk])
        tiles.append(jnp.stack(per_near, axis=1))
    return jnp.stack(tiles)


def _far_bias(rel_bias):
    return rel_bias[rel_bias.shape[0] // 2 - 1].astype(F32)


def _dsa_kernel(far_ref, q_ref, qi_ref, qw_ref, k_ref, v_ref, ki_ref, bd_ref, o_ref,
                key_scr, wb_scr, m_scr, acc_scr, *,
                tq, tk, q_off, s_valid, k_sel, n_heads, n_kv, hd, n_idx, d_idx, idx_bits):
    gqa = n_heads // n_kv
    reps = tk // LANE
    i = pl.program_id(1)
    q0 = q_off + i * tq
    kb_diag = q0 // tk
    nkb = kb_diag + 1
    q_pos = q0 + lax.broadcasted_iota(I32, (tq, 1), 0)
    lane = lax.broadcasted_iota(I32, (1, tk), 1)
    lane1 = lax.broadcasted_iota(I32, (1, LANE), 1)
    tile = lambda x: jnp.concatenate([x] * reps, axis=1)

    qi = qi_ref[...]
    qw = qw_ref[...].astype(F32)
    qi_heads = [qi[:, h * d_idx:(h + 1) * d_idx] for h in range(n_idx)]
    for h in range(n_idx):
        wb_scr[h] = jnp.broadcast_to(qw[:, d_idx + h:d_idx + h + 1] * (d_idx ** -0.5), (tq, LANE))

    def score_block(kb, carry):
        ks = pl.multiple_of(kb * tk, tk)
        ki = ki_ref[pl.ds(ks, tk), :][:, :d_idx]
        sc = jnp.zeros((tq, tk), F32)
        for h in range(n_idx):
            sc = sc + jnp.maximum(_dot_nt(qi_heads[h], ki), 0.0) * tile(wb_scr[h])
        sc = sc * (n_idx ** -0.5)
        sc = jnp.where(sc == 0.0, 0.0, sc)
        k_pos = ks + lane
        allowed = ((k_pos >> CHUNK_SHIFT) <= (q_pos >> CHUNK_SHIFT)) & (k_pos < s_valid)
        sc = jnp.where(allowed, sc, -jnp.inf)
        bits = pltpu.bitcast(sc, I32)
        key_scr[kb] = bits ^ ((bits >> 31) & 0x7FFFFFFF)
        return carry

    lax.fori_loop(0, nkb, score_block, 0)

    def count(pred):
        def body(kb, acc):
            key = key_scr[kb]
            for c in range(reps):
                acc = acc + jnp.where(pred(kb, c, key[:, c * LANE:(c + 1) * LANE]), 1.0, 0.0)
            return acc
        acc = lax.fori_loop(0, nkb, body, jnp.zeros((tq, LANE), F32))
        return jnp.sum(acc, axis=1, keepdims=True)

    def bisect_cond(state):
        it, _, _, open_rows = state
        return (it < 32) & (open_rows > 0.0)

    def bisect_body(state):
        it, t_u, cnt_t, _ = state
        cand_u = t_u | (jnp.int32(1) << (31 - it))
        cand_b = jnp.broadcast_to(cand_u ^ INT_MIN, (tq, LANE))
        cnt = count(lambda kb, c, key: key >= cand_b)
        take = cnt >= k_sel
        t_u = jnp.where(take, cand_u, t_u)
        cnt_t = jnp.where(take, cnt, cnt_t)
        return it + 1, t_u, cnt_t, jnp.max(jnp.where(cnt_t > k_sel, 1.0, 0.0))

    total = (nkb * tk).astype(F32)
    _, t_u, cnt_t, _ = lax.while_loop(
        bisect_cond, bisect_body,
        (jnp.int32(0), jnp.zeros((tq, 1), I32), jnp.full((tq, 1), total, F32),
         jnp.where(total > k_sel, 1.0, 0.0)))
    thr = t_u ^ INT_MIN
    thr_b = jnp.broadcast_to(thr, (tq, LANE))

    def tie_search():
        need = k_sel - count(lambda kb, c, key: key > thr_b)

        def index_bit(it, j_hi):
            cand = j_hi | (jnp.int32(1) << (idx_bits - 1 - it))
            cand_b = jnp.broadcast_to(cand, (tq, LANE))
            cnt = count(lambda kb, c, key: (key == thr_b) & (kb * tk + c * LANE + lane1 < cand_b))
            return jnp.where(cnt < need, cand, j_hi)

        return lax.fori_loop(0, idx_bits, index_bit, jnp.zeros((tq, 1), I32))

    tie_hi = lax.cond(jnp.max(cnt_t) > k_sel, tie_search, lambda: jnp.full((tq, 1), 2 ** 30, I32))

    m_scr[...] = jnp.full(m_scr.shape, MASKED, F32)
    acc_scr[...] = jnp.zeros(acc_scr.shape, F32)
    scale2 = (hd ** -0.5) * LOG2E
    q_groups = [jnp.concatenate([q_ref[:, (g * gqa + r) * hd:(g * gqa + r + 1) * hd]
                                 for r in range(gqa)], axis=0) for g in range(n_kv)]
    far_rows = [jnp.concatenate([jnp.full((tq, LANE), far_ref[g * gqa + r], F32)
                                 for r in range(gqa)], axis=0) for g in range(n_kv)]
    ones = jnp.ones((tk, hd), BF16)

    def attend_block(kb, near):
        ks = pl.multiple_of(kb * tk, tk)
        key = key_scr[kb]
        sel = (key > thr) | ((key == thr) & (ks + lane <= tie_hi))
        sel = sel & (key > KEY_NEG_INF)
        mbias = jnp.where(sel, 0.0, MASKED)
        for g in range(n_kv):
            kg = k_ref[pl.ds(ks, tk), g * hd:(g + 1) * hd]
            vext = jnp.concatenate([v_ref[pl.ds(ks, tk), g * hd:(g + 1) * hd], ones], axis=1)
            s = _dot_nt(q_groups[g], kg) * scale2
            m_old = m_scr[g]
            if near is None:
                t = s + jnp.concatenate([mbias] * gqa, axis=0)
                m_new = jnp.maximum(m_old, jnp.max(t, axis=-1, keepdims=True) + far_rows[g])
                p = jnp.exp2(t - tile(m_new - far_rows[g]))
            else:
                t = s + jnp.concatenate([mbias + bd_ref[g * gqa + r, near] for r in range(gqa)], axis=0)
                m_new = jnp.maximum(m_old, jnp.max(t, axis=-1, keepdims=True))
                p = jnp.exp2(t - tile(m_new))
            alpha = jnp.exp2(m_old - m_new)
            acc_scr[g] = (jnp.concatenate([alpha, alpha], axis=1) * acc_scr[g]
                          + jnp.dot(p.astype(BF16), vext, preferred_element_type=F32))
            m_scr[g] = m_new

    def far_block(kb, carry):
        attend_block(kb, None)
        return carry

    lax.fori_loop(0, jnp.maximum(kb_diag - 1, 0), far_block, 0)

    @pl.when(kb_diag >= 1)
    def _():
        attend_block(kb_diag - 1, 1)

    attend_block(kb_diag, 0)

    for g in range(n_kv):
        acc = acc_scr[g]
        o = acc[:, :hd] / acc[:, hd:]
        for r in range(gqa):
            h = g * gqa + r
            o_ref[:, h * hd:(h + 1) * hd] = o[r * tq:(r + 1) * tq].astype(BF16)


def _dsa_attn(q_src, k_src, bias_near, bias_far, *, n_batch, t, s_pad, s_valid, q_off, tq, tk,
              k_sel, n_heads, n_kv, hd, n_idx, d_idx):
    qarr, q_rb0, q_cb, qi_cb, qw_cb = q_src
    karr, k_cb, varr, v_cb, kiarr, ki_cb = k_src
    assert tk >= MAX_DISTANCE and tk % tq == 0 and q_off % tq == 0 and s_pad % tk == 0
    nqb = t // tq
    nkb_max = s_pad // tk
    gqa = n_heads // n_kv
    wq, wkv, wqi = n_heads * hd, n_kv * hd, n_idx * d_idx
    qrow = lambda b, i: q_rb0 + b * nqb + i
    kernel = functools.partial(
        _dsa_kernel, tq=tq, tk=tk, q_off=q_off, s_valid=s_valid, k_sel=k_sel, n_heads=n_heads,
        n_kv=n_kv, hd=hd, n_idx=n_idx, d_idx=d_idx, idx_bits=max((s_pad - 1).bit_length(), 1))
    return pl.pallas_call(
        kernel,
        grid=(n_batch, nqb),
        in_specs=[pl.BlockSpec(memory_space=pltpu.SMEM),
                  pl.BlockSpec((tq, wq), lambda b, i: (qrow(b, i), q_cb)),
                  pl.BlockSpec((tq, wqi), lambda b, i: (qrow(b, i), qi_cb)),
                  pl.BlockSpec((tq, LANE), lambda b, i: (qrow(b, i), qw_cb)),
                  _resident((s_pad, wkv), lambda b, i: (b, k_cb)),
                  _resident((s_pad, wkv), lambda b, i: (b, v_cb)),
                  _resident((s_pad, LANE), lambda b, i: (b, ki_cb)),
                  pl.BlockSpec((None, n_heads, 2, tq, tk),
                               lambda b, i: (_near_phase(q_off, tq, tk, nqb, i), 0, 0, 0, 0))],
        out_specs=pl.BlockSpec((tq, wq), lambda b, i: (b * nqb + i, 0)),
        out_shape=jax.ShapeDtypeStruct((n_batch * t, wq), BF16),
        scratch_shapes=[pltpu.VMEM((nkb_max, tq, tk), I32),
                        pltpu.VMEM((n_idx, tq, LANE), F32),
                        pltpu.VMEM((n_kv, gqa * tq, LANE), F32),
                        pltpu.VMEM((n_kv, gqa * tq, 2 * hd), F32)],
        compiler_params=_cparams(2),
        name="dsa_attn",
    )(bias_far, qarr, qarr, qarr, karr, varr, kiarr, bias_near)


def _stick_kernel(q_ref, k_ref, v_ref, o_ref, carry_scr, acc_scr, *, tq, tk, q_off, hd):
    i = pl.program_id(2)
    q0 = q_off + i * tq
    q_pos = q0 + lax.broadcasted_iota(I32, (tq, 1), 0)
    lane = lax.broadcasted_iota(I32, (1, tk), 1)
    tri = (lax.broadcasted_iota(I32, (tk, tk), 0) > lax.broadcasted_iota(I32, (tk, tk), 1)).astype(BF16)
    q = q_ref[...]
    scale = hd ** -0.5
    carry_scr[...] = jnp.zeros(carry_scr.shape, F32)
    acc_scr[...] = jnp.zeros(acc_scr.shape, F32)

    def cond(state):
        kb, carry_max = state
        return (kb >= 0) & (carry_max > EXP_UNDERFLOW)

    def body(state):
        kb, _ = state
        ks = pl.multiple_of(kb * tk, tk)
        z = _dot_nt(q, k_ref[pl.ds(ks, tk), :]) * scale
        before = (ks + lane) < q_pos
        log_beta = jnp.minimum(z, 0.0) - jnp.log(1.0 + jnp.exp(-jnp.abs(z)))
        log_keep = jnp.where(before, log_beta - z, 0.0)
        hi = log_keep.astype(BF16)
        lo = (log_keep - hi.astype(F32)).astype(BF16)
        tail = (jnp.dot(hi, tri, preferred_element_type=F32)
                + jnp.dot(lo, tri, preferred_element_type=F32))
        carry = carry_scr[...]
        w = jnp.where(before, jnp.exp(log_beta + tail + carry), 0.0)
        acc_scr[...] += jnp.dot(w.astype(BF16), v_ref[pl.ds(ks, tk), :], preferred_element_type=F32)
        carry = carry + jnp.sum(log_keep, axis=-1, keepdims=True)
        carry_scr[...] = carry
        return kb - 1, jnp.max(carry)

    lax.while_loop(cond, body, ((q0 + tq - 1) // tk, jnp.float32(0.0)))
    o_ref[...] = acc_scr[...].astype(BF16)


def _stick_attn(q_src, k_src, *, n_batch, t, s_pad, q_off, tq, tk, n_heads, hd):
    qarr, q_rb0, q_cb0 = q_src
    karr, k_cb0, varr, v_cb0 = k_src
    nqb = t // tq
    return pl.pallas_call(
        functools.partial(_stick_kernel, tq=tq, tk=tk, q_off=q_off, hd=hd),
        grid=(n_batch, n_heads, nqb),
        in_specs=[pl.BlockSpec((tq, hd), lambda b, h, i: (q_rb0 + b * nqb + i, q_cb0 + h)),
                  pl.BlockSpec((s_pad, hd), lambda b, h, i: (b, k_cb0 + h)),
                  pl.BlockSpec((s_pad, hd), lambda b, h, i: (b, v_cb0 + h))],
        out_specs=pl.BlockSpec((tq, hd), lambda b, h, i: (b * nqb + i, h)),
        out_shape=jax.ShapeDtypeStruct((n_batch * t, n_heads * hd), BF16),
        scratch_shapes=[pltpu.VMEM((tq, 1), F32), pltpu.VMEM((tq, hd), F32)],
        compiler_params=_cparams(3),
        name="stick_attn",
    )(qarr, karr, varr)


def _diff_kernel(scal_ref, q_ref, k_ref, v_ref, bd_ref, g_ref, o_ref, m_scr, acc_scr, *,
                 tq, tk, q_off, s_valid, hd, out_scale):
    h = pl.program_id(1)
    i = pl.program_id(2)
    reps = tk // LANE
    q0 = q_off + i * tq
    kb_diag = q0 // tk
    q_pos = q0 + lax.broadcasted_iota(I32, (tq, 1), 0)
    lane = lax.broadcasted_iota(I32, (1, tk), 1)
    tile = lambda x: jnp.concatenate([x] * reps, axis=1)
    scale2 = (hd ** -0.5) * LOG2E
    m_scr[...] = jnp.full(m_scr.shape, MASKED, F32)
    acc_scr[...] = jnp.zeros(acc_scr.shape, F32)
    q_maps = [q_ref[:, m * hd:(m + 1) * hd] for m in range(2)]
    ones = jnp.ones((tk, hd), BF16)

    def attend_block(kb, near):
        ks = pl.multiple_of(kb * tk, tk)
        vext = jnp.concatenate([v_ref[pl.ds(ks, tk), :], ones], axis=1)
        if near is not None:
            k_pos = ks + lane
            allowed = ((k_pos >> CHUNK_SHIFT) <= (q_pos >> CHUNK_SHIFT)) & (k_pos < s_valid)
            mbias = jnp.where(allowed, 0.0, MASKED)
        for m in range(2):
            s = _dot_nt(q_maps[m], k_ref[pl.ds(ks, tk), m * hd:(m + 1) * hd]) * scale2
            m_old = m_scr[m]
            if near is None:
                far = scal_ref[1 + 2 * h + m]
                m_new = jnp.maximum(m_old, jnp.max(s, axis=-1, keepdims=True) + far)
                p = jnp.exp2(s - tile(m_new - far))
            else:
                t = s + (mbias + bd_ref[m, near])
                m_new = jnp.maximum(m_old, jnp.max(t, axis=-1, keepdims=True))
                p = jnp.exp2(t - tile(m_new))
            alpha = jnp.exp2(m_old - m_new)
            acc_scr[m] = (jnp.concatenate([alpha] * 3, axis=1) * acc_scr[m]
                          + jnp.dot(p.astype(BF16), vext, preferred_element_type=F32))
            m_scr[m] = m_new

    n_far = jnp.maximum(kb_diag - 1, 0)

    def far_pair(pair, carry):
        attend_block(2 * pair, None)
        attend_block(2 * pair + 1, None)
        return carry

    lax.fori_loop(0, n_far // 2, far_pair, 0)

    @pl.when(n_far % 2 == 1)
    def _():
        attend_block(n_far - 1, None)

    @pl.when(kb_diag >= 1)
    def _():
        attend_block(kb_diag - 1, 1)

    attend_block(kb_diag, 0)

    def normalised(m):
        acc = acc_scr[m]
        return acc[:, :2 * hd] / jnp.concatenate([acc[:, 2 * hd:]] * 2, axis=1)

    o = normalised(0) - scal_ref[0] * normalised(1)
    o = o * lax.rsqrt(jnp.mean(o * o, axis=-1, keepdims=True) + LN_EPS) * g_ref[...]
    o_ref[...] = (o * out_scale).astype(BF16)


def _diff_attn(q_src, k_src, bias_near, scalars, subln_g, *, n_batch, t, s_pad, s_valid, q_off,
               tq, tk, n_heads, hd, out_scale):
    qarr, q_rb0, q_cb0 = q_src
    karr, k_cb0, varr, v_cb0 = k_src
    assert tk >= MAX_DISTANCE and tk % tq == 0 and q_off % tq == 0 and s_pad % tk == 0
    nqb = t // tq
    w = 2 * hd
    return pl.pallas_call(
        functools.partial(_diff_kernel, tq=tq, tk=tk, q_off=q_off, s_valid=s_valid, hd=hd,
                          out_scale=out_scale),
        grid=(n_batch, n_heads, nqb),
        in_specs=[pl.BlockSpec(memory_space=pltpu.SMEM),
                  pl.BlockSpec((tq, w), lambda b, h, i: (q_rb0 + b * nqb + i, q_cb0 + h)),
                  pl.BlockSpec((s_pad, w), lambda b, h, i: (b, k_cb0 + h)),
                  pl.BlockSpec((s_pad, w), lambda b, h, i: (b, v_cb0 + h)),
                  pl.BlockSpec((None, None, 2, 2, tq, tk),
                               lambda b, h, i: (_near_phase(q_off, tq, tk, nqb, i), h, 0, 0, 0, 0)),
                  pl.BlockSpec((1, w), lambda b, h, i: (0, 0))],
        out_specs=pl.BlockSpec((tq, w), lambda b, h, i: (b * nqb + i, h)),
        out_shape=jax.ShapeDtypeStruct((n_batch * t, n_heads * w), BF16),
        scratch_shapes=[pltpu.VMEM((2, tq, LANE), F32), pltpu.VMEM((2, tq, w + hd), F32)],
        compiler_params=_cparams(3),
        name="diff_attn",
    )(scalars, qarr, karr, varr, bias_near, subln_g)


def _router_kernel(x_ref, wh_ref, wl_ref, b_ref, e_ref, g_ref, r_ref, c_ref, seen_scr):
    @pl.when(pl.program_id(0) == 0)
    def _():
        seen_scr[...] = jnp.zeros(seen_scr.shape, F32)

    x = x_ref[...]
    xh = x.astype(BF16)
    xl = (x - xh.astype(F32)).astype(BF16)
    wh = wh_ref[...]
    logits = (jnp.dot(xh, wh, preferred_element_type=F32)
              + jnp.dot(xl, wh, preferred_element_type=F32)
              + jnp.dot(xh, wl_ref[...], preferred_element_type=F32)) + b_ref[...]
    tm = logits.shape[0]
    lane = lax.broadcasted_iota(I32, logits.shape, 1)
    lane_f = lane.astype(F32)
    experts = jnp.zeros(logits.shape, I32)
    gates = jnp.zeros(logits.shape, F32)
    chosen = []
    top = None
    for k in range(TOP_K):
        m = jnp.max(logits, axis=-1, keepdims=True)
        ix = jnp.min(jnp.where(logits == m, lane_f, float(LANE)), axis=-1, keepdims=True).astype(I32)
        top = m if top is None else top
        experts = jnp.where(lane == k, ix, experts)
        gates = jnp.where(lane == k, jnp.exp(m - top), gates)
        chosen.append(lane == ix)
        logits = jnp.where(chosen[-1], -jnp.inf, logits)
    e_ref[...] = experts
    g_ref[...] = gates / jnp.sum(gates, axis=-1, keepdims=True)

    hot = functools.reduce(jnp.logical_or, chosen)
    hot_f = jnp.where(hot, 1.0, 0.0)
    lower = (lax.broadcasted_iota(I32, (tm, tm), 0) > lax.broadcasted_iota(I32, (tm, tm), 1)).astype(BF16)
    earlier = jnp.dot(lower, hot_f.astype(BF16), preferred_element_type=F32) + seen_scr[...]
    ranks = jnp.zeros(logits.shape, F32)
    for k in range(TOP_K):
        rank_k = jnp.sum(jnp.where(chosen[k], earlier, 0.0), axis=-1, keepdims=True)
        ranks = jnp.where(lane == k, rank_k, ranks)
    r_ref[...] = ranks.astype(I32)
    seen = seen_scr[...] + jnp.sum(hot_f, axis=0, keepdims=True)
    seen_scr[...] = seen
    c_ref[...] = seen


def _router(x, w_hi, w_lo, bias):
    m, d = x.shape
    tm = _pick_tile(m, (256, 128, 64, 32, 16))
    row = lambda i: (i, 0)
    fix = lambda i: (0, 0)
    return pl.pallas_call(
        _router_kernel,
        grid=(m // tm,),
        in_specs=[pl.BlockSpec((tm, d), row), pl.BlockSpec((d, LANE), fix),
                  pl.BlockSpec((d, LANE), fix), pl.BlockSpec((1, LANE), fix)],
        out_specs=[pl.BlockSpec((tm, LANE), row), pl.BlockSpec((tm, LANE), row),
                   pl.BlockSpec((tm, LANE), row), pl.BlockSpec((1, LANE), fix)],
        out_shape=[jax.ShapeDtypeStruct((m, LANE), I32), jax.ShapeDtypeStruct((m, LANE), F32),
                   jax.ShapeDtypeStruct((m, LANE), I32), jax.ShapeDtypeStruct((1, LANE), F32)],
        scratch_shapes=[pltpu.VMEM((1, LANE), F32)],
        compiler_params=_cparams(1),
        name="router",
    )(x, w_hi, w_lo, bias)


def _dispatch_kernel(dest_ref, x_ref, rows_in, rows_out, sem, *, tm):
    del rows_in

    def send(r, carry):
        for k in range(TOP_K):
            _row_copy(x_ref, r, rows_out, dest_ref[0, r * TOP_K + k], sem).start()
        return carry
    lax.fori_loop(0, tm, send, 0)

    def drain(r, carry):
        for k in range(TOP_K):
            _row_copy(x_ref, 0, rows_out, 0, sem).wait()
        return carry
    lax.fori_loop(0, tm, drain, 0)


def _dispatch(x_packed, dest, rows_buf):
    m, w = x_packed.shape
    tm = _pick_tile(m, (256, 128, 64, 32, 16))
    n_steps = m // tm
    return pl.pallas_call(
        functools.partial(_dispatch_kernel, tm=tm),
        grid=(n_steps,),
        in_specs=[pl.BlockSpec((None, 1, tm * TOP_K), lambda i: (i, 0, 0), memory_space=pltpu.SMEM),
                  pl.BlockSpec((tm, w), lambda i: (i, 0)),
                  pl.BlockSpec(memory_space=pl.ANY)],
        out_specs=pl.BlockSpec(memory_space=pl.ANY),
        out_shape=jax.ShapeDtypeStruct(rows_buf.shape, I32),
        scratch_shapes=[pltpu.SemaphoreType.DMA(())],
        input_output_aliases={2: 0},
        compiler_params=_cparams(1),
        name="dispatch",
    )(dest.reshape(n_steps, 1, tm * TOP_K), x_packed, rows_buf)


def _cast_rows(src_ref, dst_ref, rows_per_step):
    def body(c, carry):
        r = pl.multiple_of(c * rows_per_step, rows_per_step)
        dst_ref[pl.ds(r, rows_per_step), :] = src_ref[pl.ds(r, rows_per_step), :].astype(BF16)
        return carry
    lax.fori_loop(0, src_ref.shape[0] // rows_per_step, body, 0)


def _expert_changed(be_ref, j):
    return (j == 0) | (be_ref[j] != be_ref[jnp.maximum(j - 1, 0)])


def _moe_up_kernel(be_ref, nu_ref, x_ref, wg_ref, wl_ref, bg_ref, bl_ref, h_ref, wg_bf, wl_bf, *,
                   cast_rows):
    j = pl.program_id(1)

    @pl.when(_expert_changed(be_ref, j))
    def _():
        _cast_rows(wg_ref, wg_bf, cast_rows)
        _cast_rows(wl_ref, wl_bf, cast_rows)

    @pl.when(j < nu_ref[0])
    def _():
        x = _unpack_halves(x_ref[...])
        glu = jnp.dot(x, wg_bf[...], preferred_element_type=F32) + bg_ref[...]
        lin = jnp.dot(x, wl_bf[...], preferred_element_type=F32) + bl_ref[...]
        glu = jnp.minimum(glu, SWIGLU_LIMIT)
        lin = jnp.clip(lin, -SWIGLU_LIMIT, SWIGLU_LIMIT)
        h_ref[...] = (glu * jax.nn.sigmoid(SWIGLU_ALPHA * glu) * (lin + 1.0)).astype(BF16)

    @pl.when(j >= nu_ref[0])
    def _():
        h_ref[...] = jnp.zeros(h_ref.shape, BF16)


def _moe_down_kernel(be_ref, nu_ref, h_ref, wd_ref, bd_ref, y_ref, wd_bf, *, cast_rows):
    j = pl.program_id(1)

    @pl.when(_expert_changed(be_ref, j))
    def _():
        _cast_rows(wd_ref, wd_bf, cast_rows)

    @pl.when(j < nu_ref[0])
    def _():
        y_ref[...] = jnp.dot(h_ref[...], wd_bf[...], preferred_element_type=F32) + bd_ref[...]

    @pl.when(j >= nu_ref[0])
    def _():
        y_ref[...] = jnp.zeros(y_ref.shape, F32)


def _moe_experts(xs, block_e, n_used, layer, w_gate_up, b_gate_up, w_down, b_down, tm):
    n_rows = xs.shape[0]
    d = 2 * xs.shape[1]
    n_blocks = n_rows // tm
    d_ff = w_down.shape[2]
    tf = _pick_tile(d_ff, (1024, 512, 256, 128))
    nf = d_ff // tf
    tn = _pick_tile(d, (1024, 512, 256, 128))
    blk = lambda j, nu: jnp.minimum(j, nu[0] - 1)
    hidden = pl.pallas_call(
        functools.partial(_moe_up_kernel, cast_rows=_pick_tile(d, (256, 128, 64, 32, 16, 8))),
        grid_spec=pltpu.PrefetchScalarGridSpec(
            num_scalar_prefetch=2,
            grid=(nf, n_blocks),
            in_specs=[pl.BlockSpec((tm, d // 2), lambda f, j, be, nu: (blk(j, nu), 0)),
                      pl.BlockSpec((None, None, d, tf), lambda f, j, be, nu: (layer, be[j], 0, f)),
                      pl.BlockSpec((None, None, d, tf), lambda f, j, be, nu: (layer, be[j], 0, nf + f)),
                      pl.BlockSpec((None, None, 1, tf), lambda f, j, be, nu: (layer, be[j], 0, f)),
                      pl.BlockSpec((None, None, 1, tf), lambda f, j, be, nu: (layer, be[j], 0, nf + f))],
            out_specs=pl.BlockSpec((tm, tf), lambda f, j, be, nu: (j, f)),
            scratch_shapes=[pltpu.VMEM((d, tf), BF16), pltpu.VMEM((d, tf), BF16)]),
        out_shape=jax.ShapeDtypeStruct((n_rows, d_ff), BF16),
        compiler_params=_cparams(2),
        name="moe_up",
    )(block_e, n_used, xs, w_gate_up, w_gate_up, b_gate_up, b_gate_up)
    return pl.pallas_call(
        functools.partial(_moe_down_kernel, cast_rows=_pick_tile(d_ff, (256, 128, 64, 32, 16, 8))),
        grid_spec=pltpu.PrefetchScalarGridSpec(
            num_scalar_prefetch=2,
            grid=(d // tn, n_blocks),
            in_specs=[pl.BlockSpec((tm, d_ff), lambda n, j, be, nu: (blk(j, nu), 0)),
                      pl.BlockSpec((None, None, d_ff, tn), lambda n, j, be, nu: (layer, be[j], 0, n)),
                      pl.BlockSpec((None, None, 1, tn), lambda n, j, be, nu: (layer, be[j], 0, n))],
            out_specs=pl.BlockSpec((tm, tn), lambda n, j, be, nu: (j, n)),
            scratch_shapes=[pltpu.VMEM((d_ff, tn), BF16)]),
        out_shape=jax.ShapeDtypeStruct((n_rows, d), F32),
        compiler_params=_cparams(2),
        name="moe_down",
    )(block_e, n_used, hidden, w_down, b_down)


def _moe(x_f32, x_packed, rows_buf, layer, w_router_hi, w_router_lo, b_router, w_gate_up,
         b_gate_up, w_down, b_down):
    n = x_f32.shape[0]
    n_experts = w_gate_up.shape[1]
    tm = 512 if n * TOP_K >= 1024 * n_experts else 128
    experts, gates, ranks, counts = _router(x_f32, w_router_hi, w_router_lo, b_router)
    counts = counts[0, :n_experts].astype(I32)
    pend = jnp.cumsum((counts + tm - 1) // tm * tm)
    pstart = pend - (counts + tm - 1) // tm * tm
    n_blocks = -(-(n * TOP_K + n_experts * (tm - 1)) // tm)
    block_e = jnp.minimum(jnp.sum(pend[None, :] <= (jnp.arange(n_blocks) * tm)[:, None], axis=1),
                          n_experts - 1).astype(I32)
    n_used = (pend[-1] // tm).astype(I32).reshape(1)
    e4 = experts[:, :TOP_K]
    onehot = e4[..., None] == jnp.arange(n_experts, dtype=I32)
    dest = (ranks[:, :TOP_K] + jnp.sum(jnp.where(onehot, pstart.astype(I32), 0), axis=-1)).reshape(-1)
    if rows_buf is None:
        rows_buf = jnp.zeros((n_blocks * tm,) + x_packed.shape[1:], I32)
    xs = _dispatch(x_packed, dest, rows_buf)
    y = _moe_experts(xs, block_e, n_used, layer, w_gate_up, b_gate_up, w_down, b_down, tm)
    return y, dest, gates, xs


def _pad_cols(w, n):
    return jnp.pad(w, ((0, 0), (0, n - w.shape[1])))


def _keys_with_past(past, new, s_pad):
    b, p, w = past.shape
    rows = jnp.pad(past.astype(BF16), ((0, 0), (0, s_pad - p), (0, 0)))
    rows = lax.dynamic_update_slice(rows, new.astype(BF16), (0, p, 0))
    return rows.reshape(b * s_pad, w)


def kernel(x_prompt, x_sample, cache_a_k, cache_a_v, cache_a_kidx, cache_b_k, cache_b_v,
           cache_c_k, cache_c_v, cache_mem_k, cache_mem_v, mem_prompt,
           w_in_a, w_in_b, w_in_c, diff_lambda, diff_subln, rel_bias, w_mem_kv, w_o,
           ln_g, ln_b, w_router, b_router, w_gate_up, b_gate_up, w_down, b_down):
    bp, tp, d = x_prompt.shape
    bs, ts, _ = x_sample.shape
    depth = w_o.shape[0]
    past_len = cache_a_k.shape[2]
    n_kv, hd = cache_a_k.shape[3], cache_a_k.shape[4]
    d_idx = cache_a_kidx.shape[3]
    n_mem, h_mem = cache_mem_k.shape[2], cache_mem_k.shape[3]
    mem_w = h_mem * hd
    tok_w = d - mem_w
    h_a = tok_w // hd
    h_b = cache_b_k.shape[3]
    hc = cache_c_k.shape[3]
    n_idx = (w_in_a.shape[2] - tok_w - 2 * n_kv * hd - d_idx - mem_w) // (d_idx + 1)
    n_experts = w_router.shape[2]
    alpha = (2 * depth) ** 0.25
    mp, ms = bp * tp, bs * ts
    s_new = past_len + ts
    kv_w, qi_w = n_kv * hd, n_idx * d_idx

    o_k, o_v, o_qi = tok_w, tok_w + kv_w, tok_w + 2 * kv_w
    o_ki = o_qi + qi_w
    o_wi = o_ki + d_idx
    o_mq = o_wi + n_idx
    a_qi = tok_w
    a_mq = a_qi + qi_w
    a_kw = a_mq + mem_w
    bf = lambda w: w.astype(BF16)
    w_a = [bf(_pad_cols(jnp.concatenate([w[:, :o_k], w[:, o_qi:o_ki], w[:, o_mq:], w[:, o_ki:o_mq]],
                                        axis=1), _round_up(a_kw + LANE, 7 * LANE if d >= 7 * LANE else LANE)))
           for w in w_in_a]
    w_a_k = [bf(w[:, o_k:o_v]) for w in w_in_a]
    w_a_v = [bf(w[:, o_v:o_qi]) for w in w_in_a]
    w_ki = [bf(w[:, o_ki:o_wi]) for w in w_in_a]
    main_bc = lambda w: bf(jnp.concatenate([w[:, :tok_w], w[:, 3 * tok_w:]], axis=1))
    w_b, w_c = [main_bc(w) for w in w_in_b], [main_bc(w) for w in w_in_c]
    w_b_k, w_b_v = [bf(w[:, tok_w:2 * tok_w]) for w in w_in_b], [bf(w[:, 2 * tok_w:3 * tok_w]) for w in w_in_b]
    w_c_k, w_c_v = [bf(w[:, tok_w:2 * tok_w]) for w in w_in_c], [bf(w[:, 2 * tok_w:3 * tok_w]) for w in w_in_c]
    w_o_bf = w_o.astype(BF16)
    w_r_pad = jnp.pad(w_router, ((0, 0), (0, 0), (0, LANE - n_experts)))
    w_r_hi = w_r_pad.astype(BF16)
    w_r_lo = (w_r_pad - w_r_hi.astype(F32)).astype(BF16)
    b_r_pad = jnp.pad(b_router, ((0, 0), (0, LANE - n_experts)), constant_values=MASKED)[:, None, :]
    b_gu = b_gate_up[:, :, None, :]
    b_dn = b_down[:, :, None, :]

    tq_a = _pick_tile(tp, (128, 64))
    tk_a = _pick_tile(tp, (512, 256, 128))
    tq_b = _pick_tile(tp, (256, 128, 64))
    tk_b = min(tq_b, 256)
    tq_c = _pick_tile(tp, (512, 256, 128))
    tk_c = _pick_tile(tp, (512, 256, 128))
    tq_m = _pick_tile(tp, (512, 256, 128))
    sp_a, sp_b, sp_c = _round_up(s_new, tk_a), _round_up(s_new, tk_b), _round_up(s_new, tk_c)

    far2 = _far_bias(rel_bias) * LOG2E
    near_a_p = _near_bias(rel_bias, tq_a, tk_a, 0, tp // tq_a) * LOG2E
    near_a_s = _near_bias(rel_bias, ts, tk_a, past_len, 1) * LOG2E
    near_c_p = _near_bias(rel_bias, tq_c, tk_c, 0, tp // tq_c) * LOG2E
    near_c_s = _near_bias(rel_bias, ts, tk_c, past_len, 1) * LOG2E
    pair = lambda nb: nb.reshape(nb.shape[0], hc, 2, *nb.shape[2:])

    w_mem = jnp.moveaxis(w_mem_kv, 0, 1).reshape(d, depth * 2 * mem_w).astype(BF16)
    mem_f, mem_b = _matmul(mem_prompt.reshape(bp * n_mem, d).astype(BF16), w_mem)
    mem_f = mem_f.reshape(bp, n_mem, depth, 2, h_mem, hd)
    mem_k_p = jnp.moveaxis(mem_f[:, :, :, 0], 2, 0)
    mem_v_p = jnp.moveaxis(mem_f[:, :, :, 1], 2, 0)
    mem_s_k = cache_mem_k.reshape(depth, bs * n_mem, mem_w).astype(BF16)
    mem_s_v = cache_mem_v.reshape(depth, bs * n_mem, mem_w).astype(BF16)

    x_f = jnp.concatenate([x_prompt.reshape(mp, d), x_sample.reshape(ms, d)], axis=0)
    x_b = x_f.astype(BF16)
    rows_p = [None] * depth
    rows_s = [None] * depth
    rows_buf = None

    for i in range(depth):
        kind, j = i % N_MIXERS, i // N_MIXERS
        h_b_ = _matmul_bf16(x_b, (w_a, w_b, w_c)[kind][j])
        w_k, w_v = ((w_a_k, w_a_v), (w_b_k, w_b_v), (w_c_k, w_c_v))[kind]
        k_shape = ((n_kv, hd), (h_b, hd), (hc, 2, hd))[kind]
        v_shape = ((n_kv, hd), (h_b, hd), (hc, 2 * hd))[kind]
        kp_f, kp_b = _proj_heads(x_b, 0, mp, w_k[j], k_shape)
        vp_f, vp_b = _proj_heads(x_b, 0, mp, w_v[j], v_shape)
        ks_f, ks_b = _proj_heads(x_b, mp, ms, w_k[j], k_shape)
        vs_f, vs_b = _proj_heads(x_b, mp, ms, w_v[j], v_shape)
        k_new, v_new = ks_b.reshape(bs, ts, -1), vs_b.reshape(bs, ts, -1)
        rows_p[i] = (kp_f.reshape(bp, tp, *k_shape), vp_f.reshape(bp, tp, *v_shape))
        rows_s[i] = (ks_f.reshape(bs, ts, *k_shape), vs_f.reshape(bs, ts, *v_shape))
        rb_s = mp // ts

        if kind == 0:
            k_sel_p = min(K_SEL_MAX, tp // 4)
            k_sel_s = min(K_SEL_MAX, s_new // 4)
            dims = dict(n_heads=h_a, n_kv=n_kv, hd=hd, n_idx=n_idx, d_idx=d_idx)
            mix_p = _dsa_attn(
                (h_b_, 0, 0, a_qi // qi_w, a_kw // LANE),
                (kp_b, 0, vp_b, 0, h_b_, a_kw // LANE),
                near_a_p, far2, n_batch=bp, t=tp, s_pad=tp, s_valid=tp, q_off=0,
                tq=tq_a, tk=tk_a, k_sel=k_sel_p, **dims)
            ki_f, ki_b = _matmul(x_b, w_ki[j])
            kiw_new = jnp.pad(ki_b[mp:].reshape(bs, ts, d_idx), ((0, 0), (0, 0), (0, LANE - d_idx)))
            kiw_past = jnp.pad(cache_a_kidx[j], ((0, 0), (0, 0), (0, LANE - d_idx)))
            k_all = _keys_with_past(cache_a_k[j].reshape(bs, past_len, kv_w), k_new, sp_a)
            v_all = _keys_with_past(cache_a_v[j].reshape(bs, past_len, kv_w), v_new, sp_a)
            ki_all = _keys_with_past(kiw_past, kiw_new, sp_a)
            mix_s = _dsa_attn(
                (h_b_, rb_s, 0, a_qi // qi_w, a_kw // LANE),
                (k_all, 0, v_all, 0, ki_all, 0),
                near_a_s, far2, n_batch=bs, t=ts, s_pad=sp_a, s_valid=s_new, q_off=past_len,
                tq=ts, tk=tk_a, k_sel=k_sel_s, **dims)
            mq_cb = a_mq // mem_w
            rows_p[i] += (ki_f[:mp].reshape(bp, tp, d_idx),)
            rows_s[i] += (ki_f[mp:].reshape(bs, ts, d_idx),)
        elif kind == 1:
            dims = dict(n_heads=h_b, hd=hd)
            mix_p = _stick_attn((h_b_, 0, 0), (kp_b, 0, vp_b, 0),
                                n_batch=bp, t=tp, s_pad=tp, q_off=0, tq=tq_b, tk=tk_b, **dims)
            k_all = _keys_with_past(cache_b_k[j].reshape(bs, past_len, tok_w), k_new, sp_b)
            v_all = _keys_with_past(cache_b_v[j].reshape(bs, past_len, tok_w), v_new, sp_b)
            mix_s = _stick_attn((h_b_, rb_s, 0), (k_all, 0, v_all, 0),
                                n_batch=bs, t=ts, s_pad=sp_b, q_off=past_len, tq=ts, tk=tk_b, **dims)
            mq_cb = tok_w // mem_w
        else:
            lambda_init = 0.8 - 0.6 * math.exp(-0.3 * i)
            lv = diff_lambda[j].astype(F32)
            lam = jnp.exp(jnp.sum(lv[0] * lv[1])) - jnp.exp(jnp.sum(lv[2] * lv[3])) + lambda_init
            scalars = jnp.concatenate([lam.reshape(1), far2]).astype(F32)
            g = diff_subln[j].reshape(1, 2 * hd).astype(F32)
            dims = dict(n_heads=hc, hd=hd, out_scale=1.0 - lambda_init)
            mix_p = _diff_attn((h_b_, 0, 0), (kp_b, 0, vp_b, 0), pair(near_c_p), scalars, g,
                               n_batch=bp, t=tp, s_pad=tp, s_valid=tp, q_off=0, tq=tq_c, tk=tk_c, **dims)
            k_all = _keys_with_past(cache_c_k[j].reshape(bs, past_len, tok_w), k_new, sp_c)
            v_all = _keys_with_past(cache_c_v[j].reshape(bs, past_len, tok_w), v_new, sp_c)
            mix_s = _diff_attn((h_b_, rb_s, 0), (k_all, 0, v_all, 0), pair(near_c_s), scalars, g,
                               n_batch=bs, t=ts, s_pad=sp_c, s_valid=s_new, q_off=past_len,
                               tq=ts, tk=tk_c, **dims)
            mq_cb = tok_w // mem_w

        mem_p = _mem_attn(h_b_, 0, mq_cb, mem_b[:, (2 * i) * mem_w:(2 * i + 1) * mem_w],
                          mem_b[:, (2 * i + 1) * mem_w:(2 * i + 2) * mem_w],
                          bp, tp, tq_m, h_mem, hd)
        mem_s = _mem_attn(h_b_, rb_s, mq_cb, mem_s_k[i], mem_s_v[i], bs, ts, ts, h_mem, hd)
        mix = jnp.concatenate([mix_p, mix_s], axis=0)
        mem_o = jnp.concatenate([mem_p, mem_s], axis=0)
        x_f, x_pk = _proj_ln(mix, mem_o, w_o_bf[i, :tok_w], w_o_bf[i, tok_w:], x_f,
                             ln_g[i, 0][None], ln_b[i, 0][None], alpha)
        y, dest, gates, rows_buf = _moe(x_f, x_pk, rows_buf, i, w_r_hi[i], w_r_lo[i], b_r_pad[i],
                                        w_gate_up, b_gu, w_down, b_dn)
        x_f, x_b = _combine_ln(x_f, y, dest, gates, ln_g[i, 1][None], ln_b[i, 1][None], alpha)

    def stack_kind(rows, kind):
        sel = [r for i, r in enumerate(rows) if i % N_MIXERS == kind]
        return [jnp.stack(parts) for parts in zip(*sel)]

    y_prompt = x_f[:mp].reshape(bp, tp, d)
    y_sample = x_f[mp:].reshape(bs, ts, d)
    a_k_p, a_v_p, a_kidx_p = stack_kind(rows_p, 0)
    b_k_p, b_v_p = stack_kind(rows_p, 1)
    c_k_p, c_v_p = stack_kind(rows_p, 2)
    a_k_s, a_v_s, a_kidx_s = stack_kind(rows_s, 0)
    b_k_s, b_v_s = stack_kind(rows_s, 1)
    c_k_s, c_v_s = stack_kind(rows_s, 2)
    return (y_prompt, y_sample, a_k_p, a_v_p, a_kidx_p, b_k_p, b_v_p, c_k_p, c_v_p, mem_k_p, mem_v_p,
            a_k_s, a_v_s, a_kidx_s, b_k_s, b_v_s, c_k_s, c_v_s)
```

```python
import functools
import math

import jax
import jax.numpy as jnp
from jax import lax
from jax.experimental import pallas as pl
from jax.experimental.pallas import tpu as pltpu

F32 = jnp.float32
BF16 = jnp.bfloat16
I32 = jnp.int32

CHUNK = 64
CHUNK_SHIFT = CHUNK.bit_length() - 1
K_SEL_MAX = 256
MAX_DISTANCE = 128
TOP_K = 4
SWIGLU_LIMIT = 7.0
SWIGLU_ALPHA = 1.702
LN_EPS = 1e-5
N_MIXERS = 3

LANE = 128
VMEM_LIMIT = 56 * 1024 * 1024
MASKED = -1e30
EXP_UNDERFLOW = -104.0
LOG2E = 1.4426950408889634
INT_MIN = -2 ** 31
KEY_NEG_INF = -2139095041
HI16 = -65536

def _cparams(n_axes):
    return pltpu.CompilerParams(dimension_semantics=("arbitrary",) * n_axes,
                                vmem_limit_bytes=VMEM_LIMIT)


def _round_up(x, m):
    return -(-x // m) * m


def _pick_tile(n, candidates):
    for c in candidates:
        if n % c == 0:
            return c
    return n


def _dot_nt(a, b):
    return lax.dot_general(a, b, (((1,), (1,)), ((), ())), preferred_element_type=F32)


def _resident(block_shape, index_map):
    return pl.BlockSpec(block_shape, index_map, pipeline_mode=pl.Buffered(1))


def _mm_kernel(x_ref, w_ref, of_ref, ob_ref):
    acc = jnp.dot(x_ref[...], w_ref[...], preferred_element_type=F32)
    of_ref[...] = acc
    ob_ref[...] = acc.astype(BF16)


def _matmul(x, w):
    m, k = x.shape
    n = w.shape[1]
    tm = _pick_tile(m, (640, 512, 256, 128, 64, 32, 16))
    tn = _pick_tile(n, (1280, 1024, 768, 512, 384, 256, 128))
    return pl.pallas_call(
        _mm_kernel,
        grid=(n // tn, m // tm),
        in_specs=[pl.BlockSpec((tm, k), lambda j, i: (i, 0)),
                  pl.BlockSpec((k, tn), lambda j, i: (0, j))],
        out_specs=[pl.BlockSpec((tm, tn), lambda j, i: (i, j)),
                   pl.BlockSpec((tm, tn), lambda j, i: (i, j))],
        out_shape=[jax.ShapeDtypeStruct((m, n), F32), jax.ShapeDtypeStruct((m, n), BF16)],
        compiler_params=_cparams(2),
        name="matmul",
    )(x, w)


def _mm_bf16_kernel(x_ref, w_ref, ob_ref):
    ob_ref[...] = jnp.dot(x_ref[...], w_ref[...], preferred_element_type=F32).astype(BF16)


def _matmul_bf16(x, w):
    m, k = x.shape
    n = w.shape[1]
    tm = _pick_tile(m, (640, 512, 256, 128, 64, 32, 16))
    tn = _pick_tile(n, (1280, 1024, 896, 768, 512, 384, 256, 128))
    return pl.pallas_call(
        _mm_bf16_kernel,
        grid=(n // tn, m // tm),
        in_specs=[pl.BlockSpec((tm, k), lambda j, i: (i, 0)),
                  pl.BlockSpec((k, tn), lambda j, i: (0, j))],
        out_specs=pl.BlockSpec((tm, tn), lambda j, i: (i, j)),
        out_shape=jax.ShapeDtypeStruct((m, n), BF16),
        compiler_params=_cparams(2),
        name="matmul_bf16",
    )(x, w)


def _proj_heads_kernel(x_ref, w_ref, of_ref, ob_ref, *, head_shape):
    acc = jnp.dot(x_ref[...], w_ref[...], preferred_element_type=F32)
    ob_ref[...] = acc.astype(BF16)
    slab = head_shape[-1]
    lead = head_shape[:-1]
    for c in range(math.prod(lead)):
        idx, rest = [], c
        for size in reversed(lead):
            idx.append(rest % size)
            rest //= size
        of_ref[(slice(None), *reversed(idx), slice(None))] = acc[:, c * slab:(c + 1) * slab]


def _proj_heads(x, row0, n_rows, w, head_shape):
    k = x.shape[1]
    n = w.shape[1]
    assert math.prod(head_shape) == n and head_shape[-1] % LANE == 0
    tm = _pick_tile(n_rows, (256, 128, 64, 32, 16))
    assert row0 % tm == 0
    zeros = (0,) * len(head_shape)
    return pl.pallas_call(
        functools.partial(_proj_heads_kernel, head_shape=tuple(head_shape)),
        grid=(n_rows // tm,),
        in_specs=[pl.BlockSpec((tm, k), lambda i: (row0 // tm + i, 0)),
                  pl.BlockSpec((k, n), lambda i: (0, 0))],
        out_specs=[pl.BlockSpec((tm, *head_shape), lambda i: (i, *zeros)),
                   pl.BlockSpec((tm, n), lambda i: (i, 0))],
        out_shape=[jax.ShapeDtypeStruct((n_rows, *head_shape), F32),
                   jax.ShapeDtypeStruct((n_rows, n), BF16)],
        compiler_params=_cparams(1),
        name="proj_heads",
    )(x, w)


def _layer_norm_rows(z, g, b):
    mu = jnp.mean(z, axis=-1, keepdims=True)
    zc = z - mu
    var = jnp.mean(zc * zc, axis=-1, keepdims=True)
    return zc * lax.rsqrt(var + LN_EPS) * g + b


def _pack_halves(x):
    half = x.shape[1] // 2
    xb = x.astype(BF16).astype(F32)
    lo = lax.shift_right_logical(pltpu.bitcast(xb[:, :half], I32), 16)
    return lo | (pltpu.bitcast(xb[:, half:], I32) & HI16)


def _unpack_halves(p):
    lo = pltpu.bitcast(p << 16, F32).astype(BF16)
    hi = pltpu.bitcast(p & HI16, F32).astype(BF16)
    return jnp.concatenate([lo, hi], axis=1)


def _proj_ln_kernel(a1_ref, a2_ref, w1_ref, w2_ref, r_ref, g_ref, b_ref, of_ref, op_ref, *, alpha):
    y = jnp.dot(a1_ref[...], w1_ref[...], preferred_element_type=F32)
    y = y + jnp.dot(a2_ref[...], w2_ref[...], preferred_element_type=F32)
    out = _layer_norm_rows(alpha * r_ref[...] + y, g_ref[...], b_ref[...])
    of_ref[...] = out
    op_ref[...] = _pack_halves(out)


def _proj_ln(a1, a2, w1, w2, res, g, b, alpha):
    m, k1 = a1.shape
    k2 = a2.shape[1]
    d = w1.shape[1]
    tm = _pick_tile(m, (256, 128, 64, 32, 16))
    row = lambda i: (i, 0)
    fix = lambda i: (0, 0)
    return pl.pallas_call(
        functools.partial(_proj_ln_kernel, alpha=alpha),
        grid=(m // tm,),
        in_specs=[pl.BlockSpec((tm, k1), row), pl.BlockSpec((tm, k2), row),
                  pl.BlockSpec((k1, d), fix), pl.BlockSpec((k2, d), fix),
                  pl.BlockSpec((tm, d), row), pl.BlockSpec((1, d), fix), pl.BlockSpec((1, d), fix)],
        out_specs=[pl.BlockSpec((tm, d), row), pl.BlockSpec((tm, d // 2), row)],
        out_shape=[jax.ShapeDtypeStruct((m, d), F32), jax.ShapeDtypeStruct((m, d // 2), I32)],
        compiler_params=_cparams(1),
        name="proj_ln",
    )(a1, a2, w1, w2, res, g, b)


def _row_copy(src_ref, src_row, dst_ref, dst_row, sem):
    return pltpu.make_async_copy(src_ref.at[pl.ds(src_row, 1)], dst_ref.at[pl.ds(dst_row, 1)], sem)


def _combine_ln_kernel(dest_ref, next_ref, x_ref, gate_ref, g_ref, b_ref, y_hbm, of_ref, ob_ref,
                       ybuf, sems, *, alpha, tm, n_steps):
    i = pl.program_id(0)
    slot = i % 2

    def gather(idx_ref, to_slot):
        def body(r, carry):
            for k in range(TOP_K):
                _row_copy(y_hbm, idx_ref[0, r * TOP_K + k], ybuf.at[to_slot, k], r,
                          sems.at[to_slot]).start()
            return carry
        lax.fori_loop(0, tm, body, 0)

    @pl.when(i == 0)
    def _():
        gather(dest_ref, 0)

    @pl.when(i + 1 < n_steps)
    def _():
        gather(next_ref, 1 - slot)

    def drain(r, carry):
        for k in range(TOP_K):
            _row_copy(y_hbm, 0, ybuf.at[slot, k], 0, sems.at[slot]).wait()
        return carry
    lax.fori_loop(0, tm, drain, 0)

    gates = gate_ref[...]
    ffn = gates[:, 0:1] * ybuf[slot, 0]
    for k in range(1, TOP_K):
        ffn = ffn + gates[:, k:k + 1] * ybuf[slot, k]
    out = _layer_norm_rows(alpha * x_ref[...] + ffn, g_ref[...], b_ref[...])
    of_ref[...] = out
    ob_ref[...] = out.astype(BF16)


def _combine_ln(x, y, dest, gates, g, b, alpha):
    m, d = x.shape
    tm = _pick_tile(m, (64, 32, 16))
    n_steps = m // tm
    dest2 = dest.reshape(n_steps, 1, tm * TOP_K)
    row = lambda i: (i, 0)
    fix = lambda i: (0, 0)
    return pl.pallas_call(
        functools.partial(_combine_ln_kernel, alpha=alpha, tm=tm, n_steps=n_steps),
        grid=(n_steps,),
        in_specs=[pl.BlockSpec((None, 1, tm * TOP_K), lambda i: (i, 0, 0), memory_space=pltpu.SMEM),
                  pl.BlockSpec((None, 1, tm * TOP_K), lambda i: (jnp.minimum(i + 1, n_steps - 1), 0, 0),
                               memory_space=pltpu.SMEM),
                  pl.BlockSpec((tm, d), row), pl.BlockSpec((tm, LANE), row),
                  pl.BlockSpec((1, d), fix), pl.BlockSpec((1, d), fix),
                  pl.BlockSpec(memory_space=pl.ANY)],
        out_specs=[pl.BlockSpec((tm, d), row), pl.BlockSpec((tm, d), row)],
        out_shape=[jax.ShapeDtypeStruct((m, d), F32), jax.ShapeDtypeStruct((m, d), BF16)],
        scratch_shapes=[pltpu.VMEM((2, TOP_K, tm, d), F32), pltpu.SemaphoreType.DMA((2,))],
        compiler_params=_cparams(1),
        name="combine_ln",
    )(dest2, dest2, x, gates, g, b, y)


def _mem_attn_kernel(q_ref, k_ref, v_ref, o_ref, *, n_heads, hd):
    scale = hd ** -0.5
    for h in range(n_heads):
        cols = slice(h * hd, (h + 1) * hd)
        s = _dot_nt(q_ref[:, cols], k_ref[:, cols]) * scale
        p = jnp.exp(s - jnp.max(s, axis=-1, keepdims=True))
        p = p / jnp.sum(p, axis=-1, keepdims=True)
        o_ref[:, cols] = jnp.dot(p.astype(BF16), v_ref[:, cols],
                                 preferred_element_type=F32).astype(BF16)


def _mem_attn(qarr, q_rb0, q_cb, mk, mv, n_batch, t, tq, n_heads, hd):
    w = n_heads * hd
    n_mem = mk.shape[0] // n_batch
    nqb = t // tq
    return pl.pallas_call(
        functools.partial(_mem_attn_kernel, n_heads=n_heads, hd=hd),
        grid=(n_batch, nqb),
        in_specs=[pl.BlockSpec((tq, w), lambda b, i: (q_rb0 + b * nqb + i, q_cb)),
                  pl.BlockSpec((n_mem, w), lambda b, i: (b, 0)),
                  pl.BlockSpec((n_mem, w), lambda b, i: (b, 0))],
        out_specs=pl.BlockSpec((tq, w), lambda b, i: (b * nqb + i, 0)),
        out_shape=jax.ShapeDtypeStruct((n_batch * t, w), BF16),
        compiler_params=_cparams(2),
        name="mem_attn",
    )(qarr, mk, mv)


def _t5_bucket(rel, num_buckets):
    half = num_buckets // 2
    exact = half // 2
    n = jnp.abs(rel)
    far = exact + (jnp.log(jnp.maximum(n, 1).astype(F32) / exact)
                   / math.log(MAX_DISTANCE / exact) * (half - exact)).astype(I32)
    return jnp.where(rel > 0, half, 0) + jnp.where(n < exact, n, jnp.minimum(far, half - 1))


def _near_phase(q_off, tq, tk, nqb, i):
    full = tk // tq
    return (q_off // tq + i) % full if nqb >= full else i


def _near_bias(rel_bias, tq, tk, q_off, nqb):
    full = tk // tq
    n_maps = rel_bias.shape[1]
    rel_lo = -(2 * tk + tq)
    rels = jnp.arange(rel_lo, tk + 1, dtype=I32)
    by_rel = jnp.moveaxis(rel_bias[_t5_bucket(rels, rel_bias.shape[0])], -1, 0).astype(F32)
    n = tq + tk
    tiles = []
    for p in range(min(full, nqb)):
        q_in_block = (p if nqb >= full else (q_off // tq + p) % full) * tq
        per_near = []
        for near in range(2):
            rel0 = -near * tk - q_in_block
            start = rel0 - rel_lo
            v = by_rel[:, start - tq:start + tk]
            u = jnp.roll(v, -tq, axis=1)
            skew = jnp.tile(u, (1, tq))[:, :tq * (n - 1)].reshape(n_maps, tq, n - 1)
            per_near.append(skew[:, :, :tk])
        tiles.append(jnp.stack(per_near, axis=1))
    return jnp.stack(tiles)


def _far_bias(rel_bias):
    return rel_bias[rel_bias.shape[0] // 2 - 1].astype(F32)


def _dsa_kernel(far_ref, q_ref, qi_ref, qw_ref, k_ref, v_ref, ki_ref, bd_ref, o_ref,
                key_scr, wb_scr, m_scr, acc_scr, *,
                tq, tk, q_off, s_valid, k_sel, n_heads, n_kv, hd, n_idx, d_idx, idx_bits):
    gqa = n_heads // n_kv
    reps = tk // LANE
    i = pl.program_id(1)
    q0 = q_off + i * tq
    kb_diag = q0 // tk
    nkb = kb_diag + 1
    q_pos = q0 + lax.broadcasted_iota(I32, (tq, 1), 0)
    lane = lax.broadcasted_iota(I32, (1, tk), 1)
    lane1 = lax.broadcasted_iota(I32, (1, LANE), 1)
    tile = lambda x: jnp.concatenate([x] * reps, axis=1)

    qi = qi_ref[...]
    qw = qw_ref[...].astype(F32)
    qi_heads = [qi[:, h * d_idx:(h + 1) * d_idx] for h in range(n_idx)]
    for h in range(n_idx):
        wb_scr[h] = jnp.broadcast_to(qw[:, d_idx + h:d_idx + h + 1] * (d_idx ** -0.5), (tq, LANE))

    def score_block(kb, carry):
        ks = pl.multiple_of(kb * tk, tk)
        ki = ki_ref[pl.ds(ks, tk), :][:, :d_idx]
        sc = jnp.zeros((tq, tk), F32)
        for h in range(n_idx):
            sc = sc + jnp.maximum(_dot_nt(qi_heads[h], ki), 0.0) * tile(wb_scr[h])
        sc = sc * (n_idx ** -0.5)
        sc = jnp.where(sc == 0.0, 0.0, sc)
        k_pos = ks + lane
        allowed = ((k_pos >> CHUNK_SHIFT) <= (q_pos >> CHUNK_SHIFT)) & (k_pos < s_valid)
        sc = jnp.where(allowed, sc, -jnp.inf)
        bits = pltpu.bitcast(sc, I32)
        key_scr[kb] = bits ^ ((bits >> 31) & 0x7FFFFFFF)
        return carry

    lax.fori_loop(0, nkb, score_block, 0)

    def count(pred):
        def body(kb, acc):
            key = key_scr[kb]
            for c in range(reps):
                acc = acc + jnp.where(pred(kb, c, key[:, c * LANE:(c + 1) * LANE]), 1.0, 0.0)
            return acc
        acc = lax.fori_loop(0, nkb, body, jnp.zeros((tq, LANE), F32))
        return jnp.sum(acc, axis=1, keepdims=True)

    def bisect_cond(state):
        it, _, _, open_rows = state
        return (it < 32) & (open_rows > 0.0)

    def bisect_body(state):
        it, t_u, cnt_t, _ = state
        cand_u = t_u | (jnp.int32(1) << (31 - it))
        cand_b = jnp.broadcast_to(cand_u ^ INT_MIN, (tq, LANE))
        cnt = count(lambda kb, c, key: key >= cand_b)
        take = cnt >= k_sel
        t_u = jnp.where(take, cand_u, t_u)
        cnt_t = jnp.where(take, cnt, cnt_t)
        return it + 1, t_u, cnt_t, jnp.max(jnp.where(cnt_t > k_sel, 1.0, 0.0))

    total = (nkb * tk).astype(F32)
    _, t_u, cnt_t, _ = lax.while_loop(
        bisect_cond, bisect_body,
        (jnp.int32(0), jnp.zeros((tq, 1), I32), jnp.full((tq, 1), total, F32),
         jnp.where(total > k_sel, 1.0, 0.0)))
    thr = t_u ^ INT_MIN
    thr_b = jnp.broadcast_to(thr, (tq, LANE))

    def tie_search():
        need = k_sel - count(lambda kb, c, key: key > thr_b)

        def index_bit(it, j_hi):
            cand = j_hi | (jnp.int32(1) << (idx_bits - 1 - it))
            cand_b = jnp.broadcast_to(cand, (tq, LANE))
            cnt = count(lambda kb, c, key: (key == thr_b) & (kb * tk + c * LANE + lane1 < cand_b))
            return jnp.where(cnt < need, cand, j_hi)

        return lax.fori_loop(0, idx_bits, index_bit, jnp.zeros((tq, 1), I32))

    tie_hi = lax.cond(jnp.max(cnt_t) > k_sel, tie_search, lambda: jnp.full((tq, 1), 2 ** 30, I32))

    m_scr[...] = jnp.full(m_scr.shape, MASKED, F32)
    acc_scr[...] = jnp.zeros(acc_scr.shape, F32)
    scale2 = (hd ** -0.5) * LOG2E
    q_groups = [jnp.concatenate([q_ref[:, (g * gqa + r) * hd:(g * gqa + r + 1) * hd]
                                 for r in range(gqa)], axis=0) for g in range(n_kv)]
    far_rows = [jnp.concatenate([jnp.full((tq, LANE), far_ref[g * gqa + r], F32)
                                 for r in range(gqa)], axis=0) for g in range(n_kv)]
    ones = jnp.ones((tk, hd), BF16)

    def attend_block(kb, near):
        ks = pl.multiple_of(kb * tk, tk)
        key = key_scr[kb]
        sel = (key > thr) | ((key == thr) & (ks + lane <= tie_hi))
        sel = sel & (key > KEY_NEG_INF)
        mbias = jnp.where(sel, 0.0, MASKED)
        for g in range(n_kv):
            kg = k_ref[pl.ds(ks, tk), g * hd:(g + 1) * hd]
            vext = jnp.concatenate([v_ref[pl.ds(ks, tk), g * hd:(g + 1) * hd], ones], axis=1)
            s = _dot_nt(q_groups[g], kg) * scale2
            m_old = m_scr[g]
            if near is None:
                t = s + jnp.concatenate([mbias] * gqa, axis=0)
                m_new = jnp.maximum(m_old, jnp.max(t, axis=-1, keepdims=True) + far_rows[g])
                p = jnp.exp2(t - tile(m_new - far_rows[g]))
            else:
                t = s + jnp.concatenate([mbias + bd_ref[g * gqa + r, near] for r in range(gqa)], axis=0)
                m_new = jnp.maximum(m_old, jnp.max(t, axis=-1, keepdims=True))
                p = jnp.exp2(t - tile(m_new))
            alpha = jnp.exp2(m_old - m_new)
            acc_scr[g] = (jnp.concatenate([alpha, alpha], axis=1) * acc_scr[g]
                          + jnp.dot(p.astype(BF16), vext, preferred_element_type=F32))
            m_scr[g] = m_new

    def far_block(kb, carry):
        attend_block(kb, None)
        return carry

    lax.fori_loop(0, jnp.maximum(kb_diag - 1, 0), far_block, 0)

    @pl.when(kb_diag >= 1)
    def _():
        attend_block(kb_diag - 1, 1)

    attend_block(kb_diag, 0)

    for g in range(n_kv):
        acc = acc_scr[g]
        o = acc[:, :hd] / acc[:, hd:]
        for r in range(gqa):
            h = g * gqa + r
            o_ref[:, h * hd:(h + 1) * hd] = o[r * tq:(r + 1) * tq].astype(BF16)


def _dsa_attn(q_src, k_src, bias_near, bias_far, *, n_batch, t, s_pad, s_valid, q_off, tq, tk,
              k_sel, n_heads, n_kv, hd, n_idx, d_idx):
    qarr, q_rb0, q_cb, qi_cb, qw_cb = q_src
    karr, k_cb, varr, v_cb, kiarr, ki_cb = k_src
    assert tk >= MAX_DISTANCE and tk % tq == 0 and q_off % tq == 0 and s_pad % tk == 0
    nqb = t // tq
    nkb_max = s_pad // tk
    gqa = n_heads // n_kv
    wq, wkv, wqi = n_heads * hd, n_kv * hd, n_idx * d_idx
    qrow = lambda b, i: q_rb0 + b * nqb + i
    kernel = functools.partial(
        _dsa_kernel, tq=tq, tk=tk, q_off=q_off, s_valid=s_valid, k_sel=k_sel, n_heads=n_heads,
        n_kv=n_kv, hd=hd, n_idx=n_idx, d_idx=d_idx, idx_bits=max((s_pad - 1).bit_length(), 1))
    return pl.pallas_call(
        kernel,
        grid=(n_batch, nqb),
        in_specs=[pl.BlockSpec(memory_space=pltpu.SMEM),
                  pl.BlockSpec((tq, wq), lambda b, i: (qrow(b, i), q_cb)),
                  pl.BlockSpec((tq, wqi), lambda b, i: (qrow(b, i), qi_cb)),
                  pl.BlockSpec((tq, LANE), lambda b, i: (qrow(b, i), qw_cb)),
                  _resident((s_pad, wkv), lambda b, i: (b, k_cb)),
                  _resident((s_pad, wkv), lambda b, i: (b, v_cb)),
                  _resident((s_pad, LANE), lambda b, i: (b, ki_cb)),
                  pl.BlockSpec((None, n_heads, 2, tq, tk),
                               lambda b, i: (_near_phase(q_off, tq, tk, nqb, i), 0, 0, 0, 0))],
        out_specs=pl.BlockSpec((tq, wq), lambda b, i: (b * nqb + i, 0)),
        out_shape=jax.ShapeDtypeStruct((n_batch * t, wq), BF16),
        scratch_shapes=[pltpu.VMEM((nkb_max, tq, tk), I32),
                        pltpu.VMEM((n_idx, tq, LANE), F32),
                        pltpu.VMEM((n_kv, gqa * tq, LANE), F32),
                        pltpu.VMEM((n_kv, gqa * tq, 2 * hd), F32)],
        compiler_params=_cparams(2),
        name="dsa_attn",
    )(bias_far, qarr, qarr, qarr, karr, varr, kiarr, bias_near)


def _stick_kernel(q_ref, k_ref, v_ref, o_ref, carry_scr, acc_scr, *, tq, tk, q_off, hd):
    i = pl.program_id(2)
    q0 = q_off + i * tq
    q_pos = q0 + lax.broadcasted_iota(I32, (tq, 1), 0)
    lane = lax.broadcasted_iota(I32, (1, tk), 1)
    tri = (lax.broadcasted_iota(I32, (tk, tk), 0) > lax.broadcasted_iota(I32, (tk, tk), 1)).astype(BF16)
    q = q_ref[...]
    scale = hd ** -0.5
    carry_scr[...] = jnp.zeros(carry_scr.shape, F32)
    acc_scr[...] = jnp.zeros(acc_scr.shape, F32)

    def cond(state):
        kb, carry_max = state
        return (kb >= 0) & (carry_max > EXP_UNDERFLOW)

    def body(state):
        kb, _ = state
        ks = pl.multiple_of(kb * tk, tk)
        z = _dot_nt(q, k_ref[pl.ds(ks, tk), :]) * scale
        before = (ks + lane) < q_pos
        log_beta = jnp.minimum(z, 0.0) - jnp.log(1.0 + jnp.exp(-jnp.abs(z)))
        log_keep = jnp.where(before, log_beta - z, 0.0)
        hi = log_keep.astype(BF16)
        lo = (log_keep - hi.astype(F32)).astype(BF16)
        tail = (jnp.dot(hi, tri, preferred_element_type=F32)
                + jnp.dot(lo, tri, preferred_element_type=F32))
        carry = carry_scr[...]
        w = jnp.where(before, jnp.exp(log_beta + tail + carry), 0.0)
        acc_scr[...] += jnp.dot(w.astype(BF16), v_ref[pl.ds(ks, tk), :], preferred_element_type=F32)
        carry = carry + jnp.sum(log_keep, axis=-1, keepdims=True)
        carry_scr[...] = carry
        return kb - 1, jnp.max(carry)

    lax.while_loop(cond, body, ((q0 + tq - 1) // tk, jnp.float32(0.0)))
    o_ref[...] = acc_scr[...].astype(BF16)


def _stick_attn(q_src, k_src, *, n_batch, t, s_pad, q_off, tq, tk, n_heads, hd):
    qarr, q_rb0, q_cb0 = q_src
    karr, k_cb0, varr, v_cb0 = k_src
    nqb = t // tq
    return pl.pallas_call(
        functools.partial(_stick_kernel, tq=tq, tk=tk, q_off=q_off, hd=hd),
        grid=(n_batch, n_heads, nqb),
        in_specs=[pl.BlockSpec((tq, hd), lambda b, h, i: (q_rb0 + b * nqb + i, q_cb0 + h)),
                  pl.BlockSpec((s_pad, hd), lambda b, h, i: (b, k_cb0 + h)),
                  pl.BlockSpec((s_pad, hd), lambda b, h, i: (b, v_cb0 + h))],
        out_specs=pl.BlockSpec((tq, hd), lambda b, h, i: (b * nqb + i, h)),
        out_shape=jax.ShapeDtypeStruct((n_batch * t, n_heads * hd), BF16),
        scratch_shapes=[pltpu.VMEM((tq, 1), F32), pltpu.VMEM((tq, hd), F32)],
        compiler_params=_cparams(3),
        name="stick_attn",
    )(qarr, karr, varr)


def _diff_kernel(scal_ref, q_ref, k_ref, v_ref, bd_ref, g_ref, o_ref, m_scr, acc_scr, *,
                 tq, tk, q_off, s_valid, hd, out_scale):
    h = pl.program_id(1)
    i = pl.program_id(2)
    reps = tk // LANE
    q0 = q_off + i * tq
    kb_diag = q0 // tk
    q_pos = q0 + lax.broadcasted_iota(I32, (tq, 1), 0)
    lane = lax.broadcasted_iota(I32, (1, tk), 1)
    tile = lambda x: jnp.concatenate([x] * reps, axis=1)
    scale2 = (hd ** -0.5) * LOG2E
    m_scr[...] = jnp.full(m_scr.shape, MASKED, F32)
    acc_scr[...] = jnp.zeros(acc_scr.shape, F32)
    q_maps = [q_ref[:, m * hd:(m + 1) * hd] for m in range(2)]
    ones = jnp.ones((tk, hd), BF16)

    def attend_block(kb, near):
        ks = pl.multiple_of(kb * tk, tk)
        vext = jnp.concatenate([v_ref[pl.ds(ks, tk), :], ones], axis=1)
        if near is not None:
            k_pos = ks + lane
            allowed = ((k_pos >> CHUNK_SHIFT) <= (q_pos >> CHUNK_SHIFT)) & (k_pos < s_valid)
            mbias = jnp.where(allowed, 0.0, MASKED)
        for m in range(2):
            s = _dot_nt(q_maps[m], k_ref[pl.ds(ks, tk), m * hd:(m + 1) * hd]) * scale2
            m_old = m_scr[m]
            if near is None:
                far = scal_ref[1 + 2 * h + m]
                m_new = jnp.maximum(m_old, jnp.max(s, axis=-1, keepdims=True) + far)
                p = jnp.exp2(s - tile(m_new - far))
            else:
                t = s + (mbias + bd_ref[m, near])
                m_new = jnp.maximum(m_old, jnp.max(t, axis=-1, keepdims=True))
                p = jnp.exp2(t - tile(m_new))
            alpha = jnp.exp2(m_old - m_new)
            acc_scr[m] = (jnp.concatenate([alpha] * 3, axis=1) * acc_scr[m]
                          + jnp.dot(p.astype(BF16), vext, preferred_element_type=F32))
            m_scr[m] = m_new

    n_far = jnp.maximum(kb_diag - 1, 0)

    def far_pair(pair, carry):
        attend_block(2 * pair, None)
        attend_block(2 * pair + 1, None)
        return carry

    lax.fori_loop(0, n_far // 2, far_pair, 0)

    @pl.when(n_far % 2 == 1)
    def _():
        attend_block(n_far - 1, None)

    @pl.when(kb_diag >= 1)
    def _():
        attend_block(kb_diag - 1, 1)

    attend_block(kb_diag, 0)

    def normalised(m):
        acc = acc_scr[m]
        return acc[:, :2 * hd] / jnp.concatenate([acc[:, 2 * hd:]] * 2, axis=1)

    o = normalised(0) - scal_ref[0] * normalised(1)
    o = o * lax.rsqrt(jnp.mean(o * o, axis=-1, keepdims=True) + LN_EPS) * g_ref[...]
    o_ref[...] = (o * out_scale).astype(BF16)


def _diff_attn(q_src, k_src, bias_near, scalars, subln_g, *, n_batch, t, s_pad, s_valid, q_off,
               tq, tk, n_heads, hd, out_scale):
    qarr, q_rb0, q_cb0 = q_src
    karr, k_cb0, varr, v_cb0 = k_src
    assert tk >= MAX_DISTANCE and tk % tq == 0 and q_off % tq == 0 and s_pad % tk == 0
    nqb = t // tq
    w = 2 * hd
    return pl.pallas_call(
        functools.partial(_diff_kernel, tq=tq, tk=tk, q_off=q_off, s_valid=s_valid, hd=hd,
                          out_scale=out_scale),
        grid=(n_batch, n_heads, nqb),
        in_specs=[pl.BlockSpec(memory_space=pltpu.SMEM),
                  pl.BlockSpec((tq, w), lambda b, h, i: (q_rb0 + b * nqb + i, q_cb0 + h)),
                  pl.BlockSpec((s_pad, w), lambda b, h, i: (b, k_cb0 + h)),
                  pl.BlockSpec((s_pad, w), lambda b, h, i: (b, v_cb0 + h)),
                  pl.BlockSpec((None, None, 2, 2, tq, tk),
                               lambda b, h, i: (_near_phase(q_off, tq, tk, nqb, i), h, 0, 0, 0, 0)),
                  pl.BlockSpec((1, w), lambda b, h, i: (0, 0))],
        out_specs=pl.BlockSpec((tq, w), lambda b, h, i: (b * nqb + i, h)),
        out_shape=jax.ShapeDtypeStruct((n_batch * t, n_heads * w), BF16),
        scratch_shapes=[pltpu.VMEM((2, tq, LANE), F32), pltpu.VMEM((2, tq, w + hd), F32)],
        compiler_params=_cparams(3),
        name="diff_attn",
    )(scalars, qarr, karr, varr, bias_near, subln_g)


def _router_kernel(x_ref, wh_ref, wl_ref, b_ref, e_ref, g_ref, r_ref, c_ref, seen_scr):
    @pl.when(pl.program_id(0) == 0)
    def _():
        seen_scr[...] = jnp.zeros(seen_scr.shape, F32)

    x = x_ref[...]
    xh = x.astype(BF16)
    xl = (x - xh.astype(F32)).astype(BF16)
    wh = wh_ref[...]
    logits = (jnp.dot(xh, wh, preferred_element_type=F32)
              + jnp.dot(xl, wh, preferred_element_type=F32)
              + jnp.dot(xh, wl_ref[...], preferred_element_type=F32)) + b_ref[...]
    tm = logits.shape[0]
    lane = lax.broadcasted_iota(I32, logits.shape, 1)
    lane_f = lane.astype(F32)
    experts = jnp.zeros(logits.shape, I32)
    gates = jnp.zeros(logits.shape, F32)
    chosen = []
    top = None
    for k in range(TOP_K):
        m = jnp.max(logits, axis=-1, keepdims=True)
        ix = jnp.min(jnp.where(logits == m, lane_f, float(LANE)), axis=-1, keepdims=True).astype(I32)
        top = m if top is None else top
        experts = jnp.where(lane == k, ix, experts)
        gates = jnp.where(lane == k, jnp.exp(m - top), gates)
        chosen.append(lane == ix)
        logits = jnp.where(chosen[-1], -jnp.inf, logits)
    e_ref[...] = experts
    g_ref[...] = gates / jnp.sum(gates, axis=-1, keepdims=True)

    hot = functools.reduce(jnp.logical_or, chosen)
    hot_f = jnp.where(hot, 1.0, 0.0)
    lower = (lax.broadcasted_iota(I32, (tm, tm), 0) > lax.broadcasted_iota(I32, (tm, tm), 1)).astype(BF16)
    earlier = jnp.dot(lower, hot_f.astype(BF16), preferred_element_type=F32) + seen_scr[...]
    ranks = jnp.zeros(logits.shape, F32)
    for k in range(TOP_K):
        rank_k = jnp.sum(jnp.where(chosen[k], earlier, 0.0), axis=-1, keepdims=True)
        ranks = jnp.where(lane == k, rank_k, ranks)
    r_ref[...] = ranks.astype(I32)
    seen = seen_scr[...] + jnp.sum(hot_f, axis=0, keepdims=True)
    seen_scr[...] = seen
    c_ref[...] = seen


def _router(x, w_hi, w_lo, bias):
    m, d = x.shape
    tm = _pick_tile(m, (256, 128, 64, 32, 16))
    row = lambda i: (i, 0)
    fix = lambda i: (0, 0)
    return pl.pallas_call(
        _router_kernel,
        grid=(m // tm,),
        in_specs=[pl.BlockSpec((tm, d), row), pl.BlockSpec((d, LANE), fix),
                  pl.BlockSpec((d, LANE), fix), pl.BlockSpec((1, LANE), fix)],
        out_specs=[pl.BlockSpec((tm, LANE), row), pl.BlockSpec((tm, LANE), row),
                   pl.BlockSpec((tm, LANE), row), pl.BlockSpec((1, LANE), fix)],
        out_shape=[jax.ShapeDtypeStruct((m, LANE), I32), jax.ShapeDtypeStruct((m, LANE), F32),
                   jax.ShapeDtypeStruct((m, LANE), I32), jax.ShapeDtypeStruct((1, LANE), F32)],
        scratch_shapes=[pltpu.VMEM((1, LANE), F32)],
        compiler_params=_cparams(1),
        name="router",
    )(x, w_hi, w_lo, bias)


def _dispatch_kernel(dest_ref, x_ref, rows_in, rows_out, sem, *, tm):
    del rows_in

    def send(r, carry):
        for k in range(TOP_K):
            _row_copy(x_ref, r, rows_out, dest_ref[0, r * TOP_K + k], sem).start()
        return carry
    lax.fori_loop(0, tm, send, 0)

    def drain(r, carry):
        for k in range(TOP_K):
            _row_copy(x_ref, 0, rows_out, 0, sem).wait()
        return carry
    lax.fori_loop(0, tm, drain, 0)


def _dispatch(x_packed, dest, rows_buf):
    m, w = x_packed.shape
    tm = _pick_tile(m, (256, 128, 64, 32, 16))
    n_steps = m // tm
    return pl.pallas_call(
        functools.partial(_dispatch_kernel, tm=tm),
        grid=(n_steps,),
        in_specs=[pl.BlockSpec((None, 1, tm * TOP_K), lambda i: (i, 0, 0), memory_space=pltpu.SMEM),
                  pl.BlockSpec((tm, w), lambda i: (i, 0)),
                  pl.BlockSpec(memory_space=pl.ANY)],
        out_specs=pl.BlockSpec(memory_space=pl.ANY),
        out_shape=jax.ShapeDtypeStruct(rows_buf.shape, I32),
        scratch_shapes=[pltpu.SemaphoreType.DMA(())],
        input_output_aliases={2: 0},
        compiler_params=_cparams(1),
        name="dispatch",
    )(dest.reshape(n_steps, 1, tm * TOP_K), x_packed, rows_buf)


def _cast_rows(src_ref, dst_ref, rows_per_step):
    def body(c, carry):
        r = pl.multiple_of(c * rows_per_step, rows_per_step)
        dst_ref[pl.ds(r, rows_per_step), :] = src_ref[pl.ds(r, rows_per_step), :].astype(BF16)
        return carry
    lax.fori_loop(0, src_ref.shape[0] // rows_per_step, body, 0)


def _expert_changed(be_ref, j):
    return (j == 0) | (be_ref[j] != be_ref[jnp.maximum(j - 1, 0)])


def _moe_up_kernel(be_ref, nu_ref, x_ref, wg_ref, wl_ref, bg_ref, bl_ref, h_ref, wg_bf, wl_bf, *,
                   cast_rows):
    j = pl.program_id(1)

    @pl.when(_expert_changed(be_ref, j))
    def _():
        _cast_rows(wg_ref, wg_bf, cast_rows)
        _cast_rows(wl_ref, wl_bf, cast_rows)

    def compute(n):
        x = _unpack_halves(x_ref[:n])
        glu = jnp.dot(x, wg_bf[...], preferred_element_type=F32) + bg_ref[...]
        lin = jnp.dot(x, wl_bf[...], preferred_element_type=F32) + bl_ref[...]
        glu = jnp.minimum(glu, SWIGLU_LIMIT)
        lin = jnp.clip(lin, -SWIGLU_LIMIT, SWIGLU_LIMIT)
        h_ref[:n] = (glu * jax.nn.sigmoid(SWIGLU_ALPHA * glu) * (lin + 1.0)).astype(BF16)

    _by_block_fill(nu_ref[1 + j], h_ref, compute)


def _by_block_fill(rows, out_ref, compute):
    tm = out_ref.shape[0]
    half = tm // 2

    @pl.when(rows > half)
    def _():
        compute(tm)

    @pl.when((rows > 0) & (rows <= half))
    def _():
        compute(half)
        out_ref[half:] = jnp.zeros((tm - half,) + out_ref.shape[1:], out_ref.dtype)

    @pl.when(rows == 0)
    def _():
        out_ref[...] = jnp.zeros(out_ref.shape, out_ref.dtype)


def _moe_down_kernel(be_ref, nu_ref, h_ref, wd_ref, bd_ref, y_ref, wd_bf, *, cast_rows):
    j = pl.program_id(1)

    @pl.when(_expert_changed(be_ref, j))
    def _():
        _cast_rows(wd_ref, wd_bf, cast_rows)

    def compute(n):
        y_ref[:n] = jnp.dot(h_ref[:n], wd_bf[...], preferred_element_type=F32) + bd_ref[...]

    _by_block_fill(nu_ref[1 + j], y_ref, compute)


def _moe_experts(xs, block_e, n_used, layer, w_gate_up, b_gate_up, w_down, b_down, tm):
    n_rows = xs.shape[0]
    d = 2 * xs.shape[1]
    n_blocks = n_rows // tm
    d_ff = w_down.shape[2]
    tf = _pick_tile(d_ff, (1024, 512, 256, 128))
    nf = d_ff // tf
    tn = _pick_tile(d, (1024, 512, 256, 128))
    blk = lambda j, nu: jnp.minimum(j, nu[0] - 1)
    hidden = pl.pallas_call(
        functools.partial(_moe_up_kernel, cast_rows=_pick_tile(d, (256, 128, 64, 32, 16, 8))),
        grid_spec=pltpu.PrefetchScalarGridSpec(
            num_scalar_prefetch=2,
            grid=(nf, n_blocks),
            in_specs=[pl.BlockSpec((tm, d // 2), lambda f, j, be, nu: (blk(j, nu), 0)),
                      pl.BlockSpec((None, None, d, tf), lambda f, j, be, nu: (layer, be[j], 0, f)),
                      pl.BlockSpec((None, None, d, tf), lambda f, j, be, nu: (layer, be[j], 0, nf + f)),
                      pl.BlockSpec((None, None, 1, tf), lambda f, j, be, nu: (layer, be[j], 0, f)),
                      pl.BlockSpec((None, None, 1, tf), lambda f, j, be, nu: (layer, be[j], 0, nf + f))],
            out_specs=pl.BlockSpec((tm, tf), lambda f, j, be, nu: (j, f)),
            scratch_shapes=[pltpu.VMEM((d, tf), BF16), pltpu.VMEM((d, tf), BF16)]),
        out_shape=jax.ShapeDtypeStruct((n_rows, d_ff), BF16),
        compiler_params=_cparams(2),
        name="moe_up",
    )(block_e, n_used, xs, w_gate_up, w_gate_up, b_gate_up, b_gate_up)
    return pl.pallas_call(
        functools.partial(_moe_down_kernel, cast_rows=_pick_tile(d_ff, (256, 128, 64, 32, 16, 8))),
        grid_spec=pltpu.PrefetchScalarGridSpec(
            num_scalar_prefetch=2,
            grid=(d // tn, n_blocks),
            in_specs=[pl.BlockSpec((tm, d_ff), lambda n, j, be, nu: (blk(j, nu), 0)),
                      pl.BlockSpec((None, None, d_ff, tn), lambda n, j, be, nu: (layer, be[j], 0, n)),
                      pl.BlockSpec((None, None, 1, tn), lambda n, j, be, nu: (layer, be[j], 0, n))],
            out_specs=pl.BlockSpec((tm, tn), lambda n, j, be, nu: (j, n)),
            scratch_shapes=[pltpu.VMEM((d_ff, tn), BF16)]),
        out_shape=jax.ShapeDtypeStruct((n_rows, d), F32),
        compiler_params=_cparams(2),
        name="moe_down",
    )(block_e, n_used, hidden, w_down, b_down)


def _moe(x_f32, x_packed, rows_buf, layer, w_router_hi, w_router_lo, b_router, w_gate_up,
         b_gate_up, w_down, b_down):
    n = x_f32.shape[0]
    n_experts = w_gate_up.shape[1]
    tm = 512 if n * TOP_K >= 1024 * n_experts else 128
    experts, gates, ranks, counts = _router(x_f32, w_router_hi, w_router_lo, b_router)
    counts = counts[0, :n_experts].astype(I32)
    pend = jnp.cumsum((counts + tm - 1) // tm * tm)
    pstart = pend - (counts + tm - 1) // tm * tm
    n_blocks = -(-(n * TOP_K + n_experts * (tm - 1)) // tm)
    block_e = jnp.minimum(jnp.sum(pend[None, :] <= (jnp.arange(n_blocks) * tm)[:, None], axis=1),
                          n_experts - 1).astype(I32)
    first_row = jnp.arange(n_blocks, dtype=I32) * tm
    block_rows = jnp.clip(counts[block_e] - (first_row - pstart[block_e].astype(I32)), 0, tm)
    block_rows = jnp.where(first_row < pend[-1], block_rows, 0).astype(I32)
    n_used = jnp.concatenate([(pend[-1] // tm).astype(I32).reshape(1), block_rows])
    e4 = experts[:, :TOP_K]
    onehot = e4[..., None] == jnp.arange(n_experts, dtype=I32)
    dest = (ranks[:, :TOP_K] + jnp.sum(jnp.where(onehot, pstart.astype(I32), 0), axis=-1)).reshape(-1)
    if rows_buf is None:
        rows_buf = jnp.zeros((n_blocks * tm,) + x_packed.shape[1:], I32)
    xs = _dispatch(x_packed, dest, rows_buf)
    y = _moe_experts(xs, block_e, n_used, layer, w_gate_up, b_gate_up, w_down, b_down, tm)
    return y, dest, gates, xs


def _pad_cols(w, n):
    return jnp.pad(w, ((0, 0), (0, n - w.shape[1])))


def _keys_with_past(past, new, s_pad):
    b, p, w = past.shape
    rows = jnp.pad(past.astype(BF16), ((0, 0), (0, s_pad - p), (0, 0)))
    rows = lax.dynamic_update_slice(rows, new.astype(BF16), (0, p, 0))
    return rows.reshape(b * s_pad, w)


def kernel(x_prompt, x_sample, cache_a_k, cache_a_v, cache_a_kidx, cache_b_k, cache_b_v,
           cache_c_k, cache_c_v, cache_mem_k, cache_mem_v, mem_prompt,
           w_in_a, w_in_b, w_in_c, diff_lambda, diff_subln, rel_bias, w_mem_kv, w_o,
           ln_g, ln_b, w_router, b_router, w_gate_up, b_gate_up, w_down, b_down):
    bp, tp, d = x_prompt.shape
    bs, ts, _ = x_sample.shape
    depth = w_o.shape[0]
    past_len = cache_a_k.shape[2]
    n_kv, hd = cache_a_k.shape[3], cache_a_k.shape[4]
    d_idx = cache_a_kidx.shape[3]
    n_mem, h_mem = cache_mem_k.shape[2], cache_mem_k.shape[3]
    mem_w = h_mem * hd
    tok_w = d - mem_w
    h_a = tok_w // hd
    h_b = cache_b_k.shape[3]
    hc = cache_c_k.shape[3]
    n_idx = (w_in_a.shape[2] - tok_w - 2 * n_kv * hd - d_idx - mem_w) // (d_idx + 1)
    n_experts = w_router.shape[2]
    alpha = (2 * depth) ** 0.25
    mp, ms = bp * tp, bs * ts
    s_new = past_len + ts
    kv_w, qi_w = n_kv * hd, n_idx * d_idx

    o_k, o_v, o_qi = tok_w, tok_w + kv_w, tok_w + 2 * kv_w
    o_ki = o_qi + qi_w
    o_wi = o_ki + d_idx
    o_mq = o_wi + n_idx
    a_qi = tok_w
    a_mq = a_qi + qi_w
    a_kw = a_mq + mem_w
    bf = lambda w: w.astype(BF16)
    w_a = [bf(_pad_cols(jnp.concatenate([w[:, :o_k], w[:, o_qi:o_ki], w[:, o_mq:], w[:, o_ki:o_mq]],
                                        axis=1), _round_up(a_kw + LANE, 7 * LANE if d >= 7 * LANE else LANE)))
           for w in w_in_a]
    w_a_k = [bf(w[:, o_k:o_v]) for w in w_in_a]
    w_a_v = [bf(w[:, o_v:o_qi]) for w in w_in_a]
    w_ki = [bf(w[:, o_ki:o_wi]) for w in w_in_a]
    main_bc = lambda w: bf(jnp.concatenate([w[:, :tok_w], w[:, 3 * tok_w:]], axis=1))
    w_b, w_c = [main_bc(w) for w in w_in_b], [main_bc(w) for w in w_in_c]
    w_b_k, w_b_v = [bf(w[:, tok_w:2 * tok_w]) for w in w_in_b], [bf(w[:, 2 * tok_w:3 * tok_w]) for w in w_in_b]
    w_c_k, w_c_v = [bf(w[:, tok_w:2 * tok_w]) for w in w_in_c], [bf(w[:, 2 * tok_w:3 * tok_w]) for w in w_in_c]
    w_o_bf = w_o.astype(BF16)
    w_r_pad = jnp.pad(w_router, ((0, 0), (0, 0), (0, LANE - n_experts)))
    w_r_hi = w_r_pad.astype(BF16)
    w_r_lo = (w_r_pad - w_r_hi.astype(F32)).astype(BF16)
    b_r_pad = jnp.pad(b_router, ((0, 0), (0, LANE - n_experts)), constant_values=MASKED)[:, None, :]
    b_gu = b_gate_up[:, :, None, :]
    b_dn = b_down[:, :, None, :]

    tq_a = _pick_tile(tp, (128, 64))
    tk_a = _pick_tile(tp, (512, 256, 128))
    tq_b = _pick_tile(tp, (256, 128, 64))
    tk_b = min(tq_b, 256)
    tq_c = _pick_tile(tp, (512, 256, 128))
    tk_c = _pick_tile(tp, (512, 256, 128))
    tq_m = _pick_tile(tp, (512, 256, 128))
    sp_a, sp_b, sp_c = _round_up(s_new, tk_a), _round_up(s_new, tk_b), _round_up(s_new, tk_c)

    far2 = _far_bias(rel_bias) * LOG2E
    near_a_p = _near_bias(rel_bias, tq_a, tk_a, 0, tp // tq_a) * LOG2E
    near_a_s = _near_bias(rel_bias, ts, tk_a, past_len, 1) * LOG2E
    near_c_p = _near_bias(rel_bias, tq_c, tk_c, 0, tp // tq_c) * LOG2E
    near_c_s = _near_bias(rel_bias, ts, tk_c, past_len, 1) * LOG2E
    pair = lambda nb: nb.reshape(nb.shape[0], hc, 2, *nb.shape[2:])

    w_mem = jnp.moveaxis(w_mem_kv, 0, 1).reshape(d, depth * 2 * mem_w).astype(BF16)
    mem_f, mem_b = _matmul(mem_prompt.reshape(bp * n_mem, d).astype(BF16), w_mem)
    mem_f = mem_f.reshape(bp, n_mem, depth, 2, h_mem, hd)
    mem_k_p = jnp.moveaxis(mem_f[:, :, :, 0], 2, 0)
    mem_v_p = jnp.moveaxis(mem_f[:, :, :, 1], 2, 0)
    mem_s_k = cache_mem_k.reshape(depth, bs * n_mem, mem_w).astype(BF16)
    mem_s_v = cache_mem_v.reshape(depth, bs * n_mem, mem_w).astype(BF16)

    x_f = jnp.concatenate([x_prompt.reshape(mp, d), x_sample.reshape(ms, d)], axis=0)
    x_b = x_f.astype(BF16)
    rows_p = [None] * depth
    rows_s = [None] * depth
    rows_buf = None

    for i in range(depth):
        kind, j = i % N_MIXERS, i // N_MIXERS
        h_b_ = _matmul_bf16(x_b, (w_a, w_b, w_c)[kind][j])
        w_k, w_v = ((w_a_k, w_a_v), (w_b_k, w_b_v), (w_c_k, w_c_v))[kind]
        k_shape = ((n_kv, hd), (h_b, hd), (hc, 2, hd))[kind]
        v_shape = ((n_kv, hd), (h_b, hd), (hc, 2 * hd))[kind]
        kp_f, kp_b = _proj_heads(x_b, 0, mp, w_k[j], k_shape)
        vp_f, vp_b = _proj_heads(x_b, 0, mp, w_v[j], v_shape)
        ks_f, ks_b = _proj_heads(x_b, mp, ms, w_k[j], k_shape)
        vs_f, vs_b = _proj_heads(x_b, mp, ms, w_v[j], v_shape)
        k_new, v_new = ks_b.reshape(bs, ts, -1), vs_b.reshape(bs, ts, -1)
        rows_p[i] = (kp_f.reshape(bp, tp, *k_shape), vp_f.reshape(bp, tp, *v_shape))
        rows_s[i] = (ks_f.reshape(bs, ts, *k_shape), vs_f.reshape(bs, ts, *v_shape))
        rb_s = mp // ts

        if kind == 0:
            k_sel_p = min(K_SEL_MAX, tp // 4)
            k_sel_s = min(K_SEL_MAX, s_new // 4)
            dims = dict(n_heads=h_a, n_kv=n_kv, hd=hd, n_idx=n_idx, d_idx=d_idx)
            mix_p = _dsa_attn(
                (h_b_, 0, 0, a_qi // qi_w, a_kw // LANE),
                (kp_b, 0, vp_b, 0, h_b_, a_kw // LANE),
                near_a_p, far2, n_batch=bp, t=tp, s_pad=tp, s_valid=tp, q_off=0,
                tq=tq_a, tk=tk_a, k_sel=k_sel_p, **dims)
            ki_f, ki_b = _matmul(x_b, w_ki[j])
            kiw_new = jnp.pad(ki_b[mp:].reshape(bs, ts, d_idx), ((0, 0), (0, 0), (0, LANE - d_idx)))
            kiw_past = jnp.pad(cache_a_kidx[j], ((0, 0), (0, 0), (0, LANE - d_idx)))
            k_all = _keys_with_past(cache_a_k[j].reshape(bs, past_len, kv_w), k_new, sp_a)
            v_all = _keys_with_past(cache_a_v[j].reshape(bs, past_len, kv_w), v_new, sp_a)
            ki_all = _keys_with_past(kiw_past, kiw_new, sp_a)
            mix_s = _dsa_attn(
                (h_b_, rb_s, 0, a_qi // qi_w, a_kw // LANE),
                (k_all, 0, v_all, 0, ki_all, 0),
                near_a_s, far2, n_batch=bs, t=ts, s_pad=sp_a, s_valid=s_new, q_off=past_len,
                tq=ts, tk=tk_a, k_sel=k_sel_s, **dims)
            mq_cb = a_mq // mem_w
            rows_p[i] += (ki_f[:mp].reshape(bp, tp, d_idx),)
            rows_s[i] += (ki_f[mp:].reshape(bs, ts, d_idx),)
        elif kind == 1:
            dims = dict(n_heads=h_b, hd=hd)
            mix_p = _stick_attn((h_b_, 0, 0), (kp_b, 0, vp_b, 0),
                                n_batch=bp, t=tp, s_pad=tp, q_off=0, tq=tq_b, tk=tk_b, **dims)
            k_all = _keys_with_past(cache_b_k[j].reshape(bs, past_len, tok_w), k_new, sp_b)
            v_all = _keys_with_past(cache_b_v[j].reshape(bs, past_len, tok_w), v_new, sp_b)
            mix_s = _stick_attn((h_b_, rb_s, 0), (k_all, 0, v_all, 0),
                                n_batch=bs, t=ts, s_pad=sp_b, q_off=past_len, tq=ts, tk=tk_b, **dims)
            mq_cb = tok_w // mem_w
        else:
            lambda_init = 0.8 - 0.6 * math.exp(-0.3 * i)
            lv = diff_lambda[j].astype(F32)
            lam = jnp.exp(jnp.sum(lv[0] * lv[1])) - jnp.exp(jnp.sum(lv[2] * lv[3])) + lambda_init
            scalars = jnp.concatenate([lam.reshape(1), far2]).astype(F32)
            g = diff_subln[j].reshape(1, 2 * hd).astype(F32)
            dims = dict(n_heads=hc, hd=hd, out_scale=1.0 - lambda_init)
            mix_p = _diff_attn((h_b_, 0, 0), (kp_b, 0, vp_b, 0), pair(near_c_p), scalars, g,
                               n_batch=bp, t=tp, s_pad=tp, s_valid=tp, q_off=0, tq=tq_c, tk=tk_c, **dims)
            k_all = _keys_with_past(cache_c_k[j].reshape(bs, past_len, tok_w), k_new, sp_c)
            v_all = _keys_with_past(cache_c_v[j].reshape(bs, past_len, tok_w), v_new, sp_c)
            mix_s = _diff_attn((h_b_, rb_s, 0), (k_all, 0, v_all, 0), pair(near_c_s), scalars, g,
                               n_batch=bs, t=ts, s_pad=sp_c, s_valid=s_new, q_off=past_len,
                               tq=ts, tk=tk_c, **dims)
            mq_cb = tok_w // mem_w

        mem_p = _mem_attn(h_b_, 0, mq_cb, mem_b[:, (2 * i) * mem_w:(2 * i + 1) * mem_w],
                          mem_b[:, (2 * i + 1) * mem_w:(2 * i + 2) * mem_w],
                          bp, tp, tq_m, h_mem, hd)
        mem_s = _mem_attn(h_b_, rb_s, mq_cb, mem_s_k[i], mem_s_v[i], bs, ts, ts, h_mem, hd)
        mix = jnp.concatenate([mix_p, mix_s], axis=0)
        mem_o = jnp.concatenate([mem_p, mem_s], axis=0)
        x_f, x_pk = _proj_ln(mix, mem_o, w_o_bf[i, :tok_w], w_o_bf[i, tok_w:], x_f,
                             ln_g[i, 0][None], ln_b[i, 0][None], alpha)
        y, dest, gates, rows_buf = _moe(x_f, x_pk, rows_buf, i, w_r_hi[i], w_r_lo[i], b_r_pad[i],
                                        w_gate_up, b_gu, w_down, b_dn)
        x_f, x_b = _combine_ln(x_f, y, dest, gates, ln_g[i, 1][None], ln_b[i, 1][None], alpha)

    def stack_kind(rows, kind):
        sel = [r for i, r in enumerate(rows) if i % N_MIXERS == kind]
        return [jnp.stack(parts) for parts in zip(*sel)]

    y_prompt = x_f[:mp].reshape(bp, tp, d)
    y_sample = x_f[mp:].reshape(bs, ts, d)
    a_k_p, a_v_p, a_kidx_p = stack_kind(rows_p, 0)
    b_k_p, b_v_p = stack_kind(rows_p, 1)
    c_k_p, c_v_p = stack_kind(rows_p, 2)
    a_k_s, a_v_s, a_kidx_s = stack_kind(rows_s, 0)
    b_k_s, b_v_s = stack_kind(rows_s, 1)
    c_k_s, c_v_s = stack_kind(rows_s, 2)
    return (y_prompt, y_sample, a_k_p, a_v_p, a_kidx_p, b_k_p, b_v_p, c_k_p, c_v_p, mem_k_p, mem_v_p,
            a_k_s, a_v_s, a_kidx_s, b_k_s, b_v_s, c_k_s, c_v_s)
```

```python
import functools
import math

import jax
import jax.numpy as jnp
from jax import lax
from jax.experimental import pallas as pl
from jax.experimental.pallas import tpu as pltpu

F32 = jnp.float32
BF16 = jnp.bfloat16
I32 = jnp.int32

CHUNK = 64
CHUNK_SHIFT = CHUNK.bit_length() - 1
K_SEL_MAX = 256
MAX_DISTANCE = 128
TOP_K = 4
SWIGLU_LIMIT = 7.0
SWIGLU_ALPHA = 1.702
LN_EPS = 1e-5
N_MIXERS = 3

LANE = 128
VMEM_LIMIT = 56 * 1024 * 1024
MASKED = -1e30
EXP_UNDERFLOW = -104.0
LOG2E = 1.4426950408889634
INT_MIN = -2 ** 31
KEY_NEG_INF = -2139095041
ALL_TIES = 2 ** 30
MOE_BLOCK_ROWS = (512, 128)
HI16 = -65536

def _cparams(n_axes):
    return pltpu.CompilerParams(dimension_semantics=("arbitrary",) * n_axes,
                                vmem_limit_bytes=VMEM_LIMIT)


def _round_up(x, m):
    return -(-x // m) * m


def _pick_tile(n, candidates):
    for c in candidates:
        if n % c == 0:
            return c
    return n


def _dot_nt(a, b):
    return lax.dot_general(a, b, (((1,), (1,)), ((), ())), preferred_element_type=F32)


def _resident(block_shape, index_map):
    return pl.BlockSpec(block_shape, index_map, pipeline_mode=pl.Buffered(1))


def _mm_kernel(x_ref, w_ref, of_ref, ob_ref):
    acc = jnp.dot(x_ref[...], w_ref[...], preferred_element_type=F32)
    of_ref[...] = acc
    ob_ref[...] = acc.astype(BF16)


def _matmul(x, w):
    m, k = x.shape
    n = w.shape[1]
    tm = _pick_tile(m, (640, 512, 256, 128, 64, 32, 16))
    tn = _pick_tile(n, (1280, 1024, 768, 512, 384, 256, 128))
    return pl.pallas_call(
        _mm_kernel,
        grid=(n // tn, m // tm),
        in_specs=[pl.BlockSpec((tm, k), lambda j, i: (i, 0)),
                  pl.BlockSpec((k, tn), lambda j, i: (0, j))],
        out_specs=[pl.BlockSpec((tm, tn), lambda j, i: (i, j)),
                   pl.BlockSpec((tm, tn), lambda j, i: (i, j))],
        out_shape=[jax.ShapeDtypeStruct((m, n), F32), jax.ShapeDtypeStruct((m, n), BF16)],
        compiler_params=_cparams(2),
        name="matmul",
    )(x, w)


def _mm_bf16_kernel(x_ref, w_ref, ob_ref):
    ob_ref[...] = jnp.dot(x_ref[...], w_ref[...], preferred_element_type=F32).astype(BF16)


def _matmul_bf16(x, w):
    m, k = x.shape
    n = w.shape[1]
    tm = _pick_tile(m, (640, 512, 256, 128, 64, 32, 16))
    tn = _pick_tile(n, (1280, 1024, 896, 768, 512, 384, 256, 128))
    return pl.pallas_call(
        _mm_bf16_kernel,
        grid=(n // tn, m // tm),
        in_specs=[pl.BlockSpec((tm, k), lambda j, i: (i, 0)),
                  pl.BlockSpec((k, tn), lambda j, i: (0, j))],
        out_specs=pl.BlockSpec((tm, tn), lambda j, i: (i, j)),
        out_shape=jax.ShapeDtypeStruct((m, n), BF16),
        compiler_params=_cparams(2),
        name="matmul_bf16",
    )(x, w)


def _proj_heads_kernel(x_ref, w_ref, of_ref, ob_ref, *, head_shape):
    acc = jnp.dot(x_ref[...], w_ref[...], preferred_element_type=F32)
    ob_ref[...] = acc.astype(BF16)
    slab = head_shape[-1]
    lead = head_shape[:-1]
    for c in range(math.prod(lead)):
        idx, rest = [], c
        for size in reversed(lead):
            idx.append(rest % size)
            rest //= size
        of_ref[(slice(None), *reversed(idx), slice(None))] = acc[:, c * slab:(c + 1) * slab]


def _proj_heads(x, row0, n_rows, w, head_shape):
    k = x.shape[1]
    n = w.shape[1]
    assert math.prod(head_shape) == n and head_shape[-1] % LANE == 0
    tm = _pick_tile(n_rows, (256, 128, 64, 32, 16))
    assert row0 % tm == 0
    zeros = (0,) * len(head_shape)
    return pl.pallas_call(
        functools.partial(_proj_heads_kernel, head_shape=tuple(head_shape)),
        grid=(n_rows // tm,),
        in_specs=[pl.BlockSpec((tm, k), lambda i: (row0 // tm + i, 0)),
                  pl.BlockSpec((k, n), lambda i: (0, 0))],
        out_specs=[pl.BlockSpec((tm, *head_shape), lambda i: (i, *zeros)),
                   pl.BlockSpec((tm, n), lambda i: (i, 0))],
        out_shape=[jax.ShapeDtypeStruct((n_rows, *head_shape), F32),
                   jax.ShapeDtypeStruct((n_rows, n), BF16)],
        compiler_params=_cparams(1),
        name="proj_heads",
    )(x, w)


def _layer_norm_rows(z, g, b):
    mu = jnp.mean(z, axis=-1, keepdims=True)
    zc = z - mu
    var = jnp.mean(zc * zc, axis=-1, keepdims=True)
    return zc * lax.rsqrt(var + LN_EPS) * g + b


def _pack_halves(x):
    half = x.shape[1] // 2
    xb = x.astype(BF16).astype(F32)
    lo = lax.shift_right_logical(pltpu.bitcast(xb[:, :half], I32), 16)
    return lo | (pltpu.bitcast(xb[:, half:], I32) & HI16)


def _unpack_halves(p):
    lo = pltpu.bitcast(p << 16, F32).astype(BF16)
    hi = pltpu.bitcast(p & HI16, F32).astype(BF16)
    return jnp.concatenate([lo, hi], axis=1)


def _proj_ln_kernel(a1_ref, a2_ref, w1_ref, w2_ref, r_ref, g_ref, b_ref, of_ref, op_ref, *, alpha):
    y = jnp.dot(a1_ref[...], w1_ref[...], preferred_element_type=F32)
    y = y + jnp.dot(a2_ref[...], w2_ref[...], preferred_element_type=F32)
    out = _layer_norm_rows(alpha * r_ref[...] + y, g_ref[...], b_ref[...])
    of_ref[...] = out
    op_ref[...] = _pack_halves(out)


def _proj_ln(a1, a2, w1, w2, res, g, b, alpha):
    m, k1 = a1.shape
    k2 = a2.shape[1]
    d = w1.shape[1]
    tm = _pick_tile(m, (256, 128, 64, 32, 16))
    row = lambda i: (i, 0)
    fix = lambda i: (0, 0)
    return pl.pallas_call(
        functools.partial(_proj_ln_kernel, alpha=alpha),
        grid=(m // tm,),
        in_specs=[pl.BlockSpec((tm, k1), row), pl.BlockSpec((tm, k2), row),
                  pl.BlockSpec((k1, d), fix), pl.BlockSpec((k2, d), fix),
                  pl.BlockSpec((tm, d), row), pl.BlockSpec((1, d), fix), pl.BlockSpec((1, d), fix)],
        out_specs=[pl.BlockSpec((tm, d), row), pl.BlockSpec((tm, d // 2), row)],
        out_shape=[jax.ShapeDtypeStruct((m, d), F32), jax.ShapeDtypeStruct((m, d // 2), I32)],
        compiler_params=_cparams(1),
        name="proj_ln",
    )(a1, a2, w1, w2, res, g, b)


def _row_copy(src_ref, src_row, dst_ref, dst_row, sem):
    return pltpu.make_async_copy(src_ref.at[pl.ds(src_row, 1)], dst_ref.at[pl.ds(dst_row, 1)], sem)


def _combine_ln_kernel(dest_ref, next_ref, x_ref, gate_ref, g_ref, b_ref, y_hbm, of_ref, ob_ref,
                       ybuf, sems, *, alpha, tm, n_steps):
    i = pl.program_id(0)
    slot = i % 2

    def gather(idx_ref, to_slot):
        def body(r, carry):
            for k in range(TOP_K):
                _row_copy(y_hbm, idx_ref[0, r * TOP_K + k], ybuf.at[to_slot, k], r,
                          sems.at[to_slot]).start()
            return carry
        lax.fori_loop(0, tm, body, 0)

    @pl.when(i == 0)
    def _():
        gather(dest_ref, 0)

    @pl.when(i + 1 < n_steps)
    def _():
        gather(next_ref, 1 - slot)

    def drain(r, carry):
        for k in range(TOP_K):
            _row_copy(y_hbm, 0, ybuf.at[slot, k], 0, sems.at[slot]).wait()
        return carry
    lax.fori_loop(0, tm, drain, 0)

    gates = gate_ref[...]
    ffn = gates[:, 0:1] * ybuf[slot, 0]
    for k in range(1, TOP_K):
        ffn = ffn + gates[:, k:k + 1] * ybuf[slot, k]
    out = _layer_norm_rows(alpha * x_ref[...] + ffn, g_ref[...], b_ref[...])
    of_ref[...] = out
    ob_ref[...] = out.astype(BF16)


def _combine_ln(x, y, dest, gates, g, b, alpha):
    m, d = x.shape
    tm = _pick_tile(m, (64, 32, 16))
    n_steps = m // tm
    dest2 = dest.reshape(n_steps, 1, tm * TOP_K)
    row = lambda i: (i, 0)
    fix = lambda i: (0, 0)
    return pl.pallas_call(
        functools.partial(_combine_ln_kernel, alpha=alpha, tm=tm, n_steps=n_steps),
        grid=(n_steps,),
        in_specs=[pl.BlockSpec((None, 1, tm * TOP_K), lambda i: (i, 0, 0), memory_space=pltpu.SMEM),
                  pl.BlockSpec((None, 1, tm * TOP_K), lambda i: (jnp.minimum(i + 1, n_steps - 1), 0, 0),
                               memory_space=pltpu.SMEM),
                  pl.BlockSpec((tm, d), row), pl.BlockSpec((tm, LANE), row),
                  pl.BlockSpec((1, d), fix), pl.BlockSpec((1, d), fix),
                  pl.BlockSpec(memory_space=pl.ANY)],
        out_specs=[pl.BlockSpec((tm, d), row), pl.BlockSpec((tm, d), row)],
        out_shape=[jax.ShapeDtypeStruct((m, d), F32), jax.ShapeDtypeStruct((m, d), BF16)],
        scratch_shapes=[pltpu.VMEM((2, TOP_K, tm, d), F32), pltpu.SemaphoreType.DMA((2,))],
        compiler_params=_cparams(1),
        name="combine_ln",
    )(dest2, dest2, x, gates, g, b, y)


def _mem_attn_kernel(q_ref, k_ref, v_ref, o_ref, *, n_heads, hd):
    scale = hd ** -0.5
    for h in range(n_heads):
        cols = slice(h * hd, (h + 1) * hd)
        s = _dot_nt(q_ref[:, cols], k_ref[:, cols]) * scale
        p = jnp.exp(s - jnp.max(s, axis=-1, keepdims=True))
        p = p / jnp.sum(p, axis=-1, keepdims=True)
        o_ref[:, cols] = jnp.dot(p.astype(BF16), v_ref[:, cols],
                                 preferred_element_type=F32).astype(BF16)


def _mem_attn(qarr, q_rb0, q_cb, mk, mv, n_batch, t, tq, n_heads, hd):
    w = n_heads * hd
    n_mem = mk.shape[0] // n_batch
    nqb = t // tq
    return pl.pallas_call(
        functools.partial(_mem_attn_kernel, n_heads=n_heads, hd=hd),
        grid=(n_batch, nqb),
        in_specs=[pl.BlockSpec((tq, w), lambda b, i: (q_rb0 + b * nqb + i, q_cb)),
                  pl.BlockSpec((n_mem, w), lambda b, i: (b, 0)),
                  pl.BlockSpec((n_mem, w), lambda b, i: (b, 0))],
        out_specs=pl.BlockSpec((tq, w), lambda b, i: (b * nqb + i, 0)),
        out_shape=jax.ShapeDtypeStruct((n_batch * t, w), BF16),
        compiler_params=_cparams(2),
        name="mem_attn",
    )(qarr, mk, mv)


def _t5_bucket(rel, num_buckets):
    half = num_buckets // 2
    exact = half // 2
    n = jnp.abs(rel)
    far = exact + (jnp.log(jnp.maximum(n, 1).astype(F32) / exact)
                   / math.log(MAX_DISTANCE / exact) * (half - exact)).astype(I32)
    return jnp.where(rel > 0, half, 0) + jnp.where(n < exact, n, jnp.minimum(far, half - 1))


def _near_phase(q_off, tq, tk, nqb, i):
    full = tk // tq
    return (q_off // tq + i) % full if nqb >= full else i


def _near_bias(rel_bias, tq, tk, q_off, nqb):
    full = tk // tq
    n_maps = rel_bias.shape[1]
    rel_lo = -(2 * tk + tq)
    rels = jnp.arange(rel_lo, tk + 1, dtype=I32)
    by_rel = jnp.moveaxis(rel_bias[_t5_bucket(rels, rel_bias.shape[0])], -1, 0).astype(F32)
    n = tq + tk
    tiles = []
    for p in range(min(full, nqb)):
        q_in_block = (p if nqb >= full else (q_off // tq + p) % full) * tq
        per_near = []
        for near in range(2):
            rel0 = -near * tk - q_in_block
            start = rel0 - rel_lo
            v = by_rel[:, start - tq:start + tk]
            u = jnp.roll(v, -tq, axis=1)
            skew = jnp.tile(u, (1, tq))[:, :tq * (n - 1)].reshape(n_maps, tq, n - 1)
            per_near.append(skew[:, :, :tk])
        tiles.append(jnp.stack(per_near, axis=1))
    return jnp.stack(tiles)


def _far_bias(rel_bias):
    return rel_bias[rel_bias.shape[0] // 2 - 1].astype(F32)


def _dsa_kernel(far_ref, q_ref, qi_ref, qw_ref, k_ref, v_ref, ki_ref, bd_ref, o_ref,
                key_scr, wb_scr, m_scr, acc_scr, *,
                tq, tk, q_off, s_valid, k_sel, n_heads, n_kv, hd, n_idx, d_idx, idx_bits):
    gqa = n_heads // n_kv
    reps = tk // LANE
    i = pl.program_id(1)
    q0 = q_off + i * tq
    kb_diag = q0 // tk
    nkb = kb_diag + 1
    q_pos = q0 + lax.broadcasted_iota(I32, (tq, 1), 0)
    lane = lax.broadcasted_iota(I32, (1, tk), 1)
    lane1 = lax.broadcasted_iota(I32, (1, LANE), 1)
    tile = lambda x: jnp.concatenate([x] * reps, axis=1)

    qi = qi_ref[...]
    qw = qw_ref[...].astype(F32)
    qi_heads = [qi[:, h * d_idx:(h + 1) * d_idx] for h in range(n_idx)]
    for h in range(n_idx):
        wb_scr[h] = jnp.broadcast_to(qw[:, d_idx + h:d_idx + h + 1] * (d_idx ** -0.5), (tq, LANE))

    def score_block(kb, carry):
        ks = pl.multiple_of(kb * tk, tk)
        ki = ki_ref[pl.ds(ks, tk), :][:, :d_idx]
        sc = jnp.zeros((tq, tk), F32)
        for h in range(n_idx):
            sc = sc + jnp.maximum(_dot_nt(qi_heads[h], ki), 0.0) * tile(wb_scr[h])
        sc = sc * (n_idx ** -0.5)
        sc = jnp.where(sc == 0.0, 0.0, sc)
        k_pos = ks + lane
        allowed = ((k_pos >> CHUNK_SHIFT) <= (q_pos >> CHUNK_SHIFT)) & (k_pos < s_valid)
        sc = jnp.where(allowed, sc, -jnp.inf)
        bits = pltpu.bitcast(sc, I32)
        key_scr[kb] = bits ^ ((bits >> 31) & 0x7FFFFFFF)
        return carry

    lax.fori_loop(0, nkb, score_block, 0)

    def count(pred):
        def body(kb, acc):
            key = key_scr[kb]
            for c in range(reps):
                acc = acc + jnp.where(pred(kb, c, key[:, c * LANE:(c + 1) * LANE]), 1.0, 0.0)
            return acc
        acc = lax.fori_loop(0, nkb, body, jnp.zeros((tq, LANE), F32))
        return jnp.sum(acc, axis=1, keepdims=True)

    def bisect_cond(state):
        it, _, _, open_rows = state
        return (it < 32) & (open_rows > 0.0)

    def bisect_body(state):
        it, t_u, cnt_t, _ = state
        cand_u = t_u | (jnp.int32(1) << (31 - it))
        cand_b = jnp.broadcast_to(cand_u ^ INT_MIN, (tq, LANE))
        cnt = count(lambda kb, c, key: key >= cand_b)
        take = cnt >= k_sel
        t_u = jnp.where(take, cand_u, t_u)
        cnt_t = jnp.where(take, cnt, cnt_t)
        return it + 1, t_u, cnt_t, jnp.max(jnp.where(cnt_t > k_sel, 1.0, 0.0))

    total = (nkb * tk).astype(F32)
    _, t_u, cnt_t, _ = lax.while_loop(
        bisect_cond, bisect_body,
        (jnp.int32(0), jnp.zeros((tq, 1), I32), jnp.full((tq, 1), total, F32),
         jnp.where(total > k_sel, 1.0, 0.0)))
    thr = t_u ^ INT_MIN
    thr_b = jnp.broadcast_to(thr, (tq, LANE))

    def tie_search():
        need = k_sel - count(lambda kb, c, key: key > thr_b)

        def index_bit(it, j_hi):
            cand = j_hi | (jnp.int32(1) << (idx_bits - 1 - it))
            cand_b = jnp.broadcast_to(cand, (tq, LANE))
            cnt = count(lambda kb, c, key: (key == thr_b) & (kb * tk + c * LANE + lane1 < cand_b))
            return jnp.where(cnt < need, cand, j_hi)

        return lax.fori_loop(0, idx_bits, index_bit, jnp.zeros((tq, 1), I32))

    tie_hi = lax.cond(jnp.max(cnt_t) > k_sel, tie_search, lambda: jnp.full((tq, 1), ALL_TIES, I32))

    m_scr[...] = jnp.full(m_scr.shape, MASKED, F32)
    acc_scr[...] = jnp.zeros(acc_scr.shape, F32)
    scale2 = (hd ** -0.5) * LOG2E
    q_groups = [jnp.concatenate([q_ref[:, (g * gqa + r) * hd:(g * gqa + r + 1) * hd]
                                 for r in range(gqa)], axis=0) for g in range(n_kv)]
    far_rows = [jnp.concatenate([jnp.full((tq, LANE), far_ref[g * gqa + r], F32)
                                 for r in range(gqa)], axis=0) for g in range(n_kv)]
    ones = jnp.ones((tk, hd), BF16)

    def attend_block(kb, near):
        ks = pl.multiple_of(kb * tk, tk)
        key = key_scr[kb]
        sel = (key > thr) | ((key == thr) & (ks + lane <= tie_hi))
        sel = sel & (key > KEY_NEG_INF)
        mbias = jnp.where(sel, 0.0, MASKED)
        for g in range(n_kv):
            kg = k_ref[pl.ds(ks, tk), g * hd:(g + 1) * hd]
            vext = jnp.concatenate([v_ref[pl.ds(ks, tk), g * hd:(g + 1) * hd], ones], axis=1)
            s = _dot_nt(q_groups[g], kg) * scale2
            m_old = m_scr[g]
            if near is None:
                t = s + jnp.concatenate([mbias] * gqa, axis=0)
                m_new = jnp.maximum(m_old, jnp.max(t, axis=-1, keepdims=True) + far_rows[g])
                p = jnp.exp2(t - tile(m_new - far_rows[g]))
            else:
                t = s + jnp.concatenate([mbias + bd_ref[g * gqa + r, near] for r in range(gqa)], axis=0)
                m_new = jnp.maximum(m_old, jnp.max(t, axis=-1, keepdims=True))
                p = jnp.exp2(t - tile(m_new))
            alpha = jnp.exp2(m_old - m_new)
            acc_scr[g] = (jnp.concatenate([alpha, alpha], axis=1) * acc_scr[g]
                          + jnp.dot(p.astype(BF16), vext, preferred_element_type=F32))
            m_scr[g] = m_new

    def far_block(kb, carry):
        attend_block(kb, None)
        return carry

    lax.fori_loop(0, jnp.maximum(kb_diag - 1, 0), far_block, 0)

    @pl.when(kb_diag >= 1)
    def _():
        attend_block(kb_diag - 1, 1)

    attend_block(kb_diag, 0)

    for g in range(n_kv):
        acc = acc_scr[g]
        o = acc[:, :hd] / acc[:, hd:]
        for r in range(gqa):
            h = g * gqa + r
            o_ref[:, h * hd:(h + 1) * hd] = o[r * tq:(r + 1) * tq].astype(BF16)


def _dsa_attn(q_src, k_src, bias_near, bias_far, *, n_batch, t, s_pad, s_valid, q_off, tq, tk,
              k_sel, n_heads, n_kv, hd, n_idx, d_idx):
    qarr, q_rb0, q_cb, qi_cb, qw_cb = q_src
    karr, k_cb, varr, v_cb, kiarr, ki_cb = k_src
    assert tk >= MAX_DISTANCE and tk % tq == 0 and q_off % tq == 0 and s_pad % tk == 0
    nqb = t // tq
    nkb_max = s_pad // tk
    gqa = n_heads // n_kv
    wq, wkv, wqi = n_heads * hd, n_kv * hd, n_idx * d_idx
    qrow = lambda b, i: q_rb0 + b * nqb + i
    kernel = functools.partial(
        _dsa_kernel, tq=tq, tk=tk, q_off=q_off, s_valid=s_valid, k_sel=k_sel, n_heads=n_heads,
        n_kv=n_kv, hd=hd, n_idx=n_idx, d_idx=d_idx, idx_bits=max((s_pad - 1).bit_length(), 1))
    return pl.pallas_call(
        kernel,
        grid=(n_batch, nqb),
        in_specs=[pl.BlockSpec(memory_space=pltpu.SMEM),
                  pl.BlockSpec((tq, wq), lambda b, i: (qrow(b, i), q_cb)),
                  pl.BlockSpec((tq, wqi), lambda b, i: (qrow(b, i), qi_cb)),
                  pl.BlockSpec((tq, LANE), lambda b, i: (qrow(b, i), qw_cb)),
                  _resident((s_pad, wkv), lambda b, i: (b, k_cb)),
                  _resident((s_pad, wkv), lambda b, i: (b, v_cb)),
                  _resident((s_pad, LANE), lambda b, i: (b, ki_cb)),
                  pl.BlockSpec((None, n_heads, 2, tq, tk),
                               lambda b, i: (_near_phase(q_off, tq, tk, nqb, i), 0, 0, 0, 0))],
        out_specs=pl.BlockSpec((tq, wq), lambda b, i: (b * nqb + i, 0)),
        out_shape=jax.ShapeDtypeStruct((n_batch * t, wq), BF16),
        scratch_shapes=[pltpu.VMEM((nkb_max, tq, tk), I32),
                        pltpu.VMEM((n_idx, tq, LANE), F32),
                        pltpu.VMEM((n_kv, gqa * tq, LANE), F32),
                        pltpu.VMEM((n_kv, gqa * tq, 2 * hd), F32)],
        compiler_params=_cparams(2),
        name="dsa_attn",
    )(bias_far, qarr, qarr, qarr, karr, varr, kiarr, bias_near)


def _stick_kernel(q_ref, k_ref, v_ref, o_ref, carry_scr, acc_scr, *, tq, tk, q_off, hd, hps):
    i = pl.program_id(2)
    q0 = q_off + i * tq
    q_pos = q0 + lax.broadcasted_iota(I32, (tq, 1), 0)
    lane = lax.broadcasted_iota(I32, (1, tk), 1)
    tri = (lax.broadcasted_iota(I32, (tk, tk), 0) > lax.broadcasted_iota(I32, (tk, tk), 1)).astype(BF16)
    scale = hd ** -0.5
    carry_scr[...] = jnp.zeros(carry_scr.shape, F32)
    acc_scr[...] = jnp.zeros(acc_scr.shape, F32)

    def cond(state):
        kb, carry_max = state
        return (kb >= 0) & (carry_max > EXP_UNDERFLOW)

    def body(state):
        kb, _ = state
        ks = pl.multiple_of(kb * tk, tk)
        before = (ks + lane) < q_pos
        carry_max = None
        for h in range(hps):
            cols = slice(h * hd, (h + 1) * hd)
            z = _dot_nt(q_ref[:, cols], k_ref[pl.ds(ks, tk), cols]) * scale
            log_beta = jnp.minimum(z, 0.0) - jnp.log(1.0 + jnp.exp(-jnp.abs(z)))
            log_keep = jnp.where(before, log_beta - z, 0.0)
            hi = log_keep.astype(BF16)
            lo = (log_keep - hi.astype(F32)).astype(BF16)
            tail = (jnp.dot(hi, tri, preferred_element_type=F32)
                    + jnp.dot(lo, tri, preferred_element_type=F32))
            carry = carry_scr[h]
            w = jnp.where(before, jnp.exp(log_beta + tail + carry), 0.0)
            acc_scr[h] += jnp.dot(w.astype(BF16), v_ref[pl.ds(ks, tk), cols], preferred_element_type=F32)
            carry = carry + jnp.sum(log_keep, axis=-1, keepdims=True)
            carry_scr[h] = carry
            head_max = jnp.max(carry)
            carry_max = head_max if carry_max is None else jnp.maximum(carry_max, head_max)
        return kb - 1, carry_max

    lax.while_loop(cond, body, ((q0 + tq - 1) // tk, jnp.float32(0.0)))
    for h in range(hps):
        o_ref[:, h * hd:(h + 1) * hd] = acc_scr[h].astype(BF16)


def _stick_attn(q_src, k_src, *, n_batch, t, s_pad, q_off, tq, tk, n_heads, hd):
    qarr, q_rb0, q_cb0 = q_src
    karr, k_cb0, varr, v_cb0 = k_src
    nqb = t // tq
    hps = 2 if all(v % 2 == 0 for v in (n_heads, q_cb0, k_cb0, v_cb0)) else 1
    w = hps * hd
    return pl.pallas_call(
        functools.partial(_stick_kernel, tq=tq, tk=tk, q_off=q_off, hd=hd, hps=hps),
        grid=(n_batch, n_heads // hps, nqb),
        in_specs=[pl.BlockSpec((tq, w), lambda b, h, i: (q_rb0 + b * nqb + i, q_cb0 // hps + h)),
                  pl.BlockSpec((s_pad, w), lambda b, h, i: (b, k_cb0 // hps + h)),
                  pl.BlockSpec((s_pad, w), lambda b, h, i: (b, v_cb0 // hps + h))],
        out_specs=pl.BlockSpec((tq, w), lambda b, h, i: (b * nqb + i, h)),
        out_shape=jax.ShapeDtypeStruct((n_batch * t, n_heads * hd), BF16),
        scratch_shapes=[pltpu.VMEM((hps, tq, 1), F32), pltpu.VMEM((hps, tq, hd), F32)],
        compiler_params=_cparams(3),
        name="stick_attn",
    )(qarr, karr, varr)


def _diff_kernel(scal_ref, q_ref, k_ref, v_ref, bd_ref, g_ref, o_ref, m_scr, acc_scr, *,
                 tq, tk, q_off, s_valid, hd, out_scale):
    h = pl.program_id(1)
    i = pl.program_id(2)
    reps = tk // LANE
    q0 = q_off + i * tq
    kb_diag = q0 // tk
    q_pos = q0 + lax.broadcasted_iota(I32, (tq, 1), 0)
    lane = lax.broadcasted_iota(I32, (1, tk), 1)
    tile = lambda x: jnp.concatenate([x] * reps, axis=1)
    scale2 = (hd ** -0.5) * LOG2E
    m_scr[...] = jnp.full(m_scr.shape, MASKED, F32)
    acc_scr[...] = jnp.zeros(acc_scr.shape, F32)
    q_maps = [q_ref[:, m * hd:(m + 1) * hd] for m in range(2)]
    ones = jnp.ones((tk, hd), BF16)

    def attend_block(kb, near):
        ks = pl.multiple_of(kb * tk, tk)
        vext = jnp.concatenate([v_ref[pl.ds(ks, tk), :], ones], axis=1)
        if near is not None:
            k_pos = ks + lane
            allowed = ((k_pos >> CHUNK_SHIFT) <= (q_pos >> CHUNK_SHIFT)) & (k_pos < s_valid)
            mbias = jnp.where(allowed, 0.0, MASKED)
        for m in range(2):
            s = _dot_nt(q_maps[m], k_ref[pl.ds(ks, tk), m * hd:(m + 1) * hd]) * scale2
            m_old = m_scr[m]
            if near is None:
                far = scal_ref[1 + 2 * h + m]
                m_new = jnp.maximum(m_old, jnp.max(s, axis=-1, keepdims=True) + far)
                p = jnp.exp2(s - tile(m_new - far))
            else:
                t = s + (mbias + bd_ref[m, near])
                m_new = jnp.maximum(m_old, jnp.max(t, axis=-1, keepdims=True))
                p = jnp.exp2(t - tile(m_new))
            alpha = jnp.exp2(m_old - m_new)
            acc_scr[m] = (jnp.concatenate([alpha] * 3, axis=1) * acc_scr[m]
                          + jnp.dot(p.astype(BF16), vext, preferred_element_type=F32))
            m_scr[m] = m_new

    n_far = jnp.maximum(kb_diag - 1, 0)

    def far_pair(pair, carry):
        attend_block(2 * pair, None)
        attend_block(2 * pair + 1, None)
        return carry

    lax.fori_loop(0, n_far // 2, far_pair, 0)

    @pl.when(n_far % 2 == 1)
    def _():
        attend_block(n_far - 1, None)

    @pl.when(kb_diag >= 1)
    def _():
        attend_block(kb_diag - 1, 1)

    attend_block(kb_diag, 0)

    def normalised(m):
        acc = acc_scr[m]
        return acc[:, :2 * hd] / jnp.concatenate([acc[:, 2 * hd:]] * 2, axis=1)

    o = normalised(0) - scal_ref[0] * normalised(1)
    o = o * lax.rsqrt(jnp.mean(o * o, axis=-1, keepdims=True) + LN_EPS) * g_ref[...]
    o_ref[...] = (o * out_scale).astype(BF16)


def _diff_attn(q_src, k_src, bias_near, scalars, subln_g, *, n_batch, t, s_pad, s_valid, q_off,
               tq, tk, n_heads, hd, out_scale):
    qarr, q_rb0, q_cb0 = q_src
    karr, k_cb0, varr, v_cb0 = k_src
    assert tk >= MAX_DISTANCE and tk % tq == 0 and q_off % tq == 0 and s_pad % tk == 0
    nqb = t // tq
    w = 2 * hd
    return pl.pallas_call(
        functools.partial(_diff_kernel, tq=tq, tk=tk, q_off=q_off, s_valid=s_valid, hd=hd,
                          out_scale=out_scale),
        grid=(n_batch, n_heads, nqb),
        in_specs=[pl.BlockSpec(memory_space=pltpu.SMEM),
                  pl.BlockSpec((tq, w), lambda b, h, i: (q_rb0 + b * nqb + i, q_cb0 + h)),
                  pl.BlockSpec((s_pad, w), lambda b, h, i: (b, k_cb0 + h)),
                  pl.BlockSpec((s_pad, w), lambda b, h, i: (b, v_cb0 + h)),
                  pl.BlockSpec((None, None, 2, 2, tq, tk),
                               lambda b, h, i: (_near_phase(q_off, tq, tk, nqb, i), h, 0, 0, 0, 0)),
                  pl.BlockSpec((1, w), lambda b, h, i: (0, 0))],
        out_specs=pl.BlockSpec((tq, w), lambda b, h, i: (b * nqb + i, h)),
        out_shape=jax.ShapeDtypeStruct((n_batch * t, n_heads * w), BF16),
        scratch_shapes=[pltpu.VMEM((2, tq, LANE), F32), pltpu.VMEM((2, tq, w + hd), F32)],
        compiler_params=_cparams(3),
        name="diff_attn",
    )(scalars, qarr, karr, varr, bias_near, subln_g)


def _router_kernel(x_ref, wh_ref, wl_ref, b_ref, e_ref, g_ref, r_ref, c_ref, seen_scr):
    @pl.when(pl.program_id(0) == 0)
    def _():
        seen_scr[...] = jnp.zeros(seen_scr.shape, F32)

    x = x_ref[...]
    xh = x.astype(BF16)
    xl = (x - xh.astype(F32)).astype(BF16)
    wh = wh_ref[...]
    logits = (jnp.dot(xh, wh, preferred_element_type=F32)
              + jnp.dot(xl, wh, preferred_element_type=F32)
              + jnp.dot(xh, wl_ref[...], preferred_element_type=F32)) + b_ref[...]
    tm = logits.shape[0]
    lane = lax.broadcasted_iota(I32, logits.shape, 1)
    lane_f = lane.astype(F32)
    experts = jnp.zeros(logits.shape, I32)
    gates = jnp.zeros(logits.shape, F32)
    chosen = []
    top = None
    for k in range(TOP_K):
        m = jnp.max(logits, axis=-1, keepdims=True)
        ix = jnp.min(jnp.where(logits == m, lane_f, float(LANE)), axis=-1, keepdims=True).astype(I32)
        top = m if top is None else top
        experts = jnp.where(lane == k, ix, experts)
        gates = jnp.where(lane == k, jnp.exp(m - top), gates)
        chosen.append(lane == ix)
        logits = jnp.where(chosen[-1], -jnp.inf, logits)
    e_ref[...] = experts
    g_ref[...] = gates / jnp.sum(gates, axis=-1, keepdims=True)

    hot = functools.reduce(jnp.logical_or, chosen)
    hot_f = jnp.where(hot, 1.0, 0.0)
    lower = (lax.broadcasted_iota(I32, (tm, tm), 0) > lax.broadcasted_iota(I32, (tm, tm), 1)).astype(BF16)
    earlier = jnp.dot(lower, hot_f.astype(BF16), preferred_element_type=F32) + seen_scr[...]
    ranks = jnp.zeros(logits.shape, F32)
    for k in range(TOP_K):
        rank_k = jnp.sum(jnp.where(chosen[k], earlier, 0.0), axis=-1, keepdims=True)
        ranks = jnp.where(lane == k, rank_k, ranks)
    r_ref[...] = ranks.astype(I32)
    seen = seen_scr[...] + jnp.sum(hot_f, axis=0, keepdims=True)
    seen_scr[...] = seen
    c_ref[...] = seen


def _router(x, w_hi, w_lo, bias):
    m, d = x.shape
    tm = _pick_tile(m, (256, 128, 64, 32, 16))
    row = lambda i: (i, 0)
    fix = lambda i: (0, 0)
    return pl.pallas_call(
        _router_kernel,
        grid=(m // tm,),
        in_specs=[pl.BlockSpec((tm, d), row), pl.BlockSpec((d, LANE), fix),
                  pl.BlockSpec((d, LANE), fix), pl.BlockSpec((1, LANE), fix)],
        out_specs=[pl.BlockSpec((tm, LANE), row), pl.BlockSpec((tm, LANE), row),
                   pl.BlockSpec((tm, LANE), row), pl.BlockSpec((1, LANE), fix)],
        out_shape=[jax.ShapeDtypeStruct((m, LANE), I32), jax.ShapeDtypeStruct((m, LANE), F32),
                   jax.ShapeDtypeStruct((m, LANE), I32), jax.ShapeDtypeStruct((1, LANE), F32)],
        scratch_shapes=[pltpu.VMEM((1, LANE), F32)],
        compiler_params=_cparams(1),
        name="router",
    )(x, w_hi, w_lo, bias)


def _dispatch_kernel(dest_ref, x_ref, rows_in, rows_out, sem, *, tm):
    del rows_in

    def send(r, carry):
        for k in range(TOP_K):
            _row_copy(x_ref, r, rows_out, dest_ref[0, r * TOP_K + k], sem).start()
        return carry
    lax.fori_loop(0, tm, send, 0)

    def drain(r, carry):
        for k in range(TOP_K):
            _row_copy(x_ref, 0, rows_out, 0, sem).wait()
        return carry
    lax.fori_loop(0, tm, drain, 0)


def _dispatch(x_packed, dest, rows_buf):
    m, w = x_packed.shape
    tm = _pick_tile(m, (256, 128, 64, 32, 16))
    n_steps = m // tm
    return pl.pallas_call(
        functools.partial(_dispatch_kernel, tm=tm),
        grid=(n_steps,),
        in_specs=[pl.BlockSpec((None, 1, tm * TOP_K), lambda i: (i, 0, 0), memory_space=pltpu.SMEM),
                  pl.BlockSpec((tm, w), lambda i: (i, 0)),
                  pl.BlockSpec(memory_space=pl.ANY)],
        out_specs=pl.BlockSpec(memory_space=pl.ANY),
        out_shape=jax.ShapeDtypeStruct(rows_buf.shape, I32),
        scratch_shapes=[pltpu.SemaphoreType.DMA(())],
        input_output_aliases={2: 0},
        compiler_params=_cparams(1),
        name="dispatch",
    )(dest.reshape(n_steps, 1, tm * TOP_K), x_packed, rows_buf)


def _cast_rows(src_ref, dst_ref, rows_per_step):
    def body(c, carry):
        r = pl.multiple_of(c * rows_per_step, rows_per_step)
        dst_ref[pl.ds(r, rows_per_step), :] = src_ref[pl.ds(r, rows_per_step), :].astype(BF16)
        return carry
    lax.fori_loop(0, src_ref.shape[0] // rows_per_step, body, 0)


def _expert_changed(be_ref, j):
    return (j == 0) | (be_ref[j] != be_ref[jnp.maximum(j - 1, 0)])


def _moe_up_kernel(be_ref, nu_ref, x_ref, wg_ref, wl_ref, bg_ref, bl_ref, h_ref, wg_bf, wl_bf, *,
                   cast_rows):
    j = pl.program_id(1)

    @pl.when(_expert_changed(be_ref, j))
    def _():
        _cast_rows(wg_ref, wg_bf, cast_rows)
        _cast_rows(wl_ref, wl_bf, cast_rows)

    def compute(n):
        x = _unpack_halves(x_ref[:n])
        glu = jnp.dot(x, wg_bf[...], preferred_element_type=F32) + bg_ref[...]
        lin = jnp.dot(x, wl_bf[...], preferred_element_type=F32) + bl_ref[...]
        glu = jnp.minimum(glu, SWIGLU_LIMIT)
        lin = jnp.clip(lin, -SWIGLU_LIMIT, SWIGLU_LIMIT)
        h_ref[:n] = (glu * jax.nn.sigmoid(SWIGLU_ALPHA * glu) * (lin + 1.0)).astype(BF16)

    _by_block_fill(nu_ref[1 + j], h_ref, compute)


def _by_block_fill(rows, out_ref, compute):
    tm = out_ref.shape[0]
    half = tm // 2

    @pl.when(rows > half)
    def _():
        compute(tm)

    @pl.when((rows > 0) & (rows <= half))
    def _():
        compute(half)
        out_ref[half:] = jnp.zeros((tm - half,) + out_ref.shape[1:], out_ref.dtype)

    @pl.when(rows == 0)
    def _():
        out_ref[...] = jnp.zeros(out_ref.shape, out_ref.dtype)


def _moe_down_kernel(be_ref, nu_ref, h_ref, wd_ref, bd_ref, y_ref, wd_bf, *, cast_rows):
    j = pl.program_id(1)

    @pl.when(_expert_changed(be_ref, j))
    def _():
        _cast_rows(wd_ref, wd_bf, cast_rows)

    def compute(n):
        y_ref[:n] = jnp.dot(h_ref[:n], wd_bf[...], preferred_element_type=F32) + bd_ref[...]

    _by_block_fill(nu_ref[1 + j], y_ref, compute)


def _moe_experts(xs, block_e, n_used, layer, w_gate_up, b_gate_up, w_down, b_down, tm):
    n_rows = xs.shape[0]
    d = 2 * xs.shape[1]
    n_blocks = n_rows // tm
    d_ff = w_down.shape[2]
    tf = _pick_tile(d_ff, (1024, 512, 256, 128))
    nf = d_ff // tf
    tn = _pick_tile(d, (1024, 512, 256, 128))
    blk = lambda j, nu: jnp.minimum(j, nu[0] - 1)
    hidden = pl.pallas_call(
        functools.partial(_moe_up_kernel, cast_rows=_pick_tile(d, (256, 128, 64, 32, 16, 8))),
        grid_spec=pltpu.PrefetchScalarGridSpec(
            num_scalar_prefetch=2,
            grid=(nf, n_blocks),
            in_specs=[pl.BlockSpec((tm, d // 2), lambda f, j, be, nu: (blk(j, nu), 0)),
                      pl.BlockSpec((None, None, d, tf), lambda f, j, be, nu: (layer, be[j], 0, f)),
                      pl.BlockSpec((None, None, d, tf), lambda f, j, be, nu: (layer, be[j], 0, nf + f)),
                      pl.BlockSpec((None, None, 1, tf), lambda f, j, be, nu: (layer, be[j], 0, f)),
                      pl.BlockSpec((None, None, 1, tf), lambda f, j, be, nu: (layer, be[j], 0, nf + f))],
            out_specs=pl.BlockSpec((tm, tf), lambda f, j, be, nu: (j, f)),
            scratch_shapes=[pltpu.VMEM((d, tf), BF16), pltpu.VMEM((d, tf), BF16)]),
        out_shape=jax.ShapeDtypeStruct((n_rows, d_ff), BF16),
        compiler_params=_cparams(2),
        name="moe_up",
    )(block_e, n_used, xs, w_gate_up, w_gate_up, b_gate_up, b_gate_up)
    return pl.pallas_call(
        functools.partial(_moe_down_kernel, cast_rows=_pick_tile(d_ff, (256, 128, 64, 32, 16, 8))),
        grid_spec=pltpu.PrefetchScalarGridSpec(
            num_scalar_prefetch=2,
            grid=(d // tn, n_blocks),
            in_specs=[pl.BlockSpec((tm, d_ff), lambda n, j, be, nu: (blk(j, nu), 0)),
                      pl.BlockSpec((None, None, d_ff, tn), lambda n, j, be, nu: (layer, be[j], 0, n)),
                      pl.BlockSpec((None, None, 1, tn), lambda n, j, be, nu: (layer, be[j], 0, n))],
            out_specs=pl.BlockSpec((tm, tn), lambda n, j, be, nu: (j, n)),
            scratch_shapes=[pltpu.VMEM((d_ff, tn), BF16)]),
        out_shape=jax.ShapeDtypeStruct((n_rows, d), F32),
        compiler_params=_cparams(2),
        name="moe_down",
    )(block_e, n_used, hidden, w_down, b_down)


def _moe(x_f32, x_packed, rows_buf, layer, w_router_hi, w_router_lo, b_router, w_gate_up,
         b_gate_up, w_down, b_down):
    n = x_f32.shape[0]
    n_experts = w_gate_up.shape[1]
    tm = MOE_BLOCK_ROWS[0] if n * TOP_K >= 1024 * n_experts else MOE_BLOCK_ROWS[1]
    experts, gates, ranks, counts = _router(x_f32, w_router_hi, w_router_lo, b_router)
    counts = counts[0, :n_experts].astype(I32)
    pend = jnp.cumsum((counts + tm - 1) // tm * tm)
    pstart = pend - (counts + tm - 1) // tm * tm
    n_blocks = -(-(n * TOP_K + n_experts * (tm - 1)) // tm)
    block_e = jnp.minimum(jnp.sum(pend[None, :] <= (jnp.arange(n_blocks) * tm)[:, None], axis=1),
                          n_experts - 1).astype(I32)
    first_row = jnp.arange(n_blocks, dtype=I32) * tm
    block_rows = jnp.clip(counts[block_e] - (first_row - pstart[block_e].astype(I32)), 0, tm)
    block_rows = jnp.where(first_row < pend[-1], block_rows, 0).astype(I32)
    n_used = jnp.concatenate([(pend[-1] // tm).astype(I32).reshape(1), block_rows])
    e4 = experts[:, :TOP_K]
    onehot = e4[..., None] == jnp.arange(n_experts, dtype=I32)
    dest = (ranks[:, :TOP_K] + jnp.sum(jnp.where(onehot, pstart.astype(I32), 0), axis=-1)).reshape(-1)
    if rows_buf is None:
        rows_buf = jnp.zeros((n_blocks * tm,) + x_packed.shape[1:], I32)
    xs = _dispatch(x_packed, dest, rows_buf)
    y = _moe_experts(xs, block_e, n_used, layer, w_gate_up, b_gate_up, w_down, b_down, tm)
    return y, dest, gates, xs


def _pad_cols(w, n):
    return jnp.pad(w, ((0, 0), (0, n - w.shape[1])))


def _keys_with_past(past, new, s_pad):
    b, p, w = past.shape
    rows = jnp.pad(past.astype(BF16), ((0, 0), (0, s_pad - p), (0, 0)))
    rows = lax.dynamic_update_slice(rows, new.astype(BF16), (0, p, 0))
    return rows.reshape(b * s_pad, w)


def kernel(x_prompt, x_sample, cache_a_k, cache_a_v, cache_a_kidx, cache_b_k, cache_b_v,
           cache_c_k, cache_c_v, cache_mem_k, cache_mem_v, mem_prompt,
           w_in_a, w_in_b, w_in_c, diff_lambda, diff_subln, rel_bias, w_mem_kv, w_o,
           ln_g, ln_b, w_router, b_router, w_gate_up, b_gate_up, w_down, b_down):
    bp, tp, d = x_prompt.shape
    bs, ts, _ = x_sample.shape
    depth = w_o.shape[0]
    past_len = cache_a_k.shape[2]
    n_kv, hd = cache_a_k.shape[3], cache_a_k.shape[4]
    d_idx = cache_a_kidx.shape[3]
    n_mem, h_mem = cache_mem_k.shape[2], cache_mem_k.shape[3]
    mem_w = h_mem * hd
    tok_w = d - mem_w
    h_a = tok_w // hd
    h_b = cache_b_k.shape[3]
    hc = cache_c_k.shape[3]
    n_idx = (w_in_a.shape[2] - tok_w - 2 * n_kv * hd - d_idx - mem_w) // (d_idx + 1)
    n_experts = w_router.shape[2]
    alpha = (2 * depth) ** 0.25
    mp, ms = bp * tp, bs * ts
    s_new = past_len + ts
    kv_w, qi_w = n_kv * hd, n_idx * d_idx

    o_k, o_v, o_qi = tok_w, tok_w + kv_w, tok_w + 2 * kv_w
    o_ki = o_qi + qi_w
    o_wi = o_ki + d_idx
    o_mq = o_wi + n_idx
    a_qi = tok_w
    a_mq = a_qi + qi_w
    a_kw = a_mq + mem_w
    bf = lambda w: w.astype(BF16)
    w_a = [bf(_pad_cols(jnp.concatenate([w[:, :o_k], w[:, o_qi:o_ki], w[:, o_mq:], w[:, o_ki:o_mq]],
                                        axis=1), _round_up(a_kw + LANE, 7 * LANE if d >= 7 * LANE else LANE)))
           for w in w_in_a]
    w_a_k = [bf(w[:, o_k:o_v]) for w in w_in_a]
    w_a_v = [bf(w[:, o_v:o_qi]) for w in w_in_a]
    w_ki = [bf(w[:, o_ki:o_wi]) for w in w_in_a]
    main_bc = lambda w: bf(jnp.concatenate([w[:, :tok_w], w[:, 3 * tok_w:]], axis=1))
    w_b, w_c = [main_bc(w) for w in w_in_b], [main_bc(w) for w in w_in_c]
    w_b_k, w_b_v = [bf(w[:, tok_w:2 * tok_w]) for w in w_in_b], [bf(w[:, 2 * tok_w:3 * tok_w]) for w in w_in_b]
    w_c_k, w_c_v = [bf(w[:, tok_w:2 * tok_w]) for w in w_in_c], [bf(w[:, 2 * tok_w:3 * tok_w]) for w in w_in_c]
    w_o_bf = w_o.astype(BF16)
    w_r_pad = jnp.pad(w_router, ((0, 0), (0, 0), (0, LANE - n_experts)))
    w_r_hi = w_r_pad.astype(BF16)
    w_r_lo = (w_r_pad - w_r_hi.astype(F32)).astype(BF16)
    b_r_pad = jnp.pad(b_router, ((0, 0), (0, LANE - n_experts)), constant_values=MASKED)[:, None, :]
    b_gu = b_gate_up[:, :, None, :]
    b_dn = b_down[:, :, None, :]

    tq_a = _pick_tile(tp, (128, 64))
    tk_a = _pick_tile(tp, (512, 256, 128))
    tq_b = _pick_tile(tp, (256, 128, 64))
    tk_b = min(tq_b, 256)
    tq_c = _pick_tile(tp, (512, 256, 128))
    tk_c = _pick_tile(tp, (512, 256, 128))
    tq_m = _pick_tile(tp, (512, 256, 128))
    sp_a, sp_b, sp_c = _round_up(s_new, tk_a), _round_up(s_new, tk_b), _round_up(s_new, tk_c)

    far2 = _far_bias(rel_bias) * LOG2E
    near_a_p = _near_bias(rel_bias, tq_a, tk_a, 0, tp // tq_a) * LOG2E
    near_a_s = _near_bias(rel_bias, ts, tk_a, past_len, 1) * LOG2E
    near_c_p = _near_bias(rel_bias, tq_c, tk_c, 0, tp // tq_c) * LOG2E
    near_c_s = _near_bias(rel_bias, ts, tk_c, past_len, 1) * LOG2E
    pair = lambda nb: nb.reshape(nb.shape[0], hc, 2, *nb.shape[2:])

    w_mem = jnp.moveaxis(w_mem_kv, 0, 1).reshape(d, depth * 2 * mem_w).astype(BF16)
    mem_f, mem_b = _matmul(mem_prompt.reshape(bp * n_mem, d).astype(BF16), w_mem)
    mem_f = mem_f.reshape(bp, n_mem, depth, 2, h_mem, hd)
    mem_k_p = jnp.moveaxis(mem_f[:, :, :, 0], 2, 0)
    mem_v_p = jnp.moveaxis(mem_f[:, :, :, 1], 2, 0)
    mem_s_k = cache_mem_k.reshape(depth, bs * n_mem, mem_w).astype(BF16)
    mem_s_v = cache_mem_v.reshape(depth, bs * n_mem, mem_w).astype(BF16)

    x_f = jnp.concatenate([x_prompt.reshape(mp, d), x_sample.reshape(ms, d)], axis=0)
    x_b = x_f.astype(BF16)
    rows_p = [None] * depth
    rows_s = [None] * depth
    rows_buf = None

    for i in range(depth):
        kind, j = i % N_MIXERS, i // N_MIXERS
        h_b_ = _matmul_bf16(x_b, (w_a, w_b, w_c)[kind][j])
        w_k, w_v = ((w_a_k, w_a_v), (w_b_k, w_b_v), (w_c_k, w_c_v))[kind]
        k_shape = ((n_kv, hd), (h_b, hd), (hc, 2, hd))[kind]
        v_shape = ((n_kv, hd), (h_b, hd), (hc, 2 * hd))[kind]
        kp_f, kp_b = _proj_heads(x_b, 0, mp, w_k[j], k_shape)
        vp_f, vp_b = _proj_heads(x_b, 0, mp, w_v[j], v_shape)
        ks_f, ks_b = _proj_heads(x_b, mp, ms, w_k[j], k_shape)
        vs_f, vs_b = _proj_heads(x_b, mp, ms, w_v[j], v_shape)
        k_new, v_new = ks_b.reshape(bs, ts, -1), vs_b.reshape(bs, ts, -1)
        rows_p[i] = (kp_f.reshape(bp, tp, *k_shape), vp_f.reshape(bp, tp, *v_shape))
        rows_s[i] = (ks_f.reshape(bs, ts, *k_shape), vs_f.reshape(bs, ts, *v_shape))
        rb_s = mp // ts

        if kind == 0:
            k_sel_p = min(K_SEL_MAX, tp // 4)
            k_sel_s = min(K_SEL_MAX, s_new // 4)
            dims = dict(n_heads=h_a, n_kv=n_kv, hd=hd, n_idx=n_idx, d_idx=d_idx)
            mix_p = _dsa_attn(
                (h_b_, 0, 0, a_qi // qi_w, a_kw // LANE),
                (kp_b, 0, vp_b, 0, h_b_, a_kw // LANE),
                near_a_p, far2, n_batch=bp, t=tp, s_pad=tp, s_valid=tp, q_off=0,
                tq=tq_a, tk=tk_a, k_sel=k_sel_p, **dims)
            ki_f, ki_b = _matmul(x_b, w_ki[j])
            kiw_new = jnp.pad(ki_b[mp:].reshape(bs, ts, d_idx), ((0, 0), (0, 0), (0, LANE - d_idx)))
            kiw_past = jnp.pad(cache_a_kidx[j], ((0, 0), (0, 0), (0, LANE - d_idx)))
            k_all = _keys_with_past(cache_a_k[j].reshape(bs, past_len, kv_w), k_new, sp_a)
            v_all = _keys_with_past(cache_a_v[j].reshape(bs, past_len, kv_w), v_new, sp_a)
            ki_all = _keys_with_past(kiw_past, kiw_new, sp_a)
            mix_s = _dsa_attn(
                (h_b_, rb_s, 0, a_qi // qi_w, a_kw // LANE),
                (k_all, 0, v_all, 0, ki_all, 0),
                near_a_s, far2, n_batch=bs, t=ts, s_pad=sp_a, s_valid=s_new, q_off=past_len,
                tq=ts, tk=tk_a, k_sel=k_sel_s, **dims)
            mq_cb = a_mq // mem_w
            rows_p[i] += (ki_f[:mp].reshape(bp, tp, d_idx),)
            rows_s[i] += (ki_f[mp:].reshape(bs, ts, d_idx),)
        elif kind == 1:
            dims = dict(n_heads=h_b, hd=hd)
            mix_p = _stick_attn((h_b_, 0, 0), (kp_b, 0, vp_b, 0),
                                n_batch=bp, t=tp, s_pad=tp, q_off=0, tq=tq_b, tk=tk_b, **dims)
            k_all = _keys_with_past(cache_b_k[j].reshape(bs, past_len, tok_w), k_new, sp_b)
            v_all = _keys_with_past(cache_b_v[j].reshape(bs, past_len, tok_w), v_new, sp_b)
            mix_s = _stick_attn((h_b_, rb_s, 0), (k_all, 0, v_all, 0),
                                n_batch=bs, t=ts, s_pad=sp_b, q_off=past_len, tq=ts, tk=tk_b, **dims)
            mq_cb = tok_w // mem_w
        else:
            lambda_init = 0.8 - 0.6 * math.exp(-0.3 * i)
            lv = diff_lambda[j].astype(F32)
            lam = jnp.exp(jnp.sum(lv[0] * lv[1])) - jnp.exp(jnp.sum(lv[2] * lv[3])) + lambda_init
            scalars = jnp.concatenate([lam.reshape(1), far2]).astype(F32)
            g = diff_subln[j].reshape(1, 2 * hd).astype(F32)
            dims = dict(n_heads=hc, hd=hd, out_scale=1.0 - lambda_init)
            mix_p = _diff_attn((h_b_, 0, 0), (kp_b, 0, vp_b, 0), pair(near_c_p), scalars, g,
                               n_batch=bp, t=tp, s_pad=tp, s_valid=tp, q_off=0, tq=tq_c, tk=tk_c, **dims)
            k_all = _keys_with_past(cache_c_k[j].reshape(bs, past_len, tok_w), k_new, sp_c)
            v_all = _keys_with_past(cache_c_v[j].reshape(bs, past_len, tok_w), v_new, sp_c)
            mix_s = _diff_attn((h_b_, rb_s, 0), (k_all, 0, v_all, 0), pair(near_c_s), scalars, g,
                               n_batch=bs, t=ts, s_pad=sp_c, s_valid=s_new, q_off=past_len,
                               tq=ts, tk=tk_c, **dims)
            mq_cb = tok_w // mem_w

        mem_p = _mem_attn(h_b_, 0, mq_cb, mem_b[:, (2 * i) * mem_w:(2 * i + 1) * mem_w],
                          mem_b[:, (2 * i + 1) * mem_w:(2 * i + 2) * mem_w],
                          bp, tp, tq_m, h_mem, hd)
        mem_s = _mem_attn(h_b_, rb_s, mq_cb, mem_s_k[i], mem_s_v[i], bs, ts, ts, h_mem, hd)
        mix = jnp.concatenate([mix_p, mix_s], axis=0)
        mem_o = jnp.concatenate([mem_p, mem_s], axis=0)
        x_f, x_pk = _proj_ln(mix, mem_o, w_o_bf[i, :tok_w], w_o_bf[i, tok_w:], x_f,
                             ln_g[i, 0][None], ln_b[i, 0][None], alpha)
        y, dest, gates, rows_buf = _moe(x_f, x_pk, rows_buf, i, w_r_hi[i], w_r_lo[i], b_r_pad[i],
                                        w_gate_up, b_gu, w_down, b_dn)
        x_f, x_b = _combine_ln(x_f, y, dest, gates, ln_g[i, 1][None], ln_b[i, 1][None], alpha)

    def stack_kind(rows, kind):
        sel = [r for i, r in enumerate(rows) if i % N_MIXERS == kind]
        return [jnp.stack(parts) for parts in zip(*sel)]

    y_prompt = x_f[:mp].reshape(bp, tp, d)
    y_sample = x_f[mp:].reshape(bs, ts, d)
    a_k_p, a_v_p, a_kidx_p = stack_kind(rows_p, 0)
    b_k_p, b_v_p = stack_kind(rows_p, 1)
    c_k_p, c_v_p = stack_kind(rows_p, 2)
    a_k_s, a_v_s, a_kidx_s = stack_kind(rows_s, 0)
    b_k_s, b_v_s = stack_kind(rows_s, 1)
    c_k_s, c_v_s = stack_kind(rows_s, 2)
    return (y_prompt, y_sample, a_k_p, a_v_p, a_kidx_p, b_k_p, b_v_p, c_k_p, c_v_p, mem_k_p, mem_v_p,
            a_k_s, a_v_s, a_kidx_s, b_k_s, b_v_s, c_k_s, c_v_s)
```

```python
import functools
import math

import jax
import jax.numpy as jnp
from jax import lax
from jax.experimental import pallas as pl
from jax.experimental.pallas import tpu as pltpu

F32 = jnp.float32
BF16 = jnp.bfloat16
I32 = jnp.int32

CHUNK = 64
CHUNK_SHIFT = CHUNK.bit_length() - 1
K_SEL_MAX = 256
MAX_DISTANCE = 128
TOP_K = 4
SWIGLU_LIMIT = 7.0
SWIGLU_ALPHA = 1.702
LN_EPS = 1e-5
N_MIXERS = 3

LANE = 128
VMEM_LIMIT = 56 * 1024 * 1024
MASKED = -1e30
EXP_UNDERFLOW = -104.0
LOG2E = 1.4426950408889634
INT_MIN = -2 ** 31
KEY_NEG_INF = -2139095041
BISECT_UNTESTED_BITS = 12
ALL_TIES = 2 ** 30
MOE_BLOCK_ROWS = (512, 128)
HI16 = -65536

def _cparams(n_axes):
    return pltpu.CompilerParams(dimension_semantics=("arbitrary",) * n_axes,
                                vmem_limit_bytes=VMEM_LIMIT)


def _round_up(x, m):
    return -(-x // m) * m


def _pick_tile(n, candidates):
    for c in candidates:
        if n % c == 0:
            return c
    return n


def _dot_nt(a, b):
    return lax.dot_general(a, b, (((1,), (1,)), ((), ())), preferred_element_type=F32)


def _resident(block_shape, index_map):
    return pl.BlockSpec(block_shape, index_map, pipeline_mode=pl.Buffered(1))


def _mm_kernel(x_ref, w_ref, of_ref, ob_ref):
    acc = jnp.dot(x_ref[...], w_ref[...], preferred_element_type=F32)
    of_ref[...] = acc
    ob_ref[...] = acc.astype(BF16)


def _matmul(x, w):
    m, k = x.shape
    n = w.shape[1]
    tm = _pick_tile(m, (640, 512, 256, 128, 64, 32, 16))
    tn = _pick_tile(n, (1280, 1024, 768, 512, 384, 256, 128))
    return pl.pallas_call(
        _mm_kernel,
        grid=(n // tn, m // tm),
        in_specs=[pl.BlockSpec((tm, k), lambda j, i: (i, 0)),
                  pl.BlockSpec((k, tn), lambda j, i: (0, j))],
        out_specs=[pl.BlockSpec((tm, tn), lambda j, i: (i, j)),
                   pl.BlockSpec((tm, tn), lambda j, i: (i, j))],
        out_shape=[jax.ShapeDtypeStruct((m, n), F32), jax.ShapeDtypeStruct((m, n), BF16)],
        compiler_params=_cparams(2),
        name="matmul",
    )(x, w)


def _mm_bf16_kernel(x_ref, w_ref, ob_ref):
    ob_ref[...] = jnp.dot(x_ref[...], w_ref[...], preferred_element_type=F32).astype(BF16)


def _matmul_bf16(x, w):
    m, k = x.shape
    n = w.shape[1]
    tm = _pick_tile(m, (640, 512, 256, 128, 64, 32, 16))
    tn = _pick_tile(n, (1280, 1024, 896, 768, 512, 384, 256, 128))
    return pl.pallas_call(
        _mm_bf16_kernel,
        grid=(n // tn, m // tm),
        in_specs=[pl.BlockSpec((tm, k), lambda j, i: (i, 0)),
                  pl.BlockSpec((k, tn), lambda j, i: (0, j))],
        out_specs=pl.BlockSpec((tm, tn), lambda j, i: (i, j)),
        out_shape=jax.ShapeDtypeStruct((m, n), BF16),
        compiler_params=_cparams(2),
        name="matmul_bf16",
    )(x, w)


def _proj_heads_kernel(x_ref, w_ref, of_ref, ob_ref, *, head_shape):
    acc = jnp.dot(x_ref[...], w_ref[...], preferred_element_type=F32)
    ob_ref[...] = acc.astype(BF16)
    slab = head_shape[-1]
    lead = head_shape[:-1]
    for c in range(math.prod(lead)):
        idx, rest = [], c
        for size in reversed(lead):
            idx.append(rest % size)
            rest //= size
        of_ref[(slice(None), *reversed(idx), slice(None))] = acc[:, c * slab:(c + 1) * slab]


def _proj_heads(x, row0, n_rows, w, head_shape):
    k = x.shape[1]
    n = w.shape[1]
    assert math.prod(head_shape) == n and head_shape[-1] % LANE == 0
    tm = _pick_tile(n_rows, (256, 128, 64, 32, 16))
    assert row0 % tm == 0
    zeros = (0,) * len(head_shape)
    return pl.pallas_call(
        functools.partial(_proj_heads_kernel, head_shape=tuple(head_shape)),
        grid=(n_rows // tm,),
        in_specs=[pl.BlockSpec((tm, k), lambda i: (row0 // tm + i, 0)),
                  pl.BlockSpec((k, n), lambda i: (0, 0))],
        out_specs=[pl.BlockSpec((tm, *head_shape), lambda i: (i, *zeros)),
                   pl.BlockSpec((tm, n), lambda i: (i, 0))],
        out_shape=[jax.ShapeDtypeStruct((n_rows, *head_shape), F32),
                   jax.ShapeDtypeStruct((n_rows, n), BF16)],
        compiler_params=_cparams(1),
        name="proj_heads",
    )(x, w)


def _layer_norm_rows(z, g, b):
    mu = jnp.mean(z, axis=-1, keepdims=True)
    zc = z - mu
    var = jnp.mean(zc * zc, axis=-1, keepdims=True)
    return zc * lax.rsqrt(var + LN_EPS) * g + b


def _pack_halves(x):
    half = x.shape[1] // 2
    xb = x.astype(BF16).astype(F32)
    lo = lax.shift_right_logical(pltpu.bitcast(xb[:, :half], I32), 16)
    return lo | (pltpu.bitcast(xb[:, half:], I32) & HI16)


def _unpack_halves(p):
    lo = pltpu.bitcast(p << 16, F32).astype(BF16)
    hi = pltpu.bitcast(p & HI16, F32).astype(BF16)
    return jnp.concatenate([lo, hi], axis=1)


def _proj_ln_kernel(a1_ref, a2_ref, w1_ref, w2_ref, r_ref, g_ref, b_ref, of_ref, op_ref, *, alpha):
    y = jnp.dot(a1_ref[...], w1_ref[...], preferred_element_type=F32)
    y = y + jnp.dot(a2_ref[...], w2_ref[...], preferred_element_type=F32)
    out = _layer_norm_rows(alpha * r_ref[...] + y, g_ref[...], b_ref[...])
    of_ref[...] = out
    op_ref[...] = _pack_halves(out)


def _proj_ln(a1, a2, w1, w2, res, g, b, alpha):
    m, k1 = a1.shape
    k2 = a2.shape[1]
    d = w1.shape[1]
    tm = _pick_tile(m, (256, 128, 64, 32, 16))
    row = lambda i: (i, 0)
    fix = lambda i: (0, 0)
    return pl.pallas_call(
        functools.partial(_proj_ln_kernel, alpha=alpha),
        grid=(m // tm,),
        in_specs=[pl.BlockSpec((tm, k1), row), pl.BlockSpec((tm, k2), row),
                  pl.BlockSpec((k1, d), fix), pl.BlockSpec((k2, d), fix),
                  pl.BlockSpec((tm, d), row), pl.BlockSpec((1, d), fix), pl.BlockSpec((1, d), fix)],
        out_specs=[pl.BlockSpec((tm, d), row), pl.BlockSpec((tm, d // 2), row)],
        out_shape=[jax.ShapeDtypeStruct((m, d), F32), jax.ShapeDtypeStruct((m, d // 2), I32)],
        compiler_params=_cparams(1),
        name="proj_ln",
    )(a1, a2, w1, w2, res, g, b)


def _row_copy(src_ref, src_row, dst_ref, dst_row, sem):
    return pltpu.make_async_copy(src_ref.at[pl.ds(src_row, 1)], dst_ref.at[pl.ds(dst_row, 1)], sem)


def _combine_ln_kernel(dest_ref, next_ref, x_ref, gate_ref, g_ref, b_ref, y_hbm, of_ref, ob_ref,
                       ybuf, sems, *, alpha, tm, n_steps):
    i = pl.program_id(0)
    slot = i % 2

    def gather(idx_ref, to_slot):
        def body(r, carry):
            for k in range(TOP_K):
                _row_copy(y_hbm, idx_ref[0, r * TOP_K + k], ybuf.at[to_slot, k], r,
                          sems.at[to_slot]).start()
            return carry
        lax.fori_loop(0, tm, body, 0)

    @pl.when(i == 0)
    def _():
        gather(dest_ref, 0)

    @pl.when(i + 1 < n_steps)
    def _():
        gather(next_ref, 1 - slot)

    def drain(r, carry):
        for k in range(TOP_K):
            _row_copy(y_hbm, 0, ybuf.at[slot, k], 0, sems.at[slot]).wait()
        return carry
    lax.fori_loop(0, tm, drain, 0)

    gates = gate_ref[...]
    ffn = gates[:, 0:1] * ybuf[slot, 0]
    for k in range(1, TOP_K):
        ffn = ffn + gates[:, k:k + 1] * ybuf[slot, k]
    out = _layer_norm_rows(alpha * x_ref[...] + ffn, g_ref[...], b_ref[...])
    of_ref[...] = out
    ob_ref[...] = out.astype(BF16)


def _combine_ln(x, y, dest, gates, g, b, alpha):
    m, d = x.shape
    tm = _pick_tile(m, (64, 32, 16))
    n_steps = m // tm
    dest2 = dest.reshape(n_steps, 1, tm * TOP_K)
    row = lambda i: (i, 0)
    fix = lambda i: (0, 0)
    return pl.pallas_call(
        functools.partial(_combine_ln_kernel, alpha=alpha, tm=tm, n_steps=n_steps),
        grid=(n_steps,),
        in_specs=[pl.BlockSpec((None, 1, tm * TOP_K), lambda i: (i, 0, 0), memory_space=pltpu.SMEM),
                  pl.BlockSpec((None, 1, tm * TOP_K), lambda i: (jnp.minimum(i + 1, n_steps - 1), 0, 0),
                               memory_space=pltpu.SMEM),
                  pl.BlockSpec((tm, d), row), pl.BlockSpec((tm, LANE), row),
                  pl.BlockSpec((1, d), fix), pl.BlockSpec((1, d), fix),
                  pl.BlockSpec(memory_space=pl.ANY)],
        out_specs=[pl.BlockSpec((tm, d), row), pl.BlockSpec((tm, d), row)],
        out_shape=[jax.ShapeDtypeStruct((m, d), F32), jax.ShapeDtypeStruct((m, d), BF16)],
        scratch_shapes=[pltpu.VMEM((2, TOP_K, tm, d), F32), pltpu.SemaphoreType.DMA((2,))],
        compiler_params=_cparams(1),
        name="combine_ln",
    )(dest2, dest2, x, gates, g, b, y)


def _mem_attn_kernel(q_ref, k_ref, v_ref, o_ref, *, n_heads, hd):
    scale = hd ** -0.5
    for h in range(n_heads):
        cols = slice(h * hd, (h + 1) * hd)
        s = _dot_nt(q_ref[:, cols], k_ref[:, cols]) * scale
        p = jnp.exp(s - jnp.max(s, axis=-1, keepdims=True))
        p = p / jnp.sum(p, axis=-1, keepdims=True)
        o_ref[:, cols] = jnp.dot(p.astype(BF16), v_ref[:, cols],
                                 preferred_element_type=F32).astype(BF16)


def _mem_attn(qarr, q_rb0, q_cb, mk, mv, n_batch, t, tq, n_heads, hd):
    w = n_heads * hd
    n_mem = mk.shape[0] // n_batch
    nqb = t // tq
    return pl.pallas_call(
        functools.partial(_mem_attn_kernel, n_heads=n_heads, hd=hd),
        grid=(n_batch, nqb),
        in_specs=[pl.BlockSpec((tq, w), lambda b, i: (q_rb0 + b * nqb + i, q_cb)),
                  pl.BlockSpec((n_mem, w), lambda b, i: (b, 0)),
                  pl.BlockSpec((n_mem, w), lambda b, i: (b, 0))],
        out_specs=pl.BlockSpec((tq, w), lambda b, i: (b * nqb + i, 0)),
        out_shape=jax.ShapeDtypeStruct((n_batch * t, w), BF16),
        compiler_params=_cparams(2),
        name="mem_attn",
    )(qarr, mk, mv)


def _t5_bucket(rel, num_buckets):
    half = num_buckets // 2
    exact = half // 2
    n = jnp.abs(rel)
    far = exact + (jnp.log(jnp.maximum(n, 1).astype(F32) / exact)
                   / math.log(MAX_DISTANCE / exact) * (half - exact)).astype(I32)
    return jnp.where(rel > 0, half, 0) + jnp.where(n < exact, n, jnp.minimum(far, half - 1))


def _near_phase(q_off, tq, tk, nqb, i):
    full = tk // tq
    return (q_off // tq + i) % full if nqb >= full else i


def _near_bias(rel_bias, tq, tk, q_off, nqb):
    full = tk // tq
    n_maps = rel_bias.shape[1]
    rel_lo = -(2 * tk + tq)
    rels = jnp.arange(rel_lo, tk + 1, dtype=I32)
    by_rel = jnp.moveaxis(rel_bias[_t5_bucket(rels, rel_bias.shape[0])], -1, 0).astype(F32)
    n = tq + tk
    tiles = []
    for p in range(min(full, nqb)):
        q_in_block = (p if nqb >= full else (q_off // tq + p) % full) * tq
        per_near = []
        for near in range(2):
            rel0 = -near * tk - q_in_block
            start = rel0 - rel_lo
            v = by_rel[:, start - tq:start + tk]
            u = jnp.roll(v, -tq, axis=1)
            skew = jnp.tile(u, (1, tq))[:, :tq * (n - 1)].reshape(n_maps, tq, n - 1)
            per_near.append(skew[:, :, :tk])
        tiles.append(jnp.stack(per_near, axis=1))
    return jnp.stack(tiles)


def _far_bias(rel_bias):
    return rel_bias[rel_bias.shape[0] // 2 - 1].astype(F32)


def _dsa_kernel(far_ref, q_ref, qi_ref, qw_ref, k_ref, v_ref, ki_ref, bd_ref, o_ref,
                key_scr, wb_scr, m_scr, acc_scr, *,
                tq, tk, q_off, s_valid, k_sel, n_heads, n_kv, hd, n_idx, d_idx, idx_bits):
    gqa = n_heads // n_kv
    reps = tk // LANE
    i = pl.program_id(1)
    q0 = q_off + i * tq
    kb_diag = q0 // tk
    nkb = kb_diag + 1
    q_pos = q0 + lax.broadcasted_iota(I32, (tq, 1), 0)
    lane = lax.broadcasted_iota(I32, (1, tk), 1)
    lane1 = lax.broadcasted_iota(I32, (1, LANE), 1)
    tile = lambda x: jnp.concatenate([x] * reps, axis=1)

    qi = qi_ref[...]
    qw = qw_ref[...].astype(F32)
    qi_heads = [qi[:, h * d_idx:(h + 1) * d_idx] for h in range(n_idx)]
    for h in range(n_idx):
        wb_scr[h] = jnp.broadcast_to(qw[:, d_idx + h:d_idx + h + 1] * (d_idx ** -0.5), (tq, LANE))

    def score_block(kb, carry):
        ks = pl.multiple_of(kb * tk, tk)
        ki = ki_ref[pl.ds(ks, tk), :][:, :d_idx]
        sc = jnp.zeros((tq, tk), F32)
        for h in range(n_idx):
            sc = sc + jnp.maximum(_dot_nt(qi_heads[h], ki), 0.0) * tile(wb_scr[h])
        sc = sc * (n_idx ** -0.5)
        sc = jnp.where(sc == 0.0, 0.0, sc)
        k_pos = ks + lane
        allowed = ((k_pos >> CHUNK_SHIFT) <= (q_pos >> CHUNK_SHIFT)) & (k_pos < s_valid)
        sc = jnp.where(allowed, sc, -jnp.inf)
        bits = pltpu.bitcast(sc, I32)
        key_scr[kb] = bits ^ ((bits >> 31) & 0x7FFFFFFF)
        return carry

    lax.fori_loop(0, nkb, score_block, 0)

    def count(pred):
        def body(kb, acc):
            key = key_scr[kb]
            for c in range(reps):
                acc = acc + jnp.where(pred(kb, c, key[:, c * LANE:(c + 1) * LANE]), 1.0, 0.0)
            return acc
        acc = lax.fori_loop(0, nkb, body, jnp.zeros((tq, LANE), F32))
        return jnp.sum(acc, axis=1, keepdims=True)

    def refine(it, t_u, cnt_t):
        cand_u = t_u | (jnp.int32(1) << (31 - it))
        cand_b = jnp.broadcast_to(cand_u ^ INT_MIN, (tq, LANE))
        cnt = count(lambda kb, c, key: key >= cand_b)
        take = cnt >= k_sel
        return jnp.where(take, cand_u, t_u), jnp.where(take, cnt, cnt_t)

    def open_rows(cnt_t):
        return jnp.max(jnp.where(cnt_t > k_sel, 1.0, 0.0))

    total = (nkb * tk).astype(F32)
    t_u, cnt_t = lax.fori_loop(
        0, BISECT_UNTESTED_BITS, lambda it, st: refine(it, *st),
        (jnp.zeros((tq, 1), I32), jnp.full((tq, 1), total, F32)))

    def bisect_cond(state):
        it, _, _, still_open = state
        return (it < 32) & (still_open > 0.0)

    def bisect_body(state):
        it, t_u, cnt_t, _ = state
        t_u, cnt_t = refine(it, t_u, cnt_t)
        return it + 1, t_u, cnt_t, open_rows(cnt_t)

    _, t_u, cnt_t, _ = lax.while_loop(
        bisect_cond, bisect_body, (jnp.int32(BISECT_UNTESTED_BITS), t_u, cnt_t, open_rows(cnt_t)))
    thr = t_u ^ INT_MIN
    thr_b = jnp.broadcast_to(thr, (tq, LANE))

    def tie_search():
        need = k_sel - count(lambda kb, c, key: key > thr_b)

        def index_bit(it, j_hi):
            cand = j_hi | (jnp.int32(1) << (idx_bits - 1 - it))
            cand_b = jnp.broadcast_to(cand, (tq, LANE))
            cnt = count(lambda kb, c, key: (key == thr_b) & (kb * tk + c * LANE + lane1 < cand_b))
            return jnp.where(cnt < need, cand, j_hi)

        return lax.fori_loop(0, idx_bits, index_bit, jnp.zeros((tq, 1), I32))

    tie_hi = lax.cond(jnp.max(cnt_t) > k_sel, tie_search, lambda: jnp.full((tq, 1), ALL_TIES, I32))

    m_scr[...] = jnp.full(m_scr.shape, MASKED, F32)
    acc_scr[...] = jnp.zeros(acc_scr.shape, F32)
    scale2 = (hd ** -0.5) * LOG2E
    q_groups = [jnp.concatenate([q_ref[:, (g * gqa + r) * hd:(g * gqa + r + 1) * hd]
                                 for r in range(gqa)], axis=0) for g in range(n_kv)]
    far_rows = [jnp.concatenate([jnp.full((tq, LANE), far_ref[g * gqa + r], F32)
                                 for r in range(gqa)], axis=0) for g in range(n_kv)]
    ones = jnp.ones((tk, hd), BF16)

    def attend_block(kb, near):
        ks = pl.multiple_of(kb * tk, tk)
        key = key_scr[kb]
        sel = (key > thr) | ((key == thr) & (ks + lane <= tie_hi))
        sel = sel & (key > KEY_NEG_INF)
        mbias = jnp.where(sel, 0.0, MASKED)
        for g in range(n_kv):
            kg = k_ref[pl.ds(ks, tk), g * hd:(g + 1) * hd]
            vext = jnp.concatenate([v_ref[pl.ds(ks, tk), g * hd:(g + 1) * hd], ones], axis=1)
            s = _dot_nt(q_groups[g], kg) * scale2
            m_old = m_scr[g]
            if near is None:
                t = s + jnp.concatenate([mbias] * gqa, axis=0)
                m_new = jnp.maximum(m_old, jnp.max(t, axis=-1, keepdims=True) + far_rows[g])
                p = jnp.exp2(t - tile(m_new - far_rows[g]))
            else:
                t = s + jnp.concatenate([mbias + bd_ref[g * gqa + r, near] for r in range(gqa)], axis=0)
                m_new = jnp.maximum(m_old, jnp.max(t, axis=-1, keepdims=True))
                p = jnp.exp2(t - tile(m_new))
            alpha = jnp.exp2(m_old - m_new)
            acc_scr[g] = (jnp.concatenate([alpha, alpha], axis=1) * acc_scr[g]
                          + jnp.dot(p.astype(BF16), vext, preferred_element_type=F32))
            m_scr[g] = m_new

    def far_block(kb, carry):
        attend_block(kb, None)
        return carry

    lax.fori_loop(0, jnp.maximum(kb_diag - 1, 0), far_block, 0)

    @pl.when(kb_diag >= 1)
    def _():
        attend_block(kb_diag - 1, 1)

    attend_block(kb_diag, 0)

    for g in range(n_kv):
        acc = acc_scr[g]
        o = acc[:, :hd] / acc[:, hd:]
        for r in range(gqa):
            h = g * gqa + r
            o_ref[:, h * hd:(h + 1) * hd] = o[r * tq:(r + 1) * tq].astype(BF16)


def _dsa_attn(q_src, k_src, bias_near, bias_far, *, n_batch, t, s_pad, s_valid, q_off, tq, tk,
              k_sel, n_heads, n_kv, hd, n_idx, d_idx):
    qarr, q_rb0, q_cb, qi_cb, qw_cb = q_src
    karr, k_cb, varr, v_cb, kiarr, ki_cb = k_src
    assert tk >= MAX_DISTANCE and tk % tq == 0 and q_off % tq == 0 and s_pad % tk == 0
    nqb = t // tq
    nkb_max = s_pad // tk
    gqa = n_heads // n_kv
    wq, wkv, wqi = n_heads * hd, n_kv * hd, n_idx * d_idx
    qrow = lambda b, i: q_rb0 + b * nqb + i
    kernel = functools.partial(
        _dsa_kernel, tq=tq, tk=tk, q_off=q_off, s_valid=s_valid, k_sel=k_sel, n_heads=n_heads,
        n_kv=n_kv, hd=hd, n_idx=n_idx, d_idx=d_idx, idx_bits=max((s_pad - 1).bit_length(), 1))
    return pl.pallas_call(
        kernel,
        grid=(n_batch, nqb),
        in_specs=[pl.BlockSpec(memory_space=pltpu.SMEM),
                  pl.BlockSpec((tq, wq), lambda b, i: (qrow(b, i), q_cb)),
                  pl.BlockSpec((tq, wqi), lambda b, i: (qrow(b, i), qi_cb)),
                  pl.BlockSpec((tq, LANE), lambda b, i: (qrow(b, i), qw_cb)),
                  _resident((s_pad, wkv), lambda b, i: (b, k_cb)),
                  _resident((s_pad, wkv), lambda b, i: (b, v_cb)),
                  _resident((s_pad, LANE), lambda b, i: (b, ki_cb)),
                  pl.BlockSpec((None, n_heads, 2, tq, tk),
                               lambda b, i: (_near_phase(q_off, tq, tk, nqb, i), 0, 0, 0, 0))],
        out_specs=pl.BlockSpec((tq, wq), lambda b, i: (b * nqb + i, 0)),
        out_shape=jax.ShapeDtypeStruct((n_batch * t, wq), BF16),
        scratch_shapes=[pltpu.VMEM((nkb_max, tq, tk), I32),
                        pltpu.VMEM((n_idx, tq, LANE), F32),
                        pltpu.VMEM((n_kv, gqa * tq, LANE), F32),
                        pltpu.VMEM((n_kv, gqa * tq, 2 * hd), F32)],
        compiler_params=_cparams(2),
        name="dsa_attn",
    )(bias_far, qarr, qarr, qarr, karr, varr, kiarr, bias_near)


def _stick_kernel(q_ref, k_ref, v_ref, o_ref, carry_scr, acc_scr, *, tq, tk, q_off, hd, hps):
    i = pl.program_id(2)
    q0 = q_off + i * tq
    q_pos = q0 + lax.broadcasted_iota(I32, (tq, 1), 0)
    lane = lax.broadcasted_iota(I32, (1, tk), 1)
    tri = (lax.broadcasted_iota(I32, (tk, tk), 0) > lax.broadcasted_iota(I32, (tk, tk), 1)).astype(BF16)
    scale = hd ** -0.5
    carry_scr[...] = jnp.zeros(carry_scr.shape, F32)
    acc_scr[...] = jnp.zeros(acc_scr.shape, F32)

    def cond(state):
        kb, carry_max = state
        return (kb >= 0) & (carry_max > EXP_UNDERFLOW)

    def body(state):
        kb, _ = state
        ks = pl.multiple_of(kb * tk, tk)
        before = (ks + lane) < q_pos
        carry_max = None
        for h in range(hps):
            cols = slice(h * hd, (h + 1) * hd)
            z = _dot_nt(q_ref[:, cols], k_ref[pl.ds(ks, tk), cols]) * scale
            log_beta = jnp.minimum(z, 0.0) - jnp.log(1.0 + jnp.exp(-jnp.abs(z)))
            log_keep = jnp.where(before, log_beta - z, 0.0)
            hi = log_keep.astype(BF16)
            lo = (log_keep - hi.astype(F32)).astype(BF16)
            tail = (jnp.dot(hi, tri, preferred_element_type=F32)
                    + jnp.dot(lo, tri, preferred_element_type=F32))
            carry = carry_scr[h]
            w = jnp.where(before, jnp.exp(log_beta + tail + carry), 0.0)
            acc_scr[h] += jnp.dot(w.astype(BF16), v_ref[pl.ds(ks, tk), cols], preferred_element_type=F32)
            carry = carry + jnp.sum(log_keep, axis=-1, keepdims=True)
            carry_scr[h] = carry
            head_max = jnp.max(carry)
            carry_max = head_max if carry_max is None else jnp.maximum(carry_max, head_max)
        return kb - 1, carry_max

    lax.while_loop(cond, body, ((q0 + tq - 1) // tk, jnp.float32(0.0)))
    for h in range(hps):
        o_ref[:, h * hd:(h + 1) * hd] = acc_scr[h].astype(BF16)


def _stick_attn(q_src, k_src, *, n_batch, t, s_pad, q_off, tq, tk, n_heads, hd):
    qarr, q_rb0, q_cb0 = q_src
    karr, k_cb0, varr, v_cb0 = k_src
    nqb = t // tq
    hps = 2 if all(v % 2 == 0 for v in (n_heads, q_cb0, k_cb0, v_cb0)) else 1
    w = hps * hd
    return pl.pallas_call(
        functools.partial(_stick_kernel, tq=tq, tk=tk, q_off=q_off, hd=hd, hps=hps),
        grid=(n_batch, n_heads // hps, nqb),
        in_specs=[pl.BlockSpec((tq, w), lambda b, h, i: (q_rb0 + b * nqb + i, q_cb0 // hps + h)),
                  pl.BlockSpec((s_pad, w), lambda b, h, i: (b, k_cb0 // hps + h)),
                  pl.BlockSpec((s_pad, w), lambda b, h, i: (b, v_cb0 // hps + h))],
        out_specs=pl.BlockSpec((tq, w), lambda b, h, i: (b * nqb + i, h)),
        out_shape=jax.ShapeDtypeStruct((n_batch * t, n_heads * hd), BF16),
        scratch_shapes=[pltpu.VMEM((hps, tq, 1), F32), pltpu.VMEM((hps, tq, hd), F32)],
        compiler_params=_cparams(3),
        name="stick_attn",
    )(qarr, karr, varr)


def _diff_kernel(scal_ref, q_ref, k_ref, v_ref, bd_ref, g_ref, o_ref, m_scr, acc_scr, *,
                 tq, tk, q_off, s_valid, hd, out_scale):
    h = pl.program_id(1)
    i = pl.program_id(2)
    reps = tk // LANE
    q0 = q_off + i * tq
    kb_diag = q0 // tk
    q_pos = q0 + lax.broadcasted_iota(I32, (tq, 1), 0)
    lane = lax.broadcasted_iota(I32, (1, tk), 1)
    tile = lambda x: jnp.concatenate([x] * reps, axis=1)
    scale2 = (hd ** -0.5) * LOG2E
    m_scr[...] = jnp.full(m_scr.shape, MASKED, F32)
    acc_scr[...] = jnp.zeros(acc_scr.shape, F32)
    q_maps = [q_ref[:, m * hd:(m + 1) * hd] for m in range(2)]
    ones = jnp.ones((tk, hd), BF16)

    def attend_block(kb, near):
        ks = pl.multiple_of(kb * tk, tk)
        vext = jnp.concatenate([v_ref[pl.ds(ks, tk), :], ones], axis=1)
        if near is not None:
            k_pos = ks + lane
            allowed = ((k_pos >> CHUNK_SHIFT) <= (q_pos >> CHUNK_SHIFT)) & (k_pos < s_valid)
            mbias = jnp.where(allowed, 0.0, MASKED)
        for m in range(2):
            s = _dot_nt(q_maps[m], k_ref[pl.ds(ks, tk), m * hd:(m + 1) * hd]) * scale2
            m_old = m_scr[m]
            if near is None:
                far = scal_ref[1 + 2 * h + m]
                m_new = jnp.maximum(m_old, jnp.max(s, axis=-1, keepdims=True) + far)
                p = jnp.exp2(s - tile(m_new - far))
            else:
                t = s + (mbias + bd_ref[m, near])
                m_new = jnp.maximum(m_old, jnp.max(t, axis=-1, keepdims=True))
                p = jnp.exp2(t - tile(m_new))
            alpha = jnp.exp2(m_old - m_new)
            acc_scr[m] = (jnp.concatenate([alpha] * 3, axis=1) * acc_scr[m]
                          + jnp.dot(p.astype(BF16), vext, preferred_element_type=F32))
            m_scr[m] = m_new

    n_far = jnp.maximum(kb_diag - 1, 0)

    def far_pair(pair, carry):
        attend_block(2 * pair, None)
        attend_block(2 * pair + 1, None)
        return carry

    lax.fori_loop(0, n_far // 2, far_pair, 0)

    @pl.when(n_far % 2 == 1)
    def _():
        attend_block(n_far - 1, None)

    @pl.when(kb_diag >= 1)
    def _():
        attend_block(kb_diag - 1, 1)

    attend_block(kb_diag, 0)

    def normalised(m):
        acc = acc_scr[m]
        return acc[:, :2 * hd] / jnp.concatenate([acc[:, 2 * hd:]] * 2, axis=1)

    o = normalised(0) - scal_ref[0] * normalised(1)
    o = o * lax.rsqrt(jnp.mean(o * o, axis=-1, keepdims=True) + LN_EPS) * g_ref[...]
    o_ref[...] = (o * out_scale).astype(BF16)


def _diff_attn(q_src, k_src, bias_near, scalars, subln_g, *, n_batch, t, s_pad, s_valid, q_off,
               tq, tk, n_heads, hd, out_scale):
    qarr, q_rb0, q_cb0 = q_src
    karr, k_cb0, varr, v_cb0 = k_src
    assert tk >= MAX_DISTANCE and tk % tq == 0 and q_off % tq == 0 and s_pad % tk == 0
    nqb = t // tq
    w = 2 * hd
    return pl.pallas_call(
        functools.partial(_diff_kernel, tq=tq, tk=tk, q_off=q_off, s_valid=s_valid, hd=hd,
                          out_scale=out_scale),
        grid=(n_batch, n_heads, nqb),
        in_specs=[pl.BlockSpec(memory_space=pltpu.SMEM),
                  pl.BlockSpec((tq, w), lambda b, h, i: (q_rb0 + b * nqb + i, q_cb0 + h)),
                  pl.BlockSpec((s_pad, w), lambda b, h, i: (b, k_cb0 + h)),
                  pl.BlockSpec((s_pad, w), lambda b, h, i: (b, v_cb0 + h)),
                  pl.BlockSpec((None, None, 2, 2, tq, tk),
                               lambda b, h, i: (_near_phase(q_off, tq, tk, nqb, i), h, 0, 0, 0, 0)),
                  pl.BlockSpec((1, w), lambda b, h, i: (0, 0))],
        out_specs=pl.BlockSpec((tq, w), lambda b, h, i: (b * nqb + i, h)),
        out_shape=jax.ShapeDtypeStruct((n_batch * t, n_heads * w), BF16),
        scratch_shapes=[pltpu.VMEM((2, tq, LANE), F32), pltpu.VMEM((2, tq, w + hd), F32)],
        compiler_params=_cparams(3),
        name="diff_attn",
    )(scalars, qarr, karr, varr, bias_near, subln_g)


def _router_kernel(x_ref, wh_ref, wl_ref, b_ref, e_ref, g_ref, r_ref, c_ref, seen_scr):
    @pl.when(pl.program_id(0) == 0)
    def _():
        seen_scr[...] = jnp.zeros(seen_scr.shape, F32)

    x = x_ref[...]
    xh = x.astype(BF16)
    xl = (x - xh.astype(F32)).astype(BF16)
    wh = wh_ref[...]
    logits = (jnp.dot(xh, wh, preferred_element_type=F32)
              + jnp.dot(xl, wh, preferred_element_type=F32)
              + jnp.dot(xh, wl_ref[...], preferred_element_type=F32)) + b_ref[...]
    tm = logits.shape[0]
    lane = lax.broadcasted_iota(I32, logits.shape, 1)
    lane_f = lane.astype(F32)
    experts = jnp.zeros(logits.shape, I32)
    gates = jnp.zeros(logits.shape, F32)
    chosen = []
    top = None
    for k in range(TOP_K):
        m = jnp.max(logits, axis=-1, keepdims=True)
        ix = jnp.min(jnp.where(logits == m, lane_f, float(LANE)), axis=-1, keepdims=True).astype(I32)
        top = m if top is None else top
        experts = jnp.where(lane == k, ix, experts)
        gates = jnp.where(lane == k, jnp.exp(m - top), gates)
        chosen.append(lane == ix)
        logits = jnp.where(chosen[-1], -jnp.inf, logits)
    e_ref[...] = experts
    g_ref[...] = gates / jnp.sum(gates, axis=-1, keepdims=True)

    hot = functools.reduce(jnp.logical_or, chosen)
    hot_f = jnp.where(hot, 1.0, 0.0)
    lower = (lax.broadcasted_iota(I32, (tm, tm), 0) > lax.broadcasted_iota(I32, (tm, tm), 1)).astype(BF16)
    earlier = jnp.dot(lower, hot_f.astype(BF16), preferred_element_type=F32) + seen_scr[...]
    ranks = jnp.zeros(logits.shape, F32)
    for k in range(TOP_K):
        rank_k = jnp.sum(jnp.where(chosen[k], earlier, 0.0), axis=-1, keepdims=True)
        ranks = jnp.where(lane == k, rank_k, ranks)
    r_ref[...] = ranks.astype(I32)
    seen = seen_scr[...] + jnp.sum(hot_f, axis=0, keepdims=True)
    seen_scr[...] = seen
    c_ref[...] = seen


def _router(x, w_hi, w_lo, bias):
    m, d = x.shape
    tm = _pick_tile(m, (256, 128, 64, 32, 16))
    row = lambda i: (i, 0)
    fix = lambda i: (0, 0)
    return pl.pallas_call(
        _router_kernel,
        grid=(m // tm,),
        in_specs=[pl.BlockSpec((tm, d), row), pl.BlockSpec((d, LANE), fix),
                  pl.BlockSpec((d, LANE), fix), pl.BlockSpec((1, LANE), fix)],
        out_specs=[pl.BlockSpec((tm, LANE), row), pl.BlockSpec((tm, LANE), row),
                   pl.BlockSpec((tm, LANE), row), pl.BlockSpec((1, LANE), fix)],
        out_shape=[jax.ShapeDtypeStruct((m, LANE), I32), jax.ShapeDtypeStruct((m, LANE), F32),
                   jax.ShapeDtypeStruct((m, LANE), I32), jax.ShapeDtypeStruct((1, LANE), F32)],
        scratch_shapes=[pltpu.VMEM((1, LANE), F32)],
        compiler_params=_cparams(1),
        name="router",
    )(x, w_hi, w_lo, bias)


def _dispatch_kernel(dest_ref, x_ref, rows_in, rows_out, sem, *, tm):
    del rows_in

    def send(r, carry):
        for k in range(TOP_K):
            _row_copy(x_ref, r, rows_out, dest_ref[0, r * TOP_K + k], sem).start()
        return carry
    lax.fori_loop(0, tm, send, 0)

    def drain(r, carry):
        for k in range(TOP_K):
            _row_copy(x_ref, 0, rows_out, 0, sem).wait()
        return carry
    lax.fori_loop(0, tm, drain, 0)


def _dispatch(x_packed, dest, rows_buf):
    m, w = x_packed.shape
    tm = _pick_tile(m, (256, 128, 64, 32, 16))
    n_steps = m // tm
    return pl.pallas_call(
        functools.partial(_dispatch_kernel, tm=tm),
        grid=(n_steps,),
        in_specs=[pl.BlockSpec((None, 1, tm * TOP_K), lambda i: (i, 0, 0), memory_space=pltpu.SMEM),
                  pl.BlockSpec((tm, w), lambda i: (i, 0)),
                  pl.BlockSpec(memory_space=pl.ANY)],
        out_specs=pl.BlockSpec(memory_space=pl.ANY),
        out_shape=jax.ShapeDtypeStruct(rows_buf.shape, I32),
        scratch_shapes=[pltpu.SemaphoreType.DMA(())],
        input_output_aliases={2: 0},
        compiler_params=_cparams(1),
        name="dispatch",
    )(dest.reshape(n_steps, 1, tm * TOP_K), x_packed, rows_buf)


def _cast_rows(src_ref, dst_ref, rows_per_step):
    def body(c, carry):
        r = pl.multiple_of(c * rows_per_step, rows_per_step)
        dst_ref[pl.ds(r, rows_per_step), :] = src_ref[pl.ds(r, rows_per_step), :].astype(BF16)
        return carry
    lax.fori_loop(0, src_ref.shape[0] // rows_per_step, body, 0)


def _expert_changed(be_ref, j):
    return (j == 0) | (be_ref[j] != be_ref[jnp.maximum(j - 1, 0)])


def _moe_up_kernel(be_ref, nu_ref, x_ref, wg_ref, wl_ref, bg_ref, bl_ref, h_ref, wg_bf, wl_bf, *,
                   cast_rows):
    j = pl.program_id(1)

    @pl.when(_expert_changed(be_ref, j))
    def _():
        _cast_rows(wg_ref, wg_bf, cast_rows)
        _cast_rows(wl_ref, wl_bf, cast_rows)

    def compute(n):
        x = _unpack_halves(x_ref[:n])
        glu = jnp.dot(x, wg_bf[...], preferred_element_type=F32) + bg_ref[...]
        lin = jnp.dot(x, wl_bf[...], preferred_element_type=F32) + bl_ref[...]
        glu = jnp.minimum(glu, SWIGLU_LIMIT)
        lin = jnp.clip(lin, -SWIGLU_LIMIT, SWIGLU_LIMIT)
        h_ref[:n] = (glu * jax.nn.sigmoid(SWIGLU_ALPHA * glu) * (lin + 1.0)).astype(BF16)

    _by_block_fill(nu_ref[1 + j], h_ref, compute)


def _by_block_fill(rows, out_ref, compute):
    tm = out_ref.shape[0]
    half = tm // 2

    @pl.when(rows > half)
    def _():
        compute(tm)

    @pl.when((rows > 0) & (rows <= half))
    def _():
        compute(half)
        out_ref[half:] = jnp.zeros((tm - half,) + out_ref.shape[1:], out_ref.dtype)

    @pl.when(rows == 0)
    def _():
        out_ref[...] = jnp.zeros(out_ref.shape, out_ref.dtype)


def _moe_down_kernel(be_ref, nu_ref, h_ref, wd_ref, bd_ref, y_ref, wd_bf, *, cast_rows):
    j = pl.program_id(1)

    @pl.when(_expert_changed(be_ref, j))
    def _():
        _cast_rows(wd_ref, wd_bf, cast_rows)

    def compute(n):
        y_ref[:n] = jnp.dot(h_ref[:n], wd_bf[...], preferred_element_type=F32) + bd_ref[...]

    _by_block_fill(nu_ref[1 + j], y_ref, compute)


def _moe_experts(xs, block_e, n_used, layer, w_gate_up, b_gate_up, w_down, b_down, tm):
    n_rows = xs.shape[0]
    d = 2 * xs.shape[1]
    n_blocks = n_rows // tm
    d_ff = w_down.shape[2]
    tf = _pick_tile(d_ff, (1024, 512, 256, 128))
    nf = d_ff // tf
    tn = _pick_tile(d, (1024, 512, 256, 128))
    blk = lambda j, nu: jnp.minimum(j, nu[0] - 1)
    hidden = pl.pallas_call(
        functools.partial(_moe_up_kernel, cast_rows=_pick_tile(d, (256, 128, 64, 32, 16, 8))),
        grid_spec=pltpu.PrefetchScalarGridSpec(
            num_scalar_prefetch=2,
            grid=(nf, n_blocks),
            in_specs=[pl.BlockSpec((tm, d // 2), lambda f, j, be, nu: (blk(j, nu), 0)),
                      pl.BlockSpec((None, None, d, tf), lambda f, j, be, nu: (layer, be[j], 0, f)),
                      pl.BlockSpec((None, None, d, tf), lambda f, j, be, nu: (layer, be[j], 0, nf + f)),
                      pl.BlockSpec((None, None, 1, tf), lambda f, j, be, nu: (layer, be[j], 0, f)),
                      pl.BlockSpec((None, None, 1, tf), lambda f, j, be, nu: (layer, be[j], 0, nf + f))],
            out_specs=pl.BlockSpec((tm, tf), lambda f, j, be, nu: (j, f)),
            scratch_shapes=[pltpu.VMEM((d, tf), BF16), pltpu.VMEM((d, tf), BF16)]),
        out_shape=jax.ShapeDtypeStruct((n_rows, d_ff), BF16),
        compiler_params=_cparams(2),
        name="moe_up",
    )(block_e, n_used, xs, w_gate_up, w_gate_up, b_gate_up, b_gate_up)
    return pl.pallas_call(
        functools.partial(_moe_down_kernel, cast_rows=_pick_tile(d_ff, (256, 128, 64, 32, 16, 8))),
        grid_spec=pltpu.PrefetchScalarGridSpec(
            num_scalar_prefetch=2,
            grid=(d // tn, n_blocks),
            in_specs=[pl.BlockSpec((tm, d_ff), lambda n, j, be, nu: (blk(j, nu), 0)),
                      pl.BlockSpec((None, None, d_ff, tn), lambda n, j, be, nu: (layer, be[j], 0, n)),
                      pl.BlockSpec((None, None, 1, tn), lambda n, j, be, nu: (layer, be[j], 0, n))],
            out_specs=pl.BlockSpec((tm, tn), lambda n, j, be, nu: (j, n)),
            scratch_shapes=[pltpu.VMEM((d_ff, tn), BF16)]),
        out_shape=jax.ShapeDtypeStruct((n_rows, d), F32),
        compiler_params=_cparams(2),
        name="moe_down",
    )(block_e, n_used, hidden, w_down, b_down)


def _moe(x_f32, x_packed, rows_buf, layer, w_router_hi, w_router_lo, b_router, w_gate_up,
         b_gate_up, w_down, b_down):
    n = x_f32.shape[0]
    n_experts = w_gate_up.shape[1]
    tm = MOE_BLOCK_ROWS[0] if n * TOP_K >= 1024 * n_experts else MOE_BLOCK_ROWS[1]
    experts, gates, ranks, counts = _router(x_f32, w_router_hi, w_router_lo, b_router)
    counts = counts[0, :n_experts].astype(I32)
    pend = jnp.cumsum((counts + tm - 1) // tm * tm)
    pstart = pend - (counts + tm - 1) // tm * tm
    n_blocks = -(-(n * TOP_K + n_experts * (tm - 1)) // tm)
    block_e = jnp.minimum(jnp.sum(pend[None, :] <= (jnp.arange(n_blocks) * tm)[:, None], axis=1),
                          n_experts - 1).astype(I32)
    first_row = jnp.arange(n_blocks, dtype=I32) * tm
    block_rows = jnp.clip(counts[block_e] - (first_row - pstart[block_e].astype(I32)), 0, tm)
    block_rows = jnp.where(first_row < pend[-1], block_rows, 0).astype(I32)
    n_used = jnp.concatenate([(pend[-1] // tm).astype(I32).reshape(1), block_rows])
    e4 = experts[:, :TOP_K]
    onehot = e4[..., None] == jnp.arange(n_experts, dtype=I32)
    dest = (ranks[:, :TOP_K] + jnp.sum(jnp.where(onehot, pstart.astype(I32), 0), axis=-1)).reshape(-1)
    if rows_buf is None:
        rows_buf = jnp.zeros((n_blocks * tm,) + x_packed.shape[1:], I32)
    xs = _dispatch(x_packed, dest, rows_buf)
    y = _moe_experts(xs, block_e, n_used, layer, w_gate_up, b_gate_up, w_down, b_down, tm)
    return y, dest, gates, xs


def _pad_cols(w, n):
    return jnp.pad(w, ((0, 0), (0, n - w.shape[1])))


def _keys_with_past(past, new, s_pad):
    b, p, w = past.shape
    rows = jnp.pad(past.astype(BF16), ((0, 0), (0, s_pad - p), (0, 0)))
    rows = lax.dynamic_update_slice(rows, new.astype(BF16), (0, p, 0))
    return rows.reshape(b * s_pad, w)


def kernel(x_prompt, x_sample, cache_a_k, cache_a_v, cache_a_kidx, cache_b_k, cache_b_v,
           cache_c_k, cache_c_v, cache_mem_k, cache_mem_v, mem_prompt,
           w_in_a, w_in_b, w_in_c, diff_lambda, diff_subln, rel_bias, w_mem_kv, w_o,
           ln_g, ln_b, w_router, b_router, w_gate_up, b_gate_up, w_down, b_down):
    bp, tp, d = x_prompt.shape
    bs, ts, _ = x_sample.shape
    depth = w_o.shape[0]
    past_len = cache_a_k.shape[2]
    n_kv, hd = cache_a_k.shape[3], cache_a_k.shape[4]
    d_idx = cache_a_kidx.shape[3]
    n_mem, h_mem = cache_mem_k.shape[2], cache_mem_k.shape[3]
    mem_w = h_mem * hd
    tok_w = d - mem_w
    h_a = tok_w // hd
    h_b = cache_b_k.shape[3]
    hc = cache_c_k.shape[3]
    n_idx = (w_in_a.shape[2] - tok_w - 2 * n_kv * hd - d_idx - mem_w) // (d_idx + 1)
    n_experts = w_router.shape[2]
    alpha = (2 * depth) ** 0.25
    mp, ms = bp * tp, bs * ts
    s_new = past_len + ts
    kv_w, qi_w = n_kv * hd, n_idx * d_idx

    o_k, o_v, o_qi = tok_w, tok_w + kv_w, tok_w + 2 * kv_w
    o_ki = o_qi + qi_w
    o_wi = o_ki + d_idx
    o_mq = o_wi + n_idx
    a_qi = tok_w
    a_mq = a_qi + qi_w
    a_kw = a_mq + mem_w
    bf = lambda w: w.astype(BF16)
    w_a = [bf(_pad_cols(jnp.concatenate([w[:, :o_k], w[:, o_qi:o_ki], w[:, o_mq:], w[:, o_ki:o_mq]],
                                        axis=1), _round_up(a_kw + LANE, 7 * LANE if d >= 7 * LANE else LANE)))
           for w in w_in_a]
    w_a_k = [bf(w[:, o_k:o_v]) for w in w_in_a]
    w_a_v = [bf(w[:, o_v:o_qi]) for w in w_in_a]
    w_ki = [bf(w[:, o_ki:o_wi]) for w in w_in_a]
    main_bc = lambda w: bf(jnp.concatenate([w[:, :tok_w], w[:, 3 * tok_w:]], axis=1))
    w_b, w_c = [main_bc(w) for w in w_in_b], [main_bc(w) for w in w_in_c]
    w_b_k, w_b_v = [bf(w[:, tok_w:2 * tok_w]) for w in w_in_b], [bf(w[:, 2 * tok_w:3 * tok_w]) for w in w_in_b]
    w_c_k, w_c_v = [bf(w[:, tok_w:2 * tok_w]) for w in w_in_c], [bf(w[:, 2 * tok_w:3 * tok_w]) for w in w_in_c]
    w_o_bf = w_o.astype(BF16)
    w_r_pad = jnp.pad(w_router, ((0, 0), (0, 0), (0, LANE - n_experts)))
    w_r_hi = w_r_pad.astype(BF16)
    w_r_lo = (w_r_pad - w_r_hi.astype(F32)).astype(BF16)
    b_r_pad = jnp.pad(b_router, ((0, 0), (0, LANE - n_experts)), constant_values=MASKED)[:, None, :]
    b_gu = b_gate_up[:, :, None, :]
    b_dn = b_down[:, :, None, :]

    tq_a = _pick_tile(tp, (128, 64))
    tk_a = _pick_tile(tp, (512, 256, 128))
    tq_b = _pick_tile(tp, (256, 128, 64))
    tk_b = min(tq_b, 256)
    tq_c = _pick_tile(tp, (512, 256, 128))
    tk_c = _pick_tile(tp, (512, 256, 128))
    tq_m = _pick_tile(tp, (512, 256, 128))
    sp_a, sp_b, sp_c = _round_up(s_new, tk_a), _round_up(s_new, tk_b), _round_up(s_new, tk_c)

    far2 = _far_bias(rel_bias) * LOG2E
    near_a_p = _near_bias(rel_bias, tq_a, tk_a, 0, tp // tq_a) * LOG2E
    near_a_s = _near_bias(rel_bias, ts, tk_a, past_len, 1) * LOG2E
    near_c_p = _near_bias(rel_bias, tq_c, tk_c, 0, tp // tq_c) * LOG2E
    near_c_s = _near_bias(rel_bias, ts, tk_c, past_len, 1) * LOG2E
    pair = lambda nb: nb.reshape(nb.shape[0], hc, 2, *nb.shape[2:])

    w_mem = jnp.moveaxis(w_mem_kv, 0, 1).reshape(d, depth * 2 * mem_w).astype(BF16)
    mem_f, mem_b = _matmul(mem_prompt.reshape(bp * n_mem, d).astype(BF16), w_mem)
    mem_f = mem_f.reshape(bp, n_mem, depth, 2, h_mem, hd)
    mem_k_p = jnp.moveaxis(mem_f[:, :, :, 0], 2, 0)
    mem_v_p = jnp.moveaxis(mem_f[:, :, :, 1], 2, 0)
    mem_s_k = cache_mem_k.reshape(depth, bs * n_mem, mem_w).astype(BF16)
    mem_s_v = cache_mem_v.reshape(depth, bs * n_mem, mem_w).astype(BF16)

    x_f = jnp.concatenate([x_prompt.reshape(mp, d), x_sample.reshape(ms, d)], axis=0)
    x_b = x_f.astype(BF16)
    rows_p = [None] * depth
    rows_s = [None] * depth
    rows_buf = None

    for i in range(depth):
        kind, j = i % N_MIXERS, i // N_MIXERS
        h_b_ = _matmul_bf16(x_b, (w_a, w_b, w_c)[kind][j])
        w_k, w_v = ((w_a_k, w_a_v), (w_b_k, w_b_v), (w_c_k, w_c_v))[kind]
        k_shape = ((n_kv, hd), (h_b, hd), (hc, 2, hd))[kind]
        v_shape = ((n_kv, hd), (h_b, hd), (hc, 2 * hd))[kind]
        kp_f, kp_b = _proj_heads(x_b, 0, mp, w_k[j], k_shape)
        vp_f, vp_b = _proj_heads(x_b, 0, mp, w_v[j], v_shape)
        ks_f, ks_b = _proj_heads(x_b, mp, ms, w_k[j], k_shape)
        vs_f, vs_b = _proj_heads(x_b, mp, ms, w_v[j], v_shape)
        k_new, v_new = ks_b.reshape(bs, ts, -1), vs_b.reshape(bs, ts, -1)
        rows_p[i] = (kp_f.reshape(bp, tp, *k_shape), vp_f.reshape(bp, tp, *v_shape))
        rows_s[i] = (ks_f.reshape(bs, ts, *k_shape), vs_f.reshape(bs, ts, *v_shape))
        rb_s = mp // ts

        if kind == 0:
            k_sel_p = min(K_SEL_MAX, tp // 4)
            k_sel_s = min(K_SEL_MAX, s_new // 4)
            dims = dict(n_heads=h_a, n_kv=n_kv, hd=hd, n_idx=n_idx, d_idx=d_idx)
            mix_p = _dsa_attn(
                (h_b_, 0, 0, a_qi // qi_w, a_kw // LANE),
                (kp_b, 0, vp_b, 0, h_b_, a_kw // LANE),
                near_a_p, far2, n_batch=bp, t=tp, s_pad=tp, s_valid=tp, q_off=0,
                tq=tq_a, tk=tk_a, k_sel=k_sel_p, **dims)
            ki_f, ki_b = _matmul(x_b, w_ki[j])
            kiw_new = jnp.pad(ki_b[mp:].reshape(bs, ts, d_idx), ((0, 0), (0, 0), (0, LANE - d_idx)))
            kiw_past = jnp.pad(cache_a_kidx[j], ((0, 0), (0, 0), (0, LANE - d_idx)))
            k_all = _keys_with_past(cache_a_k[j].reshape(bs, past_len, kv_w), k_new, sp_a)
            v_all = _keys_with_past(cache_a_v[j].reshape(bs, past_len, kv_w), v_new, sp_a)
            ki_all = _keys_with_past(kiw_past, kiw_new, sp_a)
            mix_s = _dsa_attn(
                (h_b_, rb_s, 0, a_qi // qi_w, a_kw // LANE),
                (k_all, 0, v_all, 0, ki_all, 0),
                near_a_s, far2, n_batch=bs, t=ts, s_pad=sp_a, s_valid=s_new, q_off=past_len,
                tq=ts, tk=tk_a, k_sel=k_sel_s, **dims)
            mq_cb = a_mq // mem_w
            rows_p[i] += (ki_f[:mp].reshape(bp, tp, d_idx),)
            rows_s[i] += (ki_f[mp:].reshape(bs, ts, d_idx),)
        elif kind == 1:
            dims = dict(n_heads=h_b, hd=hd)
            mix_p = _stick_attn((h_b_, 0, 0), (kp_b, 0, vp_b, 0),
                                n_batch=bp, t=tp, s_pad=tp, q_off=0, tq=tq_b, tk=tk_b, **dims)
            k_all = _keys_with_past(cache_b_k[j].reshape(bs, past_len, tok_w), k_new, sp_b)
            v_all = _keys_with_past(cache_b_v[j].reshape(bs, past_len, tok_w), v_new, sp_b)
            mix_s = _stick_attn((h_b_, rb_s, 0), (k_all, 0, v_all, 0),
                                n_batch=bs, t=ts, s_pad=sp_b, q_off=past_len, tq=ts, tk=tk_b, **dims)
            mq_cb = tok_w // mem_w
        else:
            lambda_init = 0.8 - 0.6 * math.exp(-0.3 * i)
            lv = diff_lambda[j].astype(F32)
            lam = jnp.exp(jnp.sum(lv[0] * lv[1])) - jnp.exp(jnp.sum(lv[2] * lv[3])) + lambda_init
            scalars = jnp.concatenate([lam.reshape(1), far2]).astype(F32)
            g = diff_subln[j].reshape(1, 2 * hd).astype(F32)
            dims = dict(n_heads=hc, hd=hd, out_scale=1.0 - lambda_init)
            mix_p = _diff_attn((h_b_, 0, 0), (kp_b, 0, vp_b, 0), pair(near_c_p), scalars, g,
                               n_batch=bp, t=tp, s_pad=tp, s_valid=tp, q_off=0, tq=tq_c, tk=tk_c, **dims)
            k_all = _keys_with_past(cache_c_k[j].reshape(bs, past_len, tok_w), k_new, sp_c)
            v_all = _keys_with_past(cache_c_v[j].reshape(bs, past_len, tok_w), v_new, sp_c)
            mix_s = _diff_attn((h_b_, rb_s, 0), (k_all, 0, v_all, 0), pair(near_c_s), scalars, g,
                               n_batch=bs, t=ts, s_pad=sp_c, s_valid=s_new, q_off=past_len,
                               tq=ts, tk=tk_c, **dims)
            mq_cb = tok_w // mem_w

        mem_p = _mem_attn(h_b_, 0, mq_cb, mem_b[:, (2 * i) * mem_w:(2 * i + 1) * mem_w],
                          mem_b[:, (2 * i + 1) * mem_w:(2 * i + 2) * mem_w],
                          bp, tp, tq_m, h_mem, hd)
        mem_s = _mem_attn(h_b_, rb_s, mq_cb, mem_s_k[i], mem_s_v[i], bs, ts, ts, h_mem, hd)
        mix = jnp.concatenate([mix_p, mix_s], axis=0)
        mem_o = jnp.concatenate([mem_p, mem_s], axis=0)
        x_f, x_pk = _proj_ln(mix, mem_o, w_o_bf[i, :tok_w], w_o_bf[i, tok_w:], x_f,
                             ln_g[i, 0][None], ln_b[i, 0][None], alpha)
        y, dest, gates, rows_buf = _moe(x_f, x_pk, rows_buf, i, w_r_hi[i], w_r_lo[i], b_r_pad[i],
                                        w_gate_up, b_gu, w_down, b_dn)
        x_f, x_b = _combine_ln(x_f, y, dest, gates, ln_g[i, 1][None], ln_b[i, 1][None], alpha)

    def stack_kind(rows, kind):
        sel = [r for i, r in enumerate(rows) if i % N_MIXERS == kind]
        return [jnp.stack(parts) for parts in zip(*sel)]

    y_prompt = x_f[:mp].reshape(bp, tp, d)
    y_sample = x_f[mp:].reshape(bs, ts, d)
    a_k_p, a_v_p, a_kidx_p = stack_kind(rows_p, 0)
    b_k_p, b_v_p = stack_kind(rows_p, 1)
    c_k_p, c_v_p = stack_kind(rows_p, 2)
    a_k_s, a_v_s, a_kidx_s = stack_kind(rows_s, 0)
    b_k_s, b_v_s = stack_kind(rows_s, 1)
    c_k_s, c_v_s = stack_kind(rows_s, 2)
    return (y_prompt, y_sample, a_k_p, a_v_p, a_kidx_p, b_k_p, b_v_p, c_k_p, c_v_p, mem_k_p, mem_v_p,
            a_k_s, a_v_s, a_kidx_s, b_k_s, b_v_s, c_k_s, c_v_s)
```
